```python
import math
import jax
import jax.numpy as jnp
from jax import lax
import numpy as np

D_MODEL = 2048
BATCH = 16
SEQ = 2048
DEPTH = 4

GRID_W = 64
CTX_LEN = 256
N_MIXERS = 2
N_A_LAYERS = (DEPTH + 1) // 2
N_B_LAYERS = DEPTH // 2
MLP_HIDDEN = 4 * D_MODEL
N_MOD = 6
NORM_EPS = 1e-6

RW_HEAD = 64
RW_HEADS = D_MODEL // RW_HEAD
RW_DECAY_LORA = max(32, int(round(1.8 * D_MODEL ** 0.5 / 32)) * 32)
RW_AAA_LORA = max(32, int(round(1.8 * D_MODEL ** 0.5 / 32)) * 32)
RW_MV_LORA = max(32, int(round(1.3 * D_MODEL ** 0.5 / 32)) * 32)
RW_GATE_LORA = max(32, int(round(0.6 * D_MODEL ** 0.8 / 32)) * 32)
RW_GN_EPS = 64e-5
RW_L2_EPS = 1e-12

MLA_NOPE = 128
MLA_ROPE = 64
MLA_V = 128
MLA_HEADS = D_MODEL // 128
MLA_Q_LORA = D_MODEL // 4
MLA_KV_LORA = D_MODEL // 4
MLA_SCALE = 1.0 / math.sqrt(MLA_NOPE + MLA_ROPE)
ROPE_THETA = 10000.0
Q_BLOCK = 128

kernel_name = 'hybrid_rwkv7_mla_dit'


def rms_norm(x, g):
    xf = x.astype(jnp.float32)
    y = xf * lax.rsqrt(jnp.mean(xf * xf, axis=-1, keepdims=True) + NORM_EPS)
    return (y * g.astype(jnp.float32)).astype(x.dtype)


def modulate(h, g, shift, scale):
    return rms_norm(h, g) * (1.0 + scale) + shift


def sq_relu_mlp(u, w1, w2):
    return jnp.square(jax.nn.relu(u @ w1)) @ w2


def axial_rope(n):
    rows = n // GRID_W
    row = jnp.broadcast_to(jnp.arange(rows)[:, None], (rows, GRID_W)).reshape(-1)
    col = jnp.broadcast_to(jnp.arange(GRID_W)[None, :], (rows, GRID_W)).reshape(-1)
    n_freq = MLA_ROPE // 4
    inv = ROPE_THETA ** (-jnp.arange(n_freq, dtype=jnp.float32) / n_freq)
    ang = jnp.concatenate([row[:, None].astype(jnp.float32) * inv,
                           col[:, None].astype(jnp.float32) * inv], axis=-1)
    return jnp.cos(ang), jnp.sin(ang)


def apply_rope(x, cos, sin):
    xf = x.astype(jnp.float32).reshape(x.shape[:-1] + (x.shape[-1] // 2, 2))
    x0, x1 = xf[..., 0], xf[..., 1]
    out = jnp.stack([x0 * cos - x1 * sin, x0 * sin + x1 * cos], axis=-1)
    return out.reshape(x.shape).astype(x.dtype)


def centred_shift(x):
    pad = jnp.zeros_like(x[:, :1])
    prev = jnp.concatenate([pad, x[:, :-1]], axis=1)
    nxt = jnp.concatenate([x[:, 1:], pad], axis=1)
    return 0.5 * (prev + nxt) - x


def to_heads(t):
    return t.astype(jnp.float32).reshape(t.shape[:-1] + (RW_HEADS, RW_HEAD))


def rwkv_proj(h, pw, v_first, vres):
    mu, wr, wk, wv, w0, w1, w2, a0, a1, a2, g1, g2, kkp, ka = pw
    xx = centred_shift(h)
    xr, xw, xk, xv, xa, xg = [h + xx * mu[j] for j in range(6)]
    r = xr @ wr
    k = xk @ wk
    v = xv @ wv
    if vres is not None:
        v0, v1, v2 = vres
        v = v + (v_first - v) * jax.nn.sigmoid(v0 + (xv @ v1) @ v2)
    g = jax.nn.sigmoid(xg @ g1) @ g2
    w_lora = jnp.einsum('ebtr,erd->ebtd', jnp.tanh(jnp.einsum('btd,edr->ebtr', xw, w1)), w2)
    logw = -jax.nn.softplus(-(w0[:, None, None, :] + w_lora).astype(jnp.float32)) - 0.5
    decay = jnp.exp(-jnp.exp(logw))
    a_lora = jnp.einsum('ebtr,erd->ebtd', jnp.einsum('btd,edr->ebtr', xa, a1), a2)
    a = jax.nn.sigmoid((a0[:, None, None, :] + a_lora).astype(jnp.float32))
    kk = to_heads(k * kkp)
    kk = kk / jnp.maximum(jnp.linalg.norm(kk, axis=-1, keepdims=True), RW_L2_EPS)
    a_h = to_heads(a)
    k_dir = to_heads(k)[None] * (1.0 + (a_h - 1.0) * to_heads(ka))
    return to_heads(r), k_dir, to_heads(v), kk, to_heads(decay), a_h, g, v


def wkv_scan(r, w, k, v, kk, a, reverse):
    B, T, H, N = r.shape

    def step(S, inp):
        r_t, w_t, k_t, v_t, kk_t, a_t = inp
        s_kk = jnp.einsum('bhvk,bhk->bhv', S, kk_t)
        S = (S * w_t[:, :, None, :] - s_kk[..., None] * (kk_t * a_t)[:, :, None, :]
             + v_t[..., None] * k_t[:, :, None, :])
        return S, jnp.einsum('bhvk,bhk->bhv', S, r_t)

    xs = tuple(jnp.swapaxes(t, 0, 1) for t in (r, w, k, v, kk, a))
    _, y = lax.scan(step, jnp.zeros((B, H, N, N), jnp.float32), xs, reverse=reverse)
    return jnp.swapaxes(y, 0, 1)


def rwkv_out(y, r, k_dir, v, g, rk, lnx_g, lnx_b, wo):
    B, T = y.shape[:2]
    mu = jnp.mean(y, axis=-1, keepdims=True)
    var = jnp.mean(jnp.square(y - mu), axis=-1, keepdims=True)
    yn = ((y - mu) * lax.rsqrt(var + RW_GN_EPS)).reshape(B, T, D_MODEL) * lnx_g + lnx_b
    bonus = jnp.sum(r[None] * k_dir * rk, axis=(0, -1))[..., None] * v
    o = (yn + bonus.reshape(B, T, D_MODEL)).astype(g.dtype) * g
    return o @ wo


def rwkv_mixer(u_lat, u_ctx, v_first_lat, v_first_ctx, pw, vres, rk, lnx_g, lnx_b, wo, need_ctx):
    rl, kl, vl, kkl, dl, al, gl, v_lat = rwkv_proj(u_lat, pw, v_first_lat, vres)
    rc, kc, vc, kkc, dc, ac, gc, v_ctx = rwkv_proj(u_ctx, pw, v_first_ctx, vres)
    C = u_ctx.shape[1]
    S = u_lat.shape[1]

    def cat(p, q):
        return jnp.concatenate([p, q], axis=1)

    yf = wkv_scan(cat(rc, rl), cat(dc[0], dl[0]), cat(kc[0], kl[0]), cat(vc, vl), cat(kkc, kkl),
                  cat(ac[0], al[0]), reverse=False)
    yb = wkv_scan(cat(rl, rc), cat(dl[1], dc[1]), cat(kl[1], kc[1]), cat(vl, vc), cat(kkl, kkc),
                  cat(al[1], ac[1]), reverse=True)
    o_lat = rwkv_out(yf[:, C:] + yb[:, :S], rl, kl, vl, gl, rk, lnx_g, lnx_b, wo)
    o_ctx = None
    if need_ctx:
        o_ctx = rwkv_out(yf[:, :C] + yb[:, S:], rc, kc, vc, gc, rk, lnx_g, lnx_b, wo)
    return o_lat, o_ctx, v_lat, v_ctx


def mla_q(h, wdq, qnorm, wuq, qn_nope, qn_rope, rope):
    B, T, _ = h.shape
    q = (rms_norm(h @ wdq, qnorm) @ wuq).reshape(B, T, MLA_HEADS, MLA_NOPE + MLA_ROPE)
    q_nope = rms_norm(q[..., :MLA_NOPE], qn_nope)
    q_rope = rms_norm(q[..., MLA_NOPE:], qn_rope)
    if rope is not None:
        q_rope = apply_rope(q_rope, rope[0][:, None, :], rope[1][:, None, :])
    return q_nope, q_rope


def mla_kv(h, wdkv, kvnorm, wukv, kn_nope, kn_rope, rope):
    B, T, _ = h.shape
    ckv = h @ wdkv
    c_kv = rms_norm(ckv[..., :MLA_KV_LORA], kvnorm)
    k_rope = rms_norm(ckv[..., MLA_KV_LORA:], kn_rope)
    kv = (c_kv @ wukv).reshape(B, T, MLA_HEADS, MLA_NOPE + MLA_V)
    k_nope = rms_norm(kv[..., :MLA_NOPE], kn_nope)
    v = kv[..., MLA_NOPE:]
    if rope is not None:
        k_rope = apply_rope(k_rope, rope[0], rope[1])
    return k_nope, k_rope, v


def mla_attend(q_nope, q_rope, k_nope, k_rope, v):
    s = (jnp.einsum('bqhd,bkhd->bhqk', q_nope, k_nope, preferred_element_type=jnp.float32)
         + jnp.einsum('bqhr,bkr->bhqk', q_rope, k_rope, preferred_element_type=jnp.float32)) * MLA_SCALE
    p = jax.nn.softmax(s, axis=-1).astype(v.dtype)
    return jnp.einsum('bhqk,bkhd->bqhd', p, v)


def mla_mixer(u_lat, u_ctx, rope, wdq, qnorm, wuq, wdkv, kvnorm, wukv, qn_nope, qn_rope,
              kn_nope, kn_rope, wo, need_ctx):
    B, S, _ = u_lat.shape
    C = u_ctx.shape[1]
    kn_l, kr_l, v_l = mla_kv(u_lat, wdkv, kvnorm, wukv, kn_nope, kn_rope, rope)
    kn_c, kr_c, v_c = mla_kv(u_ctx, wdkv, kvnorm, wukv, kn_nope, kn_rope, None)
    k_nope = jnp.concatenate([kn_l, kn_c], axis=1)
    k_rope = jnp.concatenate([kr_l, kr_c], axis=1)
    v = jnp.concatenate([v_l, v_c], axis=1)
    qn_l, qr_l = mla_q(u_lat, wdq, qnorm, wuq, qn_nope, qn_rope, rope)
    nb = S // Q_BLOCK

    def blk(t):
        return jnp.moveaxis(t.reshape((B, nb, Q_BLOCK) + t.shape[2:]), 1, 0)

    o = lax.map(lambda q: mla_attend(q[0], q[1], k_nope, k_rope, v), (blk(qn_l), blk(qr_l)))
    o_lat = jnp.moveaxis(o, 0, 1).reshape(B, S, MLA_HEADS * MLA_V) @ wo
    o_ctx = None
    if need_ctx:
        qn_c, qr_c = mla_q(u_ctx, wdq, qnorm, wuq, qn_nope, qn_rope, None)
        o_ctx = mla_attend(qn_c, qr_c, kn_c, kr_c, v_c).reshape(B, C, MLA_HEADS * MLA_V) @ wo
    return o_lat, o_ctx


def setup_inputs(seed: int = 0):
    key = jax.random.key(seed)
    keys = jax.random.split(key, 64)
    counter = [0]
    f32 = jnp.float32

    def nk():
        k = keys[counter[0]]
        counter[0] += 1
        return k

    def nrm(shape, scale):
        return jax.random.normal(nk(), shape, f32) * scale

    def gain(shape):
        return 1.0 + 0.05 * jax.random.normal(nk(), shape, f32)

    D = D_MODEL
    NA = N_A_LAYERS
    NB = N_B_LAYERS
    NV = max(NA - 1, 0)
    return {
        'x': nrm((BATCH, SEQ, D), 1.0),
        'c': nrm((BATCH, D), 1.0),
        'ctx': nrm((BATCH, CTX_LEN, D), 1.0),
        'c_ctx': nrm((D,), 1.0),
        'mod_w': nrm((DEPTH, D, N_MOD * D), 0.5 * D ** -0.5),
        'mod_b': nrm((DEPTH, N_MOD * D), 0.01),
        'norm_g': gain((DEPTH, 2, D)),
        'mlp_w1': nrm((DEPTH, D, MLP_HIDDEN), D ** -0.5),
        'mlp_w2': nrm((DEPTH, MLP_HIDDEN, D), MLP_HIDDEN ** -0.5),
        'rw_mu': jax.random.uniform(nk(), (NA, 6, D), f32),
        'rw_wr': nrm((NA, D, D), D ** -0.5),
        'rw_wk': nrm((NA, D, D), D ** -0.5),
        'rw_wv': nrm((NA, D, D), D ** -0.5),
        'rw_wo': nrm((NA, D, D), D ** -0.5),
        'rw_w0': jax.random.uniform(nk(), (NA, 2, D), f32, minval=-5.0, maxval=1.0),
        'rw_w1': nrm((NA, 2, D, RW_DECAY_LORA), D ** -0.5),
        'rw_w2': nrm((NA, 2, RW_DECAY_LORA, D), 0.1 * RW_DECAY_LORA ** -0.5),
        'rw_a0': nrm((NA, 2, D), 0.1),
        'rw_a1': nrm((NA, 2, D, RW_AAA_LORA), D ** -0.5),
        'rw_a2': nrm((NA, 2, RW_AAA_LORA, D), 0.1 * RW_AAA_LORA ** -0.5),
        'rw_g1': nrm((NA, D, RW_GATE_LORA), D ** -0.5),
        'rw_g2': nrm((NA, RW_GATE_LORA, D), RW_GATE_LORA ** -0.5),
        'rw_kk': 0.85 + 0.05 * jax.random.normal(nk(), (NA, D), f32),
        'rw_ka': gain((NA, D)),
        'rw_rk': nrm((NA, RW_HEADS, RW_HEAD), 0.1),
        'rw_lnx_g': gain((NA, D)),
        'rw_lnx_b': nrm((NA, D), 0.01),
        'rw_v0': nrm((NV, D), 0.1),
        'rw_v1': nrm((NV, D, RW_MV_LORA), D ** -0.5),
        'rw_v2': nrm((NV, RW_MV_LORA, D), 0.1 * RW_MV_LORA ** -0.5),
        'mla_wdq': nrm((NB, D, MLA_Q_LORA), D ** -0.5),
        'mla_qnorm': gain((NB, MLA_Q_LORA)),
        'mla_wuq': nrm((NB, MLA_Q_LORA, MLA_HEADS * (MLA_NOPE + MLA_ROPE)), MLA_Q_LORA ** -0.5),
        'mla_wdkv': nrm((NB, D, MLA_KV_LORA + MLA_ROPE), D ** -0.5),
        'mla_kvnorm': gain((NB, MLA_KV_LORA)),
        'mla_wukv': nrm((NB, MLA_KV_LORA, MLA_HEADS * (MLA_NOPE + MLA_V)), MLA_KV_LORA ** -0.5),
        'mla_qn_nope': gain((NB, MLA_NOPE)),
        'mla_qn_rope': gain((NB, MLA_ROPE)),
        'mla_kn_nope': gain((NB, MLA_NOPE)),
        'mla_kn_rope': gain((NB, MLA_ROPE)),
        'mla_wo': nrm((NB, MLA_HEADS * MLA_V, D), (MLA_HEADS * MLA_V) ** -0.5),
    }


def reference(x, c, ctx, c_ctx, mod_w, mod_b, norm_g, mlp_w1, mlp_w2,
              rw_mu, rw_wr, rw_wk, rw_wv, rw_wo, rw_w0, rw_w1, rw_w2, rw_a0, rw_a1, rw_a2,
              rw_g1, rw_g2, rw_kk, rw_ka, rw_rk, rw_lnx_g, rw_lnx_b, rw_v0, rw_v1, rw_v2,
              mla_wdq, mla_qnorm, mla_wuq, mla_wdkv, mla_kvnorm, mla_wukv,
              mla_qn_nope, mla_qn_rope, mla_kn_nope, mla_kn_rope, mla_wo):
    B, S, D = x.shape
    cos, sin = axial_rope(S)
    sc = jax.nn.silu(c)
    sc_ctx = jax.nn.silu(c_ctx)
    h_lat, h_ctx = x, ctx
    v_first_lat = None
    v_first_ctx = None
    for i in range(DEPTH):
        last = i == DEPTH - 1
        j = i // N_MIXERS
        mod_l = (sc @ mod_w[i] + mod_b[i]).reshape(B, N_MOD, 1, D)
        mod_c = (sc_ctx @ mod_w[i] + mod_b[i]).reshape(1, N_MOD, 1, D)
        u_lat = modulate(h_lat, norm_g[i, 0], mod_l[:, 0], mod_l[:, 1])
        u_ctx = modulate(h_ctx, norm_g[i, 0], mod_c[:, 0], mod_c[:, 1])
        if i % N_MIXERS == 0:
            pw = (rw_mu[j], rw_wr[j], rw_wk[j], rw_wv[j], rw_w0[j], rw_w1[j], rw_w2[j],
                  rw_a0[j], rw_a1[j], rw_a2[j], rw_g1[j], rw_g2[j], rw_kk[j], rw_ka[j])
            vres = None if j == 0 else (rw_v0[j - 1], rw_v1[j - 1], rw_v2[j - 1])
            o_lat, o_ctx, v_lat, v_ctx = rwkv_mixer(u_lat, u_ctx, v_first_lat, v_first_ctx, pw, vres,
                                                    rw_rk[j], rw_lnx_g[j], rw_lnx_b[j], rw_wo[j],
                                                    not last)
            if j == 0:
                v_first_lat, v_first_ctx = v_lat, v_ctx
        else:
            o_lat, o_ctx = mla_mixer(u_lat, u_ctx, (cos, sin), mla_wdq[j], mla_qnorm[j], mla_wuq[j],
                                     mla_wdkv[j], mla_kvnorm[j], mla_wukv[j], mla_qn_nope[j],
                                     mla_qn_rope[j], mla_kn_nope[j], mla_kn_rope[j], mla_wo[j],
                                     not last)
        h_lat = h_lat + mod_l[:, 2] * o_lat
        h_lat = h_lat + mod_l[:, 5] * sq_relu_mlp(
            modulate(h_lat, norm_g[i, 1], mod_l[:, 3], mod_l[:, 4]), mlp_w1[i], mlp_w2[i])
        if not last:
            h_ctx = h_ctx + mod_c[:, 2] * o_ctx
            h_ctx = h_ctx + mod_c[:, 5] * sq_relu_mlp(
                modulate(h_ctx, norm_g[i, 1], mod_c[:, 3], mod_c[:, 4]), mlp_w1[i], mlp_w2[i])
    return h_lat
```

```python
import functools
import math

import jax
import jax.numpy as jnp
from jax import lax
from jax.experimental import pallas as pl
from jax.experimental.pallas import tpu as pltpu

F32 = jnp.float32
BF16 = jnp.bfloat16

NORM_EPS = 1e-6
N_MOD = 6
GRID_W = 64
RW_HEAD = 64
RW_GN_EPS = 64e-5
RW_L2_EPS = 1e-12
MLA_NOPE = 128
MLA_ROPE = 64
MLA_V = 128
ROPE_THETA = 10000.0

LANES = 128
WKV_CHUNK = 64
VMEM_LIMIT_BYTES = 48 * 1024 * 1024

_NT = (((1,), (1,)), ((), ()))
_TN = (((0,), (0,)), ((), ()))


def _pick(n, prefs):
    for p in prefs:
        if n % p == 0:
            return p
    return n


def _params(sem):
    return pltpu.CompilerParams(dimension_semantics=sem, vmem_limit_bytes=VMEM_LIMIT_BYTES)


class _Layout:
    def __init__(self, B, C, S, with_ctx=True):
        self.B, self.C, self.S = B, C, S
        self.n_ctx = B * C if with_ctx else 0
        self.M = self.n_ctx + B * S

    def row_tile(self):
        tm = _pick(math.gcd(self.n_ctx, self.S) if self.n_ctx else self.S, (1024, 512, 256, 128, 64, 32, 16, 8))
        return tm

    def mod_row(self, i, tm):
        n_ctx_tiles = self.n_ctx // tm
        per_b = self.S // tm
        lat = 1 + (i - n_ctx_tiles) // per_b
        if n_ctx_tiles == 0:
            return lat
        return jnp.where(i < n_ctx_tiles, 0, lat)


def _mm_body(*refs, nk, act, has_gate):
    if has_gate:
        x_ref, w_ref, res_ref, gate_ref, o_ref, *scratch = refs
    else:
        x_ref, w_ref, o_ref, *scratch = refs

    def finish(acc):
        if act == "relu2":
            acc = jnp.square(jnp.maximum(acc, 0.0))
        elif act == "sigmoid":
            acc = jax.nn.sigmoid(acc)
        elif act == "tanh":
            acc = jnp.tanh(acc)
        if has_gate:
            acc = res_ref[...] + gate_ref[0] * acc
        o_ref[...] = acc.astype(o_ref.dtype)

    part = jnp.dot(x_ref[...].astype(BF16), w_ref[...].astype(BF16), preferred_element_type=F32)
    if nk == 1:
        finish(part)
    else:
        acc_ref = scratch[0]
        k = pl.program_id(2)

        @pl.when(k == 0)
        def _():
            acc_ref[...] = part

        @pl.when(k > 0)
        def _():
            acc_ref[...] += part

        @pl.when(k == nk - 1)
        def _():
            finish(acc_ref[...])


def _mm(x, w, *, act=None, out_dtype=F32, res=None, gate=None, lay=None, tm=None, name="mm"):
    M, K = x.shape
    _, N = w.shape
    if tm is None:
        tm = lay.row_tile() if lay is not None else _pick(M, (1024, 512, 256, 128, 64, 32, 16, 8))
    tn = _pick(N, (1024, 512, 256, 128))
    tk = K if K <= 2048 else _pick(K, (2048, 1024, 512))
    nk = K // tk
    has_gate = gate is not None
    in_specs = [pl.BlockSpec((tm, tk), lambda j, i, k: (i, k)),
                pl.BlockSpec((tk, tn), lambda j, i, k: (k, j))]
    args = [x, w]
    if has_gate:
        in_specs += [pl.BlockSpec((tm, tn), lambda j, i, k: (i, j)),
                     pl.BlockSpec((1, 1, tn), lambda j, i, k: (lay.mod_row(i, tm), 0, j))]
        args += [res, gate.reshape(gate.shape[0], 1, N)]
    return pl.pallas_call(
        functools.partial(_mm_body, nk=nk, act=act, has_gate=has_gate),
        grid=(N // tn, M // tm, nk),
        in_specs=in_specs,
        out_specs=pl.BlockSpec((tm, tn), lambda j, i, k: (i, j)),
        out_shape=jax.ShapeDtypeStruct((M, N), out_dtype),
        scratch_shapes=[pltpu.VMEM((tm, tn), F32)] if nk > 1 else [],
        compiler_params=_params(("parallel", "parallel", "arbitrary")),
        name=name,
    )(*args)


def _norm_mod_body(x_ref, g_ref, mod_ref, o_ref, *, shift_row, scale_row):
    x = x_ref[...]
    ms = jnp.mean(x * x, axis=-1, keepdims=True)
    y = x * lax.rsqrt(ms + NORM_EPS) * g_ref[...]
    mod = mod_ref[0]
    out = y * (1.0 + mod[scale_row:scale_row + 1]) + mod[shift_row:shift_row + 1]
    o_ref[...] = out.astype(o_ref.dtype)


def _norm_mod(x, g, mod, lay, *, shift_row, scale_row, out_dtype):
    M, D = x.shape
    tm = min(lay.row_tile(), 512)
    return pl.pallas_call(
        functools.partial(_norm_mod_body, shift_row=shift_row, scale_row=scale_row),
        grid=(M // tm,),
        in_specs=[pl.BlockSpec((tm, D), lambda i: (i, 0)),
                  pl.BlockSpec((1, D), lambda i: (0, 0)),
                  pl.BlockSpec((1, N_MOD, D), lambda i: (lay.mod_row(i, tm), 0, 0))],
        out_specs=pl.BlockSpec((tm, D), lambda i: (i, 0)),
        out_shape=jax.ShapeDtypeStruct((M, D), out_dtype),
        compiler_params=_params(("parallel",)),
        name="norm_mod",
    )(x, g.reshape(1, D), mod)


def _wkv_body(r_ref, lw_ref, k_ref, v_ref, kk_ref, a_ref, y_ref, s_ref, *, reverse, npairs):
    L = WKV_CHUNK
    H = RW_HEAD

    @pl.when(pl.program_id(2) == 0)
    def _():
        s_ref[...] = jnp.zeros_like(s_ref)

    def order(row, col):
        return (row <= col) if reverse else (row >= col)

    row = lax.broadcasted_iota(jnp.int32, (L, L), 0)
    col = lax.broadcasted_iota(jnp.int32, (L, L), 1)
    tri = jnp.where(order(row, col), 1.0, 0.0).astype(BF16)

    prow = lax.broadcasted_iota(jnp.int32, (L, 2 * L), 0)
    pcol = lax.broadcasted_iota(jnp.int32, (L, 2 * L), 1) & (L - 1)
    incl = order(prow, pcol)
    strict = incl & (prow != pcol)
    eye = jnp.where(prow == pcol, 1.0, 0.0)

    def sibling(s):
        return ((prow // (2 * s)) == (pcol // (2 * s))) & ((prow // s) != (pcol // s))

    brow = lax.broadcasted_iota(jnp.int32, (2 * L, LANES), 0) // L
    bcol = lax.broadcasted_iota(jnp.int32, (2 * L, LANES), 1) // H
    bmask = brow == bcol
    smask = (lax.broadcasted_iota(jnp.int32, (LANES, LANES), 0) // H
             == lax.broadcasted_iota(jnp.int32, (LANES, LANES), 1) // H)

    def bd(x):
        return jnp.where(bmask, jnp.concatenate([x, x], axis=0), 0.0).astype(BF16)

    def dot(a, b):
        return jnp.dot(a, b, preferred_element_type=F32)

    for p in range(npairs):
        sl = slice(p * LANES, (p + 1) * LANES)
        r = r_ref[:, sl]
        lw = lw_ref[:, sl]
        k = k_ref[:, sl]
        v = v_ref[:, sl]
        kk = kk_ref[:, sl]
        a = a_ref[:, sl]

        hi = lw.astype(BF16)
        rem = lw - hi.astype(F32)
        mid = rem.astype(BF16)
        lo = (rem - mid.astype(F32)).astype(BF16)
        g = dot(tri, hi) + dot(tri, mid) + dot(tri, lo)
        gend = jnp.sum(lw, axis=0, keepdims=True)

        eg = jnp.exp(g)
        en = jnp.exp(-g)
        ee = jnp.exp(gend)
        abar = -(kk * jnp.exp(g - lw))
        rbar = r * eg
        bt = kk * a * en
        kt = k * en
        bh = bt * ee
        kh = kt * ee

        lhs = jnp.concatenate([abar, rbar], axis=0).astype(BF16)
        rhs = jnp.concatenate([bd(bt), bd(kt)], axis=0)
        amat = lax.dot_general(lhs, rhs, _NT, preferred_element_type=F32)
        a_ab = jnp.where(strict, amat[:L, :2 * L], 0.0)
        a_ak = jnp.where(strict, amat[:L, 2 * L:], 0.0)
        a_rb = jnp.where(incl, amat[L:, :2 * L], 0.0)
        a_rk = jnp.where(incl, amat[L:, 2 * L:], 0.0)

        t = eye + jnp.where(sibling(1), a_ab, 0.0)
        s = 2
        while s < L:
            a_l = jnp.where(sibling(s), a_ab, 0.0)
            t = t + dot(dot(t.astype(BF16), bd(a_l)).astype(BF16), bd(t))
            s *= 2

        av = dot(jnp.concatenate([a_ak, a_rk], axis=0).astype(BF16), bd(v))
        tb = t.astype(BF16)
        wt = dot(tb, bd(abar))
        ut = dot(tb, bd(av[:L]))

        s0 = s_ref[p]
        wr = lax.dot_general(jnp.concatenate([wt, rbar], axis=0).astype(BF16), s0.astype(BF16), _NT,
                             preferred_element_type=F32)
        u = wr[:L] + ut
        y = wr[L:] + dot(a_rb.astype(BF16), bd(u)) + av[L:]
        upd = lax.dot_general(jnp.concatenate([u, v], axis=0).astype(BF16),
                              jnp.concatenate([bh, kh], axis=0).astype(BF16), _TN,
                              preferred_element_type=F32)
        s_ref[p] = s0 * ee + jnp.where(smask, upd, 0.0)
        y_ref[:, sl] = y


def _wkv(r, lw, k, v, kk, a, lay, *, reverse):
    M, D = r.shape
    L = WKV_CHUNK
    td = min(D, 4 * LANES)
    npairs = td // LANES
    ctx_chunks = lay.C // L
    lat_chunks = lay.S // L
    nchunks = ctx_chunks + lat_chunks
    ctx_blocks = lay.n_ctx // L

    def tok_block(b, c):
        if reverse:
            ctx_c = ctx_chunks - 1 - c
            lat_c = lat_chunks - 1 - (c - ctx_chunks)
        else:
            ctx_c = c
            lat_c = c - ctx_chunks
        return jnp.where(c < ctx_chunks, b * ctx_chunks + ctx_c, ctx_blocks + b * lat_chunks + lat_c)

    spec = pl.BlockSpec((L, td), lambda b, d, c: (tok_block(b, c), d))
    return pl.pallas_call(
        functools.partial(_wkv_body, reverse=reverse, npairs=npairs),
        grid=(lay.B, D // td, nchunks),
        in_specs=[spec] * 6,
        out_specs=spec,
        out_shape=jax.ShapeDtypeStruct((M, D), F32),
        scratch_shapes=[pltpu.VMEM((npairs, LANES, LANES), F32)],
        compiler_params=_params(("parallel", "parallel", "arbitrary")),
        name="wkv_bwd" if reverse else "wkv_fwd",
    )(r, lw, k, v, kk, a)


def _attn_body(q_ref, k_ref, v_ref, o_ref, *, scale):
    s = lax.dot_general(q_ref[0], k_ref[0], _NT, preferred_element_type=F32) * scale
    m = jnp.max(s, axis=-1, keepdims=True)
    p = jnp.exp(s - m)
    l = jnp.sum(p, axis=-1, keepdims=True)
    o = jnp.dot(p.astype(BF16), v_ref[0], preferred_element_type=F32)
    o_ref[0] = (o / l).astype(o_ref.dtype)


def _attn(q, k, v, heads, scale):
    B, Tq, _ = q.shape
    Tk = k.shape[1]
    dk = q.shape[2] // heads
    dv = v.shape[2] // heads
    tq = _pick(Tq, (512, 256, 128, 64, 32, 16, 8))
    return pl.pallas_call(
        functools.partial(_attn_body, scale=scale),
        grid=(B, heads, Tq // tq),
        in_specs=[pl.BlockSpec((1, tq, dk), lambda b, h, i: (b, i, h)),
                  pl.BlockSpec((1, Tk, dk), lambda b, h, i: (b, 0, h)),
                  pl.BlockSpec((1, Tk, dv), lambda b, h, i: (b, 0, h))],
        out_specs=pl.BlockSpec((1, tq, dv), lambda b, h, i: (b, i, h)),
        out_shape=jax.ShapeDtypeStruct((B, Tq, heads * dv), BF16),
        compiler_params=_params(("parallel", "parallel", "arbitrary")),
        name="attn",
    )(q, k, v)


def _rms(x, g):
    return x * lax.rsqrt(jnp.mean(x * x, axis=-1, keepdims=True) + NORM_EPS) * g


def _pad_cols(w, n):
    return jnp.pad(w, ((0, 0), (0, n - w.shape[1])))


def _pad_rows(w, n):
    return jnp.pad(w, ((0, n - w.shape[0]), (0, 0)))


def _up128(n):
    return -(-n // LANES) * LANES


def _split(lay, t):
    tail = t.shape[1:]
    return (t[:lay.n_ctx].reshape((lay.B, lay.C) + tail), t[lay.n_ctx:].reshape((lay.B, lay.S) + tail))


def _join(ctx, lat):
    tail = ctx.shape[2:]
    return jnp.concatenate([ctx.reshape((-1,) + tail), lat.reshape((-1,) + tail)], axis=0)


def _centred_shift(z):
    pad = jnp.zeros_like(z[:, :1])
    prev = jnp.concatenate([pad, z[:, :-1]], axis=1)
    nxt = jnp.concatenate([z[:, 1:], pad], axis=1)
    return 0.5 * (prev + nxt) - z


def _axial_rope(n):
    rows = n // GRID_W
    row = jnp.broadcast_to(jnp.arange(rows)[:, None], (rows, GRID_W)).reshape(-1)
    col = jnp.broadcast_to(jnp.arange(GRID_W)[None, :], (rows, GRID_W)).reshape(-1)
    n_freq = MLA_ROPE // 4
    inv = ROPE_THETA ** (-jnp.arange(n_freq, dtype=F32) / n_freq)
    ang = jnp.concatenate([row[:, None].astype(F32) * inv, col[:, None].astype(F32) * inv], axis=-1)
    return jnp.cos(ang), jnp.sin(ang)


def _apply_rope(x, cos, sin):
    xf = x.reshape(x.shape[:-1] + (x.shape[-1] // 2, 2))
    x0, x1 = xf[..., 0], xf[..., 1]
    out = jnp.stack([x0 * cos - x1 * sin, x0 * sin + x1 * cos], axis=-1)
    return out.reshape(x.shape)


def _rwkv_mixer(u, lay, v_first, pw, vres, rk, lnx_g, lnx_b):
    mu, wr, wk, wv, w0, w1, w2, a0, a1, a2, g1, g2, kkp, ka = pw
    M, D = u.shape
    heads = D // RW_HEAD
    u_ctx, u_lat = _split(lay, u)
    xx = _join(_centred_shift(u_ctx), _centred_shift(u_lat))
    xr, xw, xk, xv, xa, xg = [(u + xx * mu[j]).astype(BF16) for j in range(6)]
    mm = functools.partial(_mm, lay=lay)
    r = mm(xr, wr.astype(BF16), name="rw_r")
    k = mm(xk, wk.astype(BF16), name="rw_k")
    v = mm(xv, wv.astype(BF16), name="rw_v")
    if vres is not None:
        v0, v1, v2 = vres
        n = _up128(v1.shape[1])
        lora = mm(mm(xv, _pad_cols(v1, n).astype(BF16), out_dtype=BF16, name="rw_v1"),
                  _pad_rows(v2, n).astype(BF16), name="rw_v2")
        v = v + (v_first - v) * jax.nn.sigmoid(v0 + lora)
    g = mm(mm(xg, g1.astype(BF16), act="sigmoid", out_dtype=BF16, name="rw_g1"), g2.astype(BF16), name="rw_g2")

    kk = (k * kkp).reshape(M, heads, RW_HEAD)
    kk = (kk / jnp.maximum(jnp.sqrt(jnp.sum(kk * kk, axis=-1, keepdims=True)), RW_L2_EPS)).reshape(M, D)

    y = None
    bonus = None
    for e in range(2):
        n = _up128(w1.shape[2])
        w_lora = mm(mm(xw, _pad_cols(w1[e], n).astype(BF16), act="tanh", out_dtype=BF16, name="rw_w1"),
                    _pad_rows(w2[e], n).astype(BF16), name="rw_w2")
        logw = -jax.nn.softplus(-(w0[e] + w_lora)) - 0.5
        log_decay = -jnp.exp(logw)
        n = _up128(a1.shape[2])
        a_lora = mm(mm(xa, _pad_cols(a1[e], n).astype(BF16), out_dtype=BF16, name="rw_a1"),
                    _pad_rows(a2[e], n).astype(BF16), name="rw_a2")
        a = jax.nn.sigmoid(a0[e] + a_lora)
        k_dir = k * (1.0 + (a - 1.0) * ka)
        y_e = _wkv(r, log_decay, k_dir, v, kk, a, lay, reverse=(e == 1))
        b_e = jnp.sum((r * k_dir).reshape(M, heads, RW_HEAD) * rk, axis=-1, keepdims=True)
        y = y_e if y is None else y + y_e
        bonus = b_e if bonus is None else bonus + b_e

    yh = y.reshape(M, heads, RW_HEAD)
    mean = jnp.mean(yh, axis=-1, keepdims=True)
    var = jnp.mean(jnp.square(yh - mean), axis=-1, keepdims=True)
    yn = ((yh - mean) * lax.rsqrt(var + RW_GN_EPS)).reshape(M, D) * lnx_g + lnx_b
    o = (yn + (bonus * v.reshape(M, heads, RW_HEAD)).reshape(M, D)) * g
    return o.astype(BF16), v


def _mla_mixer(u, lay, rope, wdq, qnorm, wuq, wdkv, kvnorm, wukv, qn_nope, qn_rope, kn_nope, kn_rope, need_ctx):
    M, D = u.shape
    heads = D // MLA_V
    kv_lora = kvnorm.shape[0]
    cos, sin = rope
    mm = functools.partial(_mm, lay=lay)

    def rope_lat(t):
        t_ctx, t_lat = _split(lay, t)
        extra = (1,) * (t.ndim - 2)
        c = cos.reshape((1, lay.S) + extra + (cos.shape[-1],))
        s = sin.reshape((1, lay.S) + extra + (sin.shape[-1],))
        return _join(t_ctx, _apply_rope(t_lat, c, s))

    ckv = mm(u, _pad_cols(wdkv, _up128(wdkv.shape[1])).astype(BF16), name="mla_dkv")
    c_kv = _rms(ckv[:, :kv_lora], kvnorm).astype(BF16)
    k_rope = rope_lat(_rms(ckv[:, kv_lora:kv_lora + MLA_ROPE], kn_rope))
    kv = mm(c_kv, wukv.astype(BF16), name="mla_ukv").reshape(M, heads, MLA_NOPE + MLA_V)
    k_nope = _rms(kv[..., :MLA_NOPE], kn_nope)
    v = kv[..., MLA_NOPE:].astype(BF16).reshape(M, heads * MLA_V)

    cq = _rms(mm(u, wdq.astype(BF16), name="mla_dq"), qnorm).astype(BF16)
    q = mm(cq, wuq.astype(BF16), name="mla_uq").reshape(M, heads, MLA_NOPE + MLA_ROPE)
    q_nope = _rms(q[..., :MLA_NOPE], qn_nope)
    q_rope = rope_lat(_rms(q[..., MLA_NOPE:], qn_rope))

    dk = 2 * LANES
    zq = jnp.zeros((M, heads, dk - MLA_NOPE - MLA_ROPE), F32)
    q_cat = jnp.concatenate([q_nope, q_rope, zq], axis=-1).astype(BF16).reshape(M, heads * dk)
    k_cat = jnp.concatenate([k_nope, jnp.broadcast_to(k_rope[:, None, :], (M, heads, MLA_ROPE)), zq],
                            axis=-1).astype(BF16).reshape(M, heads * dk)

    scale = 1.0 / math.sqrt(MLA_NOPE + MLA_ROPE)
    q_ctx, q_lat = _split(lay, q_cat)
    k_ctx, k_lat = _split(lay, k_cat)
    v_ctx, v_lat = _split(lay, v)
    o_lat = _attn(q_lat, jnp.concatenate([k_ctx, k_lat], axis=1), jnp.concatenate([v_ctx, v_lat], axis=1),
                  heads, scale)
    if not need_ctx:
        return o_lat.reshape(lay.B * lay.S, heads * MLA_V)
    o_ctx = _attn(q_ctx, k_ctx, v_ctx, heads, scale)
    return _join(o_ctx, o_lat)


def kernel(x, c, ctx, c_ctx, mod_w, mod_b, norm_g, mlp_w1, mlp_w2, rw_mu, rw_wr, rw_wk, rw_wv, rw_wo, rw_w0, rw_w1, rw_w2, rw_a0, rw_a1, rw_a2, rw_g1, rw_g2, rw_kk, rw_ka, rw_rk, rw_lnx_g, rw_lnx_b, rw_v0, rw_v1, rw_v2, mla_wdq, mla_qnorm, mla_wuq, mla_wdkv, mla_kvnorm, mla_wukv, mla_qn_nope, mla_qn_rope, mla_kn_nope, mla_kn_rope, mla_wo):
    B, S, D = x.shape
    C = ctx.shape[1]
    depth = mod_w.shape[0]
    lay = _Layout(B, C, S)
    rope = _axial_rope(S)

    sc_all = jnp.concatenate([jax.nn.silu(c_ctx)[None], jax.nn.silu(c)], axis=0)
    h = jnp.concatenate([ctx.reshape(B * C, D), x.reshape(B * S, D)], axis=0)
    v_first = None

    for i in range(depth):
        last = i == depth - 1
        j = i // 2
        mod = (_mm(sc_all, mod_w[i], name="adaln") + mod_b[i]).reshape(B + 1, N_MOD, D)
        u = _norm_mod(h, norm_g[i, 0], mod, lay, shift_row=0, scale_row=1,
                      out_dtype=F32 if i % 2 == 0 else BF16)
        if i % 2 == 0:
            pw = (rw_mu[j], rw_wr[j], rw_wk[j], rw_wv[j], rw_w0[j], rw_w1[j], rw_w2[j],
                  rw_a0[j], rw_a1[j], rw_a2[j], rw_g1[j], rw_g2[j], rw_kk[j], rw_ka[j])
            vres = None if j == 0 else (rw_v0[j - 1], rw_v1[j - 1], rw_v2[j - 1])
            o, v_cur = _rwkv_mixer(u, lay, v_first, pw, vres, rw_rk[j], rw_lnx_g[j], rw_lnx_b[j])
            if j == 0:
                v_first = v_cur
            wo = rw_wo[j]
        else:
            o = _mla_mixer(u, lay, rope, mla_wdq[j], mla_qnorm[j], mla_wuq[j], mla_wdkv[j], mla_kvnorm[j],
                           mla_wukv[j], mla_qn_nope[j], mla_qn_rope[j], mla_kn_nope[j], mla_kn_rope[j],
                           need_ctx=not last)
            wo = mla_wo[j]
        if last:
            if o.shape[0] != B * S:
                o = o[lay.n_ctx:]
            h = h[lay.n_ctx:]
            lay = _Layout(B, C, S, with_ctx=False)
        h = _mm(o, wo.astype(BF16), res=h, gate=mod[:, 2], lay=lay, name="mix_out")
        u2 = _norm_mod(h, norm_g[i, 1], mod, lay, shift_row=3, scale_row=4, out_dtype=BF16)
        hid = _mm(u2, mlp_w1[i].astype(BF16), act="relu2", out_dtype=BF16, lay=lay, name="mlp_up")
        h = _mm(hid, mlp_w2[i].astype(BF16), res=h, gate=mod[:, 5], lay=lay, name="mlp_down")
    return h.reshape(B, S, D)
```

```python
import functools
import math

import jax
import jax.numpy as jnp
from jax import lax
from jax.experimental import pallas as pl
from jax.experimental.pallas import tpu as pltpu

F32 = jnp.float32
BF16 = jnp.bfloat16

NORM_EPS = 1e-6
N_MOD = 6
GRID_W = 64
RW_HEAD = 64
RW_GN_EPS = 64e-5
RW_L2_EPS = 1e-12
MLA_NOPE = 128
MLA_ROPE = 64
MLA_V = 128
ROPE_THETA = 10000.0

LANES = 128
SUBLANES = 8
WKV_CHUNK = 64
VMEM_LIMIT_BYTES = 48 * 1024 * 1024

_NT = (((1,), (1,)), ((), ()))


def _pick(n, prefs):
    for p in prefs:
        if n % p == 0:
            return p
    return n


def _params(sem):
    return pltpu.CompilerParams(dimension_semantics=sem, vmem_limit_bytes=VMEM_LIMIT_BYTES)


class _Layout:
    def __init__(self, B, C, S, with_ctx=True):
        self.B, self.C, self.S = B, C, S
        self.n_ctx = B * C if with_ctx else 0
        self.M = self.n_ctx + B * S

    def row_tile(self):
        return _pick(math.gcd(self.n_ctx, self.S) if self.n_ctx else self.S, (1024, 512, 256, 128, 64, 32, 16, 8))

    def seq_tile(self):
        return _pick(math.gcd(self.C, self.S) if self.n_ctx else self.S, (256, 128, 64, 32, 16, 8))

    def mod_row(self, i, tm):
        n_ctx_tiles = self.n_ctx // tm
        per_b = self.S // tm
        lat = 1 + (i - n_ctx_tiles) // per_b
        if n_ctx_tiles == 0:
            return lat
        return jnp.where(i < n_ctx_tiles, 0, lat)


def _mm_body(*refs, nk, act, has_gate):
    if has_gate:
        x_ref, w_ref, res_ref, gate_ref, o_ref, *scratch = refs
    else:
        x_ref, w_ref, o_ref, *scratch = refs

    def finish(acc):
        if act == "relu2":
            acc = jnp.square(jnp.maximum(acc, 0.0))
        elif act == "sigmoid":
            acc = jax.nn.sigmoid(acc)
        elif act == "tanh":
            acc = jnp.tanh(acc)
        if has_gate:
            acc = res_ref[...] + gate_ref[0] * acc
        o_ref[...] = acc.astype(o_ref.dtype)

    part = jnp.dot(x_ref[...].astype(BF16), w_ref[...].astype(BF16), preferred_element_type=F32)
    if nk == 1:
        finish(part)
    else:
        acc_ref = scratch[0]
        k = pl.program_id(2)

        @pl.when(k == 0)
        def _():
            acc_ref[...] = part

        @pl.when(k > 0)
        def _():
            acc_ref[...] += part

        @pl.when(k == nk - 1)
        def _():
            finish(acc_ref[...])


def _mm(x, w, *, act=None, out_dtype=F32, res=None, gate=None, lay=None, tm=None, x_kblock=0, name="mm"):
    M = x.shape[0]
    K, N = w.shape
    if tm is None:
        tm = lay.row_tile() if lay is not None else _pick(M, (1024, 512, 256, 128, 64, 32, 16, 8))
    tn = _pick(N, (1024, 512, 256, 128))
    tk = K if K <= 2048 else _pick(K, (2048, 1024, 512))
    nk = K // tk
    has_gate = gate is not None
    in_specs = [pl.BlockSpec((tm, tk), lambda j, i, k: (i, k + x_kblock * nk)),
                pl.BlockSpec((tk, tn), lambda j, i, k: (k, j))]
    args = [x, w]
    if has_gate:
        in_specs += [pl.BlockSpec((tm, tn), lambda j, i, k: (i, j)),
                     pl.BlockSpec((1, 1, tn), lambda j, i, k: (lay.mod_row(i, tm), 0, j))]
        args += [res, gate.reshape(gate.shape[0], 1, N)]
    return pl.pallas_call(
        functools.partial(_mm_body, nk=nk, act=act, has_gate=has_gate),
        grid=(N // tn, M // tm, nk),
        in_specs=in_specs,
        out_specs=pl.BlockSpec((tm, tn), lambda j, i, k: (i, j)),
        out_shape=jax.ShapeDtypeStruct((M, N), out_dtype),
        scratch_shapes=[pltpu.VMEM((tm, tn), F32)] if nk > 1 else [],
        compiler_params=_params(("parallel", "parallel", "arbitrary")),
        name=name,
    )(*args)


def _norm_mod_rows(x, gain, mod, shift_row, scale_row):
    ms = jnp.mean(x * x, axis=-1, keepdims=True)
    y = x * lax.rsqrt(ms + NORM_EPS) * gain
    return y * (1.0 + mod[scale_row:scale_row + 1]) + mod[shift_row:shift_row + 1]


def _norm_mod_body(x_ref, g_ref, mod_ref, o_ref, *, shift_row, scale_row):
    o_ref[...] = _norm_mod_rows(x_ref[...], g_ref[...], mod_ref[0], shift_row, scale_row).astype(o_ref.dtype)


def _norm_mod(x, g, mod, lay, *, shift_row, scale_row, out_dtype):
    M, D = x.shape
    tm = min(lay.row_tile(), 512)
    return pl.pallas_call(
        functools.partial(_norm_mod_body, shift_row=shift_row, scale_row=scale_row),
        grid=(M // tm,),
        in_specs=[pl.BlockSpec((tm, D), lambda i: (i, 0)),
                  pl.BlockSpec((1, D), lambda i: (0, 0)),
                  pl.BlockSpec((1, N_MOD, D), lambda i: (lay.mod_row(i, tm), 0, 0))],
        out_specs=pl.BlockSpec((tm, D), lambda i: (i, 0)),
        out_shape=jax.ShapeDtypeStruct((M, D), out_dtype),
        compiler_params=_params(("parallel",)),
        name="norm_mod",
    )(x, g.reshape(1, D), mod)


def _rw_pre_body(h_ref, hp_ref, hn_ref, g_ref, mod_ref, mu_ref, *o_refs, tm, n_ctx_tiles, ctx_tiles, lat_tiles):
    i = pl.program_id(0)
    gain = g_ref[...]
    mod = mod_ref[0]
    u = _norm_mod_rows(h_ref[...], gain, mod, 0, 1)
    u_before = _norm_mod_rows(hp_ref[SUBLANES - 1:SUBLANES, :], gain, mod, 0, 1)
    u_after = _norm_mod_rows(hn_ref[0:1, :], gain, mod, 0, 1)
    in_ctx = i < n_ctx_tiles
    pos = jnp.where(in_ctx, i % ctx_tiles, (i - n_ctx_tiles) % lat_tiles)
    last = jnp.where(in_ctx, ctx_tiles - 1, lat_tiles - 1)
    u_before = jnp.where(pos == 0, 0.0, u_before)
    u_after = jnp.where(pos == last, 0.0, u_after)
    rows = lax.broadcasted_iota(jnp.int32, (tm, 1), 0)
    prev = jnp.where(rows == 0, u_before, pltpu.roll(u, 1, axis=0))
    nxt = jnp.where(rows == tm - 1, u_after, pltpu.roll(u, tm - 1, axis=0))
    xx = 0.5 * (prev + nxt) - u
    for j, o_ref in enumerate(o_refs):
        o_ref[...] = (u + xx * mu_ref[j:j + 1, :]).astype(o_ref.dtype)


def _rw_pre(h, g, mod, mu, lay):
    M, D = h.shape
    tm = lay.seq_tile()
    nmix = mu.shape[0]
    per8 = tm // SUBLANES
    last8 = M // SUBLANES - 1
    return pl.pallas_call(
        functools.partial(_rw_pre_body, tm=tm, n_ctx_tiles=lay.n_ctx // tm, ctx_tiles=max(lay.C // tm, 1),
                          lat_tiles=lay.S // tm),
        grid=(M // tm,),
        in_specs=[pl.BlockSpec((tm, D), lambda i: (i, 0)),
                  pl.BlockSpec((SUBLANES, D), lambda i: (jnp.maximum(i * per8 - 1, 0), 0)),
                  pl.BlockSpec((SUBLANES, D), lambda i: (jnp.minimum((i + 1) * per8, last8), 0)),
                  pl.BlockSpec((1, D), lambda i: (0, 0)),
                  pl.BlockSpec((1, N_MOD, D), lambda i: (lay.mod_row(i, tm), 0, 0)),
                  pl.BlockSpec((nmix, D), lambda i: (0, 0))],
        out_specs=[pl.BlockSpec((tm, D), lambda i: (i, 0))] * nmix,
        out_shape=[jax.ShapeDtypeStruct((M, D), BF16)] * nmix,
        compiler_params=_params(("parallel",)),
        name="rw_pre",
    )(h, h, h, g.reshape(1, D), mod, mu)


def _wkv_body(*refs, reverse, npairs, nsub, epilogue):
    if epilogue:
        (r_ref, k_ref, v_ref, wl_ref, al_ref, w0_ref, a0_ref, kkp_ref, ka_ref,
         alo_ref, a0o_ref, yo_ref, g_ref, rk_ref, lng_ref, lnb_ref, o_ref, s_ref) = refs
    else:
        r_ref, k_ref, v_ref, wl_ref, al_ref, w0_ref, a0_ref, kkp_ref, ka_ref, o_ref, s_ref = refs
    L = WKV_CHUNK
    H = RW_HEAD

    @pl.when(pl.program_id(2) == 0)
    def _():
        s_ref[...] = jnp.zeros_like(s_ref)

    def order(row, col):
        return (row <= col) if reverse else (row >= col)

    row = lax.broadcasted_iota(jnp.int32, (L, L), 0)
    col = lax.broadcasted_iota(jnp.int32, (L, L), 1)
    tri = jnp.where(order(row, col), 1.0, 0.0).astype(BF16)

    prow = lax.broadcasted_iota(jnp.int32, (L, 2 * L), 0)
    pcol = lax.broadcasted_iota(jnp.int32, (L, 2 * L), 1) & (L - 1)
    incl = order(prow, pcol)
    strict = incl & (prow != pcol)
    incl2 = jnp.concatenate([incl, incl], axis=1)
    eye = jnp.where(prow == pcol, 1.0, 0.0)

    def sibling(s):
        return ((prow // (2 * s)) == (pcol // (2 * s))) & ((prow // s) != (pcol // s))

    bmask = (lax.broadcasted_iota(jnp.int32, (2 * L, LANES), 0) // L
             == lax.broadcasted_iota(jnp.int32, (2 * L, LANES), 1) // H)
    head_ones = jnp.where(lax.broadcasted_iota(jnp.int32, (LANES, LANES), 0) // H
                          == lax.broadcasted_iota(jnp.int32, (LANES, LANES), 1) // H, 1.0, 0.0).astype(BF16)

    def bdf(x):
        return jnp.where(bmask, jnp.concatenate([x, x], axis=0), 0.0)

    def bd(x):
        return bdf(x).astype(BF16)

    def dot(a, b):
        return jnp.dot(a, b, preferred_element_type=F32)

    def cat0(*xs):
        return jnp.concatenate(xs, axis=0)

    def cat1(*xs):
        return jnp.concatenate(xs, axis=1)

    units = [(ci, p) for ci in range(nsub) for p in range(npairs)]
    rng = range(len(units))

    def tiles(ref):
        return [ref[ci * L:(ci + 1) * L, p * LANES:(p + 1) * LANES] for ci, p in units]

    def vecs(ref):
        return [ref[:, p * LANES:(p + 1) * LANES] for _, p in units]

    def head_sum(xs):
        x = cat0(*xs)
        hi = x.astype(BF16)
        lo = (x - hi.astype(F32)).astype(BF16)
        tot = dot(hi, head_ones) + dot(lo, head_ones)
        return [tot[i * L:(i + 1) * L] for i in rng]

    r, k, v = tiles(r_ref), tiles(k_ref), tiles(v_ref)
    ka = vecs(ka_ref)

    a = [jax.nn.sigmoid(a0 + al) for a0, al in zip(vecs(a0_ref), tiles(al_ref))]
    z = [-(w0 + wl) for w0, wl in zip(vecs(w0_ref), tiles(wl_ref))]
    softplus = [jnp.maximum(x, 0.0) + jnp.log(1.0 + jnp.exp(-jnp.abs(x))) for x in z]
    lw = [-jnp.exp(-x - 0.5) for x in softplus]
    kraw = [k[i] * kkp for i, kkp in zip(rng, vecs(kkp_ref))]
    norm2 = head_sum([x * x for x in kraw])
    kk = [kraw[i] / jnp.maximum(jnp.sqrt(norm2[i]), RW_L2_EPS) for i in rng]
    kd = [k[i] * (1.0 + (a[i] - 1.0) * ka[i]) for i in rng]

    hi = [x.astype(BF16) for x in lw]
    rem = [lw[i] - hi[i].astype(F32) for i in rng]
    mid = [x.astype(BF16) for x in rem]
    lo = [(rem[i] - mid[i].astype(F32)).astype(BF16) for i in rng]
    g = [dot(tri, hi[i]) + dot(tri, mid[i]) + dot(tri, lo[i]) for i in rng]
    ee = [jnp.exp(jnp.sum(x, axis=0, keepdims=True)) for x in lw]

    en = [jnp.exp(-x) for x in g]
    abar = [-(kk[i] * jnp.exp(g[i] - lw[i])) for i in rng]
    rbar = [r[i] * jnp.exp(g[i]) for i in rng]
    bt = [kk[i] * a[i] * en[i] for i in rng]
    kt = [kd[i] * en[i] for i in rng]

    amat = [lax.dot_general(cat0(abar[i], rbar[i]).astype(BF16), cat0(bd(bt[i]), bd(kt[i])), _NT,
                            preferred_element_type=F32) for i in rng]
    a_ab = [jnp.where(strict, x[:L, :2 * L], 0.0) for x in amat]
    a_ak = [jnp.where(strict, x[:L, 2 * L:], 0.0).astype(BF16) for x in amat]
    a_rbk = [jnp.where(incl2, x[L:], 0.0).astype(BF16) for x in amat]

    t = [eye + jnp.where(sibling(1), x, 0.0) for x in a_ab]
    s = 2
    while s < L:
        sib = sibling(s)
        half = [dot(t[i].astype(BF16), bd(jnp.where(sib, a_ab[i], 0.0))).astype(BF16) for i in rng]
        t = [t[i] + dot(half[i], bd(t[i])) for i in rng]
        s *= 2

    vbd = [bd(x) for x in v]
    av = [dot(a_ak[i], vbd[i]) for i in rng]
    wu = [dot(t[i].astype(BF16), cat1(bd(abar[i]), bd(av[i]))) for i in rng]
    wr_lhs = [cat0(wu[i][:, :LANES], rbar[i]).astype(BF16) for i in rng]
    ut = [x[:, LANES:] for x in wu]
    c2_lhs = [cat0(a_rbk[i], cat1(bdf(bt[i] * ee[i]).T, bdf(kt[i] * ee[i]).T).astype(BF16)) for i in rng]
    gam = [jnp.broadcast_to(x, (LANES, LANES)).T for x in ee]

    state = [s_ref[p] for p in range(npairs)]
    y = [None] * len(units)
    for ci in (reversed(range(nsub)) if reverse else range(nsub)):
        ids = [ci * npairs + p for p in range(npairs)]
        wr = [dot(wr_lhs[i], state[p].astype(BF16)) for p, i in enumerate(ids)]
        u = [wr[p][:L] + ut[i] for p, i in enumerate(ids)]
        out2 = [dot(c2_lhs[i], cat0(bd(u[p]), vbd[i])) for p, i in enumerate(ids)]
        for p, i in enumerate(ids):
            y[i] = wr[p][L:] + out2[p][:L]
            state[p] = state[p] * gam[i] + out2[p][L:]
    for p in range(npairs):
        s_ref[p] = state[p]

    def store(vals):
        for (ci, p), val in zip(units, vals):
            o_ref[ci * L:(ci + 1) * L, p * LANES:(p + 1) * LANES] = val.astype(o_ref.dtype)

    if not epilogue:
        store(y)
        return

    inv_n = 1.0 / H
    ytot = [y[i] + yo for i, yo in zip(rng, tiles(yo_ref))]
    mean = head_sum(ytot)
    dev = [ytot[i] - mean[i] * inv_n for i in rng]
    var = head_sum([x * x for x in dev])
    a_o = [jax.nn.sigmoid(a0 + al) for a0, al in zip(vecs(a0o_ref), tiles(alo_ref))]
    kd_sum = [kd[i] + k[i] * (1.0 + (a_o[i] - 1.0) * ka[i]) for i in rng]
    bonus = head_sum([r[i] * rk * kd_sum[i] for i, rk in zip(rng, vecs(rk_ref))])
    lng, lnb, gate = vecs(lng_ref), vecs(lnb_ref), tiles(g_ref)
    store([(dev[i] * lax.rsqrt(var[i] * inv_n + RW_GN_EPS) * lng[i] + lnb[i] + bonus[i] * v[i]) * gate[i]
           for i in rng])


def _wkv(r, k, v, wl, al, w0, a0, kkp, ka, lay, *, reverse, epilogue=None):
    M, D = r.shape
    L = WKV_CHUNK
    td = min(D, 4 * LANES)
    npairs = td // LANES
    nsub = _pick(math.gcd(lay.C, lay.S) // L, (4, 2, 1))
    tb = nsub * L
    ctx_blk = lay.C // tb
    lat_blk = lay.S // tb
    ctx_total = lay.n_ctx // tb

    def tok_block(b, c):
        if reverse:
            ctx_c = ctx_blk - 1 - c
            lat_c = lat_blk - 1 - (c - ctx_blk)
        else:
            ctx_c = c
            lat_c = c - ctx_blk
        return jnp.where(c < ctx_blk, b * ctx_blk + ctx_c, ctx_total + b * lat_blk + lat_c)

    mat = pl.BlockSpec((tb, td), lambda b, d, c: (tok_block(b, c), d))
    vec = pl.BlockSpec((1, td), lambda b, d, c: (0, d))
    row = lambda x: x.reshape(1, D)
    args = [r, k, v, wl, al, row(w0), row(a0), row(kkp), row(ka)]
    specs = [mat] * 5 + [vec] * 4
    if epilogue is not None:
        al_o, a0_o, y_o, gate, rk, lnx_g, lnx_b = epilogue
        args += [al_o, row(a0_o), y_o, gate, row(rk), row(lnx_g), row(lnx_b)]
        specs += [mat, vec, mat, mat, vec, vec, vec]
    return pl.pallas_call(
        functools.partial(_wkv_body, reverse=reverse, npairs=npairs, nsub=nsub, epilogue=epilogue is not None),
        grid=(lay.B, D // td, ctx_blk + lat_blk),
        in_specs=specs,
        out_specs=mat,
        out_shape=jax.ShapeDtypeStruct((M, D), F32 if epilogue is None else BF16),
        scratch_shapes=[pltpu.VMEM((npairs, LANES, LANES), F32)],
        compiler_params=_params(("parallel", "parallel", "arbitrary")),
        name="wkv_bwd" if reverse else "wkv_fwd",
    )(*args)


def _attn_body(q_ref, k_ref, v_ref, o_ref, *, scale):
    s = lax.dot_general(q_ref[0], k_ref[0], _NT, preferred_element_type=F32) * scale
    m = jnp.max(s, axis=-1, keepdims=True)
    p = jnp.exp(s - m)
    l = jnp.sum(p, axis=-1, keepdims=True)
    o = jnp.dot(p.astype(BF16), v_ref[0], preferred_element_type=F32)
    o_ref[0] = (o / l).astype(o_ref.dtype)


def _attn(q, k, v, heads, scale):
    B, Tq, _ = q.shape
    Tk = k.shape[1]
    dk = q.shape[2] // heads
    dv = v.shape[2] // heads
    tq = _pick(Tq, (512, 256, 128, 64, 32, 16, 8))
    return pl.pallas_call(
        functools.partial(_attn_body, scale=scale),
        grid=(B, heads, Tq // tq),
        in_specs=[pl.BlockSpec((1, tq, dk), lambda b, h, i: (b, i, h)),
                  pl.BlockSpec((1, Tk, dk), lambda b, h, i: (b, 0, h)),
                  pl.BlockSpec((1, Tk, dv), lambda b, h, i: (b, 0, h))],
        out_specs=pl.BlockSpec((1, tq, dv), lambda b, h, i: (b, i, h)),
        out_shape=jax.ShapeDtypeStruct((B, Tq, heads * dv), BF16),
        compiler_params=_params(("parallel", "parallel", "arbitrary")),
        name="attn",
    )(q, k, v)


def _rms(x, g):
    return x * lax.rsqrt(jnp.mean(x * x, axis=-1, keepdims=True) + NORM_EPS) * g


def _pad_cols(w, n):
    return jnp.pad(w, ((0, 0), (0, n - w.shape[1])))


def _pad_rows(w, n):
    return jnp.pad(w, ((0, n - w.shape[0]), (0, 0)))


def _up128(n):
    return -(-n // LANES) * LANES


def _split(lay, t):
    tail = t.shape[1:]
    return (t[:lay.n_ctx].reshape((lay.B, lay.C) + tail), t[lay.n_ctx:].reshape((lay.B, lay.S) + tail))


def _join(ctx, lat):
    tail = ctx.shape[2:]
    return jnp.concatenate([ctx.reshape((-1,) + tail), lat.reshape((-1,) + tail)], axis=0)


def _axial_rope(n):
    rows = n // GRID_W
    row = jnp.broadcast_to(jnp.arange(rows)[:, None], (rows, GRID_W)).reshape(-1)
    col = jnp.broadcast_to(jnp.arange(GRID_W)[None, :], (rows, GRID_W)).reshape(-1)
    n_freq = MLA_ROPE // 4
    inv = ROPE_THETA ** (-jnp.arange(n_freq, dtype=F32) / n_freq)
    ang = jnp.concatenate([row[:, None].astype(F32) * inv, col[:, None].astype(F32) * inv], axis=-1)
    return jnp.cos(ang), jnp.sin(ang)


def _apply_rope(x, cos, sin):
    xf = x.reshape(x.shape[:-1] + (x.shape[-1] // 2, 2))
    x0, x1 = xf[..., 0], xf[..., 1]
    out = jnp.stack([x0 * cos - x1 * sin, x0 * sin + x1 * cos], axis=-1)
    return out.reshape(x.shape)


def _rwkv_mixer(h, norm_g, mod, lay, v_first, pw, vres, rk, lnx_g, lnx_b):
    mu, wr, wk, wv, w0, w1, w2, a0, a1, a2, g1, g2, kkp, ka = pw
    xr, xw, xk, xv, xa, xg = _rw_pre(h, norm_g, mod, mu, lay)
    mm = functools.partial(_mm, lay=lay)
    r = mm(xr, wr.astype(BF16), name="rw_r")
    k = mm(xk, wk.astype(BF16), name="rw_k")
    v = mm(xv, wv.astype(BF16), name="rw_v")
    if vres is not None:
        v0, v1, v2 = vres
        n = _up128(v1.shape[1])
        lora = mm(mm(xv, _pad_cols(v1, n).astype(BF16), out_dtype=BF16, name="rw_v1"),
                  _pad_rows(v2, n).astype(BF16), name="rw_v2")
        v = v + (v_first - v) * jax.nn.sigmoid(v0 + lora)
    g = mm(mm(xg, g1.astype(BF16), act="sigmoid", out_dtype=BF16, name="rw_g1"), g2.astype(BF16), name="rw_g2")

    nw = _up128(w1.shape[2])
    w_mid = mm(xw, jnp.concatenate([_pad_cols(w1[e], nw) for e in range(2)], axis=1).astype(BF16),
               act="tanh", out_dtype=BF16, name="rw_w1")
    wl = [mm(w_mid, _pad_rows(w2[e], nw).astype(BF16), x_kblock=e, name="rw_w2") for e in range(2)]
    na = _up128(a1.shape[2])
    a_mid = mm(xa, jnp.concatenate([_pad_cols(a1[e], na) for e in range(2)], axis=1).astype(BF16),
               out_dtype=BF16, name="rw_a1")
    al = [mm(a_mid, _pad_rows(a2[e], na).astype(BF16), x_kblock=e, name="rw_a2") for e in range(2)]

    y_fwd = _wkv(r, k, v, wl[0], al[0], w0[0], a0[0], kkp, ka, lay, reverse=False)
    o = _wkv(r, k, v, wl[1], al[1], w0[1], a0[1], kkp, ka, lay, reverse=True,
             epilogue=(al[0], a0[0], y_fwd, g, rk.reshape(-1), lnx_g, lnx_b))
    return o, v


def _mla_mixer(u, lay, rope, wdq, qnorm, wuq, wdkv, kvnorm, wukv, qn_nope, qn_rope, kn_nope, kn_rope, need_ctx):
    M, D = u.shape
    heads = D // MLA_V
    kv_lora = kvnorm.shape[0]
    cos, sin = rope
    mm = functools.partial(_mm, lay=lay)

    def rope_lat(t):
        t_ctx, t_lat = _split(lay, t)
        extra = (1,) * (t.ndim - 2)
        c = cos.reshape((1, lay.S) + extra + (cos.shape[-1],))
        s = sin.reshape((1, lay.S) + extra + (sin.shape[-1],))
        return _join(t_ctx, _apply_rope(t_lat, c, s))

    ckv = mm(u, _pad_cols(wdkv, _up128(wdkv.shape[1])).astype(BF16), name="mla_dkv")
    c_kv = _rms(ckv[:, :kv_lora], kvnorm).astype(BF16)
    k_rope = rope_lat(_rms(ckv[:, kv_lora:kv_lora + MLA_ROPE], kn_rope))
    kv = mm(c_kv, wukv.astype(BF16), name="mla_ukv").reshape(M, heads, MLA_NOPE + MLA_V)
    k_nope = _rms(kv[..., :MLA_NOPE], kn_nope)
    v = kv[..., MLA_NOPE:].astype(BF16).reshape(M, heads * MLA_V)

    cq = _rms(mm(u, wdq.astype(BF16), name="mla_dq"), qnorm).astype(BF16)
    q = mm(cq, wuq.astype(BF16), name="mla_uq").reshape(M, heads, MLA_NOPE + MLA_ROPE)
    q_nope = _rms(q[..., :MLA_NOPE], qn_nope)
    q_rope = rope_lat(_rms(q[..., MLA_NOPE:], qn_rope))

    dk = 2 * LANES
    zq = jnp.zeros((M, heads, dk - MLA_NOPE - MLA_ROPE), F32)
    q_cat = jnp.concatenate([q_nope, q_rope, zq], axis=-1).astype(BF16).reshape(M, heads * dk)
    k_cat = jnp.concatenate([k_nope, jnp.broadcast_to(k_rope[:, None, :], (M, heads, MLA_ROPE)), zq],
                            axis=-1).astype(BF16).reshape(M, heads * dk)

    scale = 1.0 / math.sqrt(MLA_NOPE + MLA_ROPE)
    q_ctx, q_lat = _split(lay, q_cat)
    k_ctx, k_lat = _split(lay, k_cat)
    v_ctx, v_lat = _split(lay, v)
    o_lat = _attn(q_lat, jnp.concatenate([k_ctx, k_lat], axis=1), jnp.concatenate([v_ctx, v_lat], axis=1),
                  heads, scale)
    if not need_ctx:
        return o_lat.reshape(lay.B * lay.S, heads * MLA_V)
    o_ctx = _attn(q_ctx, k_ctx, v_ctx, heads, scale)
    return _join(o_ctx, o_lat)


def kernel(x, c, ctx, c_ctx, mod_w, mod_b, norm_g, mlp_w1, mlp_w2, rw_mu, rw_wr, rw_wk, rw_wv, rw_wo, rw_w0, rw_w1, rw_w2, rw_a0, rw_a1, rw_a2, rw_g1, rw_g2, rw_kk, rw_ka, rw_rk, rw_lnx_g, rw_lnx_b, rw_v0, rw_v1, rw_v2, mla_wdq, mla_qnorm, mla_wuq, mla_wdkv, mla_kvnorm, mla_wukv, mla_qn_nope, mla_qn_rope, mla_kn_nope, mla_kn_rope, mla_wo):
    B, S, D = x.shape
    C = ctx.shape[1]
    depth = mod_w.shape[0]
    lay = _Layout(B, C, S)
    rope = _axial_rope(S)

    sc_all = jnp.concatenate([jax.nn.silu(c_ctx)[None], jax.nn.silu(c)], axis=0)
    h = jnp.concatenate([ctx.reshape(B * C, D), x.reshape(B * S, D)], axis=0)
    v_first = None

    for i in range(depth):
        last = i == depth - 1
        j = i // 2
        mod = (_mm(sc_all, mod_w[i], name="adaln") + mod_b[i]).reshape(B + 1, N_MOD, D)
        if i % 2 == 0:
            pw = (rw_mu[j], rw_wr[j], rw_wk[j], rw_wv[j], rw_w0[j], rw_w1[j], rw_w2[j],
                  rw_a0[j], rw_a1[j], rw_a2[j], rw_g1[j], rw_g2[j], rw_kk[j], rw_ka[j])
            vres = None if j == 0 else (rw_v0[j - 1], rw_v1[j - 1], rw_v2[j - 1])
            o, v_cur = _rwkv_mixer(h, norm_g[i, 0], mod, lay, v_first, pw, vres, rw_rk[j], rw_lnx_g[j], rw_lnx_b[j])
            if j == 0:
                v_first = v_cur
            wo = rw_wo[j]
        else:
            u = _norm_mod(h, norm_g[i, 0], mod, lay, shift_row=0, scale_row=1, out_dtype=BF16)
            o = _mla_mixer(u, lay, rope, mla_wdq[j], mla_qnorm[j], mla_wuq[j], mla_wdkv[j], mla_kvnorm[j],
                           mla_wukv[j], mla_qn_nope[j], mla_qn_rope[j], mla_kn_nope[j], mla_kn_rope[j],
                           need_ctx=not last)
            wo = mla_wo[j]
        if last:
            if o.shape[0] != B * S:
                o = o[lay.n_ctx:]
            h = h[lay.n_ctx:]
            lay = _Layout(B, C, S, with_ctx=False)
        h = _mm(o, wo.astype(BF16), res=h, gate=mod[:, 2], lay=lay, name="mix_out")
        u2 = _norm_mod(h, norm_g[i, 1], mod, lay, shift_row=3, scale_row=4, out_dtype=BF16)
        hid = _mm(u2, mlp_w1[i].astype(BF16), act="relu2", out_dtype=BF16, lay=lay, name="mlp_up")
        h = _mm(hid, mlp_w2[i].astype(BF16), res=h, gate=mod[:, 5], lay=lay, name="mlp_down")
    return h.reshape(B, S, D)
```

```python
import functools
import math

import jax
import jax.numpy as jnp
from jax import lax
from jax.experimental import pallas as pl
from jax.experimental.pallas import tpu as pltpu

F32 = jnp.float32
BF16 = jnp.bfloat16

NORM_EPS = 1e-6
N_MOD = 6
GRID_W = 64
RW_HEAD = 64
RW_GN_EPS = 64e-5
RW_L2_EPS = 1e-12
MLA_NOPE = 128
MLA_ROPE = 64
MLA_V = 128
ROPE_THETA = 10000.0

LANES = 128
SUBLANES = 8
WKV_CHUNK = 64
VMEM_LIMIT_BYTES = 56 * 1024 * 1024

_NT = (((1,), (1,)), ((), ()))


def _pick(n, prefs):
    for p in prefs:
        if n % p == 0:
            return p
    return n


def _params(sem):
    return pltpu.CompilerParams(dimension_semantics=sem, vmem_limit_bytes=VMEM_LIMIT_BYTES)


class _Layout:
    def __init__(self, B, C, S, with_ctx=True):
        self.B, self.C, self.S = B, C, S
        self.n_ctx = B * C if with_ctx else 0
        self.M = self.n_ctx + B * S

    def row_tile(self):
        return _pick(math.gcd(self.n_ctx, self.S) if self.n_ctx else self.S, (1024, 512, 256, 128, 64, 32, 16, 8))

    def seq_tile(self):
        return _pick(math.gcd(self.C, self.S) if self.n_ctx else self.S, (256, 128, 64, 32, 16, 8))

    def mod_row(self, i, tm):
        n_ctx_tiles = self.n_ctx // tm
        per_b = self.S // tm
        lat = 1 + (i - n_ctx_tiles) // per_b
        if n_ctx_tiles == 0:
            return lat
        return jnp.where(i < n_ctx_tiles, 0, lat)


def _mm_body(*refs, nk, act, has_gate, cache_w):
    if has_gate:
        x_ref, w_ref, res_ref, gate_ref, o_ref, *scratch = refs
    else:
        x_ref, w_ref, o_ref, *scratch = refs

    def finish(acc):
        if act == "relu2":
            acc = jnp.square(jnp.maximum(acc, 0.0))
        elif act == "sigmoid":
            acc = jax.nn.sigmoid(acc)
        elif act == "tanh":
            acc = jnp.tanh(acc)
        if has_gate:
            acc = res_ref[...] + gate_ref[0] * acc
        o_ref[...] = acc.astype(o_ref.dtype)

    if cache_w:
        wc_ref = scratch[-1]

        @pl.when(pl.program_id(1) == 0)
        def _():
            wc_ref[...] = w_ref[...].astype(BF16)

        w = wc_ref[...]
    else:
        w = w_ref[...].astype(BF16)
    part = jnp.dot(x_ref[...].astype(BF16), w, preferred_element_type=F32)
    if nk == 1:
        finish(part)
    else:
        acc_ref = scratch[0]
        k = pl.program_id(2)

        @pl.when(k == 0)
        def _():
            acc_ref[...] = part

        @pl.when(k > 0)
        def _():
            acc_ref[...] += part

        @pl.when(k == nk - 1)
        def _():
            finish(acc_ref[...])


def _mm(x, w, *, layer=None, act=None, out_dtype=F32, res=None, res_row0=0, gate=None, lay=None, tm=None,
        x_kblock=0, name="mm"):
    M = x.shape[0]
    K, N = w.shape[-2:]
    if tm is None:
        tm = lay.row_tile() if lay is not None else _pick(M, (1024, 512, 256, 128, 64, 32, 16, 8))
    tn = _pick(N, (1024, 512, 256, 128))
    tk = K if K <= 2048 else _pick(K, (2048, 1024, 512))
    nk = K // tk
    has_gate = gate is not None
    cache_w = w.dtype == F32 and nk == 1 and M // tm > 1
    if w.ndim == 3:
        w_spec = pl.BlockSpec((None, tk, tn), lambda j, i, k: (layer, k, j))
    else:
        w_spec = pl.BlockSpec((tk, tn), lambda j, i, k: (k, j))
    in_specs = [pl.BlockSpec((tm, tk), lambda j, i, k: (i, k + x_kblock * nk)), w_spec]
    args = [x, w]
    if has_gate:
        in_specs += [pl.BlockSpec((tm, tn), lambda j, i, k: (i + res_row0 // tm, j)),
                     pl.BlockSpec((1, 1, tn), lambda j, i, k: (lay.mod_row(i, tm), 0, j))]
        args += [res, gate.reshape(gate.shape[0], 1, N)]
    scratch = [pltpu.VMEM((tm, tn), F32)] if nk > 1 else []
    if cache_w:
        scratch.append(pltpu.VMEM((tk, tn), BF16))
    return pl.pallas_call(
        functools.partial(_mm_body, nk=nk, act=act, has_gate=has_gate, cache_w=cache_w),
        grid=(N // tn, M // tm, nk),
        in_specs=in_specs,
        out_specs=pl.BlockSpec((tm, tn), lambda j, i, k: (i, j)),
        out_shape=jax.ShapeDtypeStruct((M, N), out_dtype),
        scratch_shapes=scratch,
        compiler_params=_params(("parallel", "arbitrary", "arbitrary")),
        name=name,
    )(*args)


def _norm_mod_rows(x, gain, mod, shift_row, scale_row):
    ms = jnp.mean(x * x, axis=-1, keepdims=True)
    y = x * lax.rsqrt(ms + NORM_EPS) * gain
    return y * (1.0 + mod[scale_row:scale_row + 1]) + mod[shift_row:shift_row + 1]


def _norm_mod_body(x_ref, g_ref, mod_ref, o_ref, *, shift_row, scale_row):
    o_ref[...] = _norm_mod_rows(x_ref[...], g_ref[...], mod_ref[0], shift_row, scale_row).astype(o_ref.dtype)


def _norm_mod(x, g, mod, lay, *, shift_row, scale_row, out_dtype):
    M, D = x.shape
    tm = min(lay.row_tile(), 512)
    return pl.pallas_call(
        functools.partial(_norm_mod_body, shift_row=shift_row, scale_row=scale_row),
        grid=(M // tm,),
        in_specs=[pl.BlockSpec((tm, D), lambda i: (i, 0)),
                  pl.BlockSpec((1, D), lambda i: (0, 0)),
                  pl.BlockSpec((1, N_MOD, D), lambda i: (lay.mod_row(i, tm), 0, 0))],
        out_specs=pl.BlockSpec((tm, D), lambda i: (i, 0)),
        out_shape=jax.ShapeDtypeStruct((M, D), out_dtype),
        compiler_params=_params(("parallel",)),
        name="norm_mod",
    )(x, g.reshape(1, D), mod)


def _rw_pre_body(h_ref, hp_ref, hn_ref, g_ref, mod_ref, mu_ref, *o_refs, tm, n_ctx_tiles, ctx_tiles, lat_tiles):
    i = pl.program_id(0)
    gain = g_ref[...]
    mod = mod_ref[0]
    u = _norm_mod_rows(h_ref[...], gain, mod, 0, 1)
    u_before = _norm_mod_rows(hp_ref[SUBLANES - 1:SUBLANES, :], gain, mod, 0, 1)
    u_after = _norm_mod_rows(hn_ref[0:1, :], gain, mod, 0, 1)
    in_ctx = i < n_ctx_tiles
    pos = jnp.where(in_ctx, i % ctx_tiles, (i - n_ctx_tiles) % lat_tiles)
    last = jnp.where(in_ctx, ctx_tiles - 1, lat_tiles - 1)
    u_before = jnp.where(pos == 0, 0.0, u_before)
    u_after = jnp.where(pos == last, 0.0, u_after)
    rows = lax.broadcasted_iota(jnp.int32, (tm, 1), 0)
    prev = jnp.where(rows == 0, u_before, pltpu.roll(u, 1, axis=0))
    nxt = jnp.where(rows == tm - 1, u_after, pltpu.roll(u, tm - 1, axis=0))
    xx = 0.5 * (prev + nxt) - u
    for j, o_ref in enumerate(o_refs):
        o_ref[...] = (u + xx * mu_ref[j:j + 1, :]).astype(o_ref.dtype)


def _rw_pre(h, g, mod, mu, lay):
    M, D = h.shape
    tm = lay.seq_tile()
    nmix = mu.shape[0]
    per8 = tm // SUBLANES
    last8 = M // SUBLANES - 1
    return pl.pallas_call(
        functools.partial(_rw_pre_body, tm=tm, n_ctx_tiles=lay.n_ctx // tm, ctx_tiles=max(lay.C // tm, 1),
                          lat_tiles=lay.S // tm),
        grid=(M // tm,),
        in_specs=[pl.BlockSpec((tm, D), lambda i: (i, 0)),
                  pl.BlockSpec((SUBLANES, D), lambda i: (jnp.maximum(i * per8 - 1, 0), 0)),
                  pl.BlockSpec((SUBLANES, D), lambda i: (jnp.minimum((i + 1) * per8, last8), 0)),
                  pl.BlockSpec((1, D), lambda i: (0, 0)),
                  pl.BlockSpec((1, N_MOD, D), lambda i: (lay.mod_row(i, tm), 0, 0)),
                  pl.BlockSpec((nmix, D), lambda i: (0, 0))],
        out_specs=[pl.BlockSpec((tm, D), lambda i: (i, 0))] * nmix,
        out_shape=[jax.ShapeDtypeStruct((M, D), BF16)] * nmix,
        compiler_params=_params(("parallel",)),
        name="rw_pre",
    )(h, h, h, g.reshape(1, D), mod, mu)


def _wkv_body(*refs, reverse, npairs, nsub, epilogue):
    if epilogue:
        (r_ref, k_ref, v_ref, wl_ref, al_ref, w0_ref, a0_ref, kkp_ref, ka_ref,
         alo_ref, a0o_ref, yo_ref, g_ref, rk_ref, lng_ref, lnb_ref, o_ref, s_ref) = refs
    else:
        r_ref, k_ref, v_ref, wl_ref, al_ref, w0_ref, a0_ref, kkp_ref, ka_ref, o_ref, s_ref = refs
    L = WKV_CHUNK
    H = RW_HEAD

    @pl.when(pl.program_id(2) == 0)
    def _():
        s_ref[...] = jnp.zeros_like(s_ref)

    def order(row, col):
        return (row <= col) if reverse else (row >= col)

    row = lax.broadcasted_iota(jnp.int32, (L, L), 0)
    col = lax.broadcasted_iota(jnp.int32, (L, L), 1)
    tri = jnp.where(order(row, col), 1.0, 0.0).astype(BF16)

    prow = lax.broadcasted_iota(jnp.int32, (L, 2 * L), 0)
    pcol = lax.broadcasted_iota(jnp.int32, (L, 2 * L), 1) & (L - 1)
    incl = order(prow, pcol)
    strict = incl & (prow != pcol)
    incl2 = jnp.concatenate([incl, incl], axis=1)
    eye = jnp.where(prow == pcol, 1.0, 0.0)

    def sibling(s):
        return ((prow // (2 * s)) == (pcol // (2 * s))) & ((prow // s) != (pcol // s))

    bmask = (lax.broadcasted_iota(jnp.int32, (2 * L, LANES), 0) // L
             == lax.broadcasted_iota(jnp.int32, (2 * L, LANES), 1) // H)
    head_ones = jnp.where(lax.broadcasted_iota(jnp.int32, (LANES, LANES), 0) // H
                          == lax.broadcasted_iota(jnp.int32, (LANES, LANES), 1) // H, 1.0, 0.0).astype(BF16)

    def bdf(x):
        return jnp.where(bmask, jnp.concatenate([x, x], axis=0), 0.0)

    def bd(x):
        return bdf(x).astype(BF16)

    def dot(a, b):
        return jnp.dot(a, b, preferred_element_type=F32)

    def cat0(*xs):
        return jnp.concatenate(xs, axis=0)

    def cat1(*xs):
        return jnp.concatenate(xs, axis=1)

    units = [(ci, p) for ci in range(nsub) for p in range(npairs)]
    rng = range(len(units))

    def tiles(ref):
        return [ref[ci * L:(ci + 1) * L, p * LANES:(p + 1) * LANES] for ci, p in units]

    def vecs(ref):
        return [ref[:, p * LANES:(p + 1) * LANES] for _, p in units]

    def head_sum(xs):
        x = cat0(*xs)
        hi = x.astype(BF16)
        lo = (x - hi.astype(F32)).astype(BF16)
        tot = dot(hi, head_ones) + dot(lo, head_ones)
        return [tot[i * L:(i + 1) * L] for i in rng]

    r, k, v = tiles(r_ref), tiles(k_ref), tiles(v_ref)
    ka = vecs(ka_ref)

    a = [jax.nn.sigmoid(a0 + al) for a0, al in zip(vecs(a0_ref), tiles(al_ref))]
    z = [-(w0 + wl) for w0, wl in zip(vecs(w0_ref), tiles(wl_ref))]
    softplus = [jnp.maximum(x, 0.0) + jnp.log(1.0 + jnp.exp(-jnp.abs(x))) for x in z]
    lw = [-jnp.exp(-x - 0.5) for x in softplus]
    kraw = [k[i] * kkp for i, kkp in zip(rng, vecs(kkp_ref))]
    norm2 = head_sum([x * x for x in kraw])
    kk = [kraw[i] / jnp.maximum(jnp.sqrt(norm2[i]), RW_L2_EPS) for i in rng]
    kd = [k[i] * (1.0 + (a[i] - 1.0) * ka[i]) for i in rng]

    hi = [x.astype(BF16) for x in lw]
    rem = [lw[i] - hi[i].astype(F32) for i in rng]
    mid = [x.astype(BF16) for x in rem]
    lo = [(rem[i] - mid[i].astype(F32)).astype(BF16) for i in rng]
    g = [dot(tri, hi[i]) + dot(tri, mid[i]) + dot(tri, lo[i]) for i in rng]
    ee = [jnp.exp(jnp.sum(x, axis=0, keepdims=True)) for x in lw]

    en = [jnp.exp(-x) for x in g]
    abar = [-(kk[i] * jnp.exp(g[i] - lw[i])) for i in rng]
    rbar = [r[i] * jnp.exp(g[i]) for i in rng]
    bt = [kk[i] * a[i] * en[i] for i in rng]
    kt = [kd[i] * en[i] for i in rng]

    amat = [lax.dot_general(cat0(abar[i], rbar[i]).astype(BF16), cat0(bd(bt[i]), bd(kt[i])), _NT,
                            preferred_element_type=F32) for i in rng]
    a_ab = [jnp.where(strict, x[:L, :2 * L], 0.0) for x in amat]
    a_ak = [jnp.where(strict, x[:L, 2 * L:], 0.0).astype(BF16) for x in amat]
    a_rbk = [jnp.where(incl2, x[L:], 0.0).astype(BF16) for x in amat]

    t = [eye + jnp.where(sibling(1), x, 0.0) for x in a_ab]
    s = 2
    while s < L:
        sib = sibling(s)
        half = [dot(t[i].astype(BF16), bd(jnp.where(sib, a_ab[i], 0.0))).astype(BF16) for i in rng]
        t = [t[i] + dot(half[i], bd(t[i])) for i in rng]
        s *= 2

    vbd = [bd(x) for x in v]
    av = [dot(a_ak[i], vbd[i]) for i in rng]
    wu = [dot(t[i].astype(BF16), cat1(bd(abar[i]), bd(av[i]))) for i in rng]
    wr_lhs = [cat0(wu[i][:, :LANES], rbar[i]).astype(BF16) for i in rng]
    ut = [x[:, LANES:] for x in wu]
    c2_lhs = [cat0(a_rbk[i], cat1(bdf(bt[i] * ee[i]).T, bdf(kt[i] * ee[i]).T).astype(BF16)) for i in rng]
    gam = [jnp.broadcast_to(x, (LANES, LANES)).T for x in ee]

    state = [s_ref[p] for p in range(npairs)]
    y = [None] * len(units)
    for ci in (reversed(range(nsub)) if reverse else range(nsub)):
        ids = [ci * npairs + p for p in range(npairs)]
        wr = [dot(wr_lhs[i], state[p].astype(BF16)) for p, i in enumerate(ids)]
        u = [wr[p][:L] + ut[i] for p, i in enumerate(ids)]
        out2 = [dot(c2_lhs[i], cat0(bd(u[p]), vbd[i])) for p, i in enumerate(ids)]
        for p, i in enumerate(ids):
            y[i] = wr[p][L:] + out2[p][:L]
            state[p] = state[p] * gam[i] + out2[p][L:]
    for p in range(npairs):
        s_ref[p] = state[p]

    def store(vals):
        for (ci, p), val in zip(units, vals):
            o_ref[ci * L:(ci + 1) * L, p * LANES:(p + 1) * LANES] = val.astype(o_ref.dtype)

    if not epilogue:
        store(y)
        return

    inv_n = 1.0 / H
    ytot = [y[i] + yo for i, yo in zip(rng, tiles(yo_ref))]
    mean = head_sum(ytot)
    dev = [ytot[i] - mean[i] * inv_n for i in rng]
    var = head_sum([x * x for x in dev])
    a_o = [jax.nn.sigmoid(a0 + al) for a0, al in zip(vecs(a0o_ref), tiles(alo_ref))]
    kd_sum = [kd[i] + k[i] * (1.0 + (a_o[i] - 1.0) * ka[i]) for i in rng]
    bonus = head_sum([r[i] * rk * kd_sum[i] for i, rk in zip(rng, vecs(rk_ref))])
    lng, lnb, gate = vecs(lng_ref), vecs(lnb_ref), tiles(g_ref)
    store([(dev[i] * lax.rsqrt(var[i] * inv_n + RW_GN_EPS) * lng[i] + lnb[i] + bonus[i] * v[i]) * gate[i]
           for i in rng])


def _wkv(r, k, v, wl, al, w0, a0, kkp, ka, lay, *, reverse, epilogue=None):
    M, D = r.shape
    L = WKV_CHUNK
    td = min(D, 4 * LANES)
    npairs = td // LANES
    nsub = _pick(math.gcd(lay.C, lay.S) // L, (4, 2, 1))
    tb = nsub * L
    ctx_blk = lay.C // tb
    lat_blk = lay.S // tb
    ctx_total = lay.n_ctx // tb

    def tok_block(b, c):
        if reverse:
            ctx_c = ctx_blk - 1 - c
            lat_c = lat_blk - 1 - (c - ctx_blk)
        else:
            ctx_c = c
            lat_c = c - ctx_blk
        return jnp.where(c < ctx_blk, b * ctx_blk + ctx_c, ctx_total + b * lat_blk + lat_c)

    mat = pl.BlockSpec((tb, td), lambda b, d, c: (tok_block(b, c), d))
    vec = pl.BlockSpec((1, td), lambda b, d, c: (0, d))
    row = lambda x: x.reshape(1, D)
    args = [r, k, v, wl, al, row(w0), row(a0), row(kkp), row(ka)]
    specs = [mat] * 5 + [vec] * 4
    if epilogue is not None:
        al_o, a0_o, y_o, gate, rk, lnx_g, lnx_b = epilogue
        args += [al_o, row(a0_o), y_o, gate, row(rk), row(lnx_g), row(lnx_b)]
        specs += [mat, vec, mat, mat, vec, vec, vec]
    return pl.pallas_call(
        functools.partial(_wkv_body, reverse=reverse, npairs=npairs, nsub=nsub, epilogue=epilogue is not None),
        grid=(lay.B, D // td, ctx_blk + lat_blk),
        in_specs=specs,
        out_specs=mat,
        out_shape=jax.ShapeDtypeStruct((M, D), F32 if epilogue is None else BF16),
        scratch_shapes=[pltpu.VMEM((npairs, LANES, LANES), F32)],
        compiler_params=_params(("parallel", "parallel", "arbitrary")),
        name="wkv_bwd" if reverse else "wkv_fwd",
    )(*args)


ATTN_KEY_CHUNK = 512


def _attn_body(q_ref, *refs, n_kv):
    k_refs, v_refs = refs[:n_kv], refs[n_kv:2 * n_kv]
    o_ref = refs[2 * n_kv]
    q = q_ref[...]
    pieces = []
    for k_ref, v_ref in zip(k_refs, v_refs):
        n = k_ref.shape[0]
        for lo in range(0, n, ATTN_KEY_CHUNK):
            pieces.append((k_ref, v_ref, lo, min(lo + ATTN_KEY_CHUNK, n)))

    def scores(piece):
        k_ref, _, lo, hi = piece
        return lax.dot_general(q, k_ref[lo:hi, :], _NT, preferred_element_type=F32)

    def lane_blocks(x):
        return [x[:, c:c + LANES] for c in range(0, x.shape[1], LANES)]

    s_next = scores(pieces[0])
    m = l = acc = None
    for c, (_, v_ref, lo, hi) in enumerate(pieces):
        s = s_next
        if c + 1 < len(pieces):
            s_next = scores(pieces[c + 1])
        m_c = jnp.max(functools.reduce(jnp.maximum, lane_blocks(s)), axis=-1, keepdims=True)
        m_new = m_c if m is None else jnp.maximum(m, m_c)
        p = jnp.exp2(s - m_new)
        l_c = functools.reduce(jnp.add, lane_blocks(p))
        pv = jnp.dot(p.astype(BF16), v_ref[lo:hi, :], preferred_element_type=F32)
        if m is None:
            l, acc = l_c, pv
        else:
            alpha = jnp.exp2(m - m_new)
            l, acc = alpha * l + l_c, alpha * acc + pv
        m = m_new
    o_ref[...] = (acc / jnp.sum(l, axis=-1, keepdims=True)).astype(o_ref.dtype)


def _attn(q, k, v, heads, lay, *, q_row0, n_q_rows, with_lat_keys):
    dk = k.shape[1] // heads
    dv = v.shape[1] // heads
    tq = _pick(n_q_rows, (512, 256, 128, 64, 32, 16, 8))
    per_b = n_q_rows // tq
    q0 = q_row0 // tq
    kv = [(lay.C, 0)]
    if with_lat_keys:
        kv.append((lay.S, lay.n_ctx // lay.S))
        k_lat, v_lat = k, v
        if lay.n_ctx % lay.S:
            k_lat, v_lat, kv[1] = k[lay.n_ctx:], v[lay.n_ctx:], (lay.S, 0)
    in_specs = [pl.BlockSpec((tq, dk), lambda b, h, i: (q0 + b * per_b + i, h))]
    in_specs += [pl.BlockSpec((n, dk), functools.partial(lambda b, h, i, base: (base + b, h), base=base))
                 for n, base in kv]
    in_specs += [pl.BlockSpec((n, dv), functools.partial(lambda b, h, i, base: (base + b, h), base=base))
                 for n, base in kv]
    operands = [q, k] + ([k_lat] if with_lat_keys else []) + [v] + ([v_lat] if with_lat_keys else [])
    return pl.pallas_call(
        functools.partial(_attn_body, n_kv=len(kv)),
        grid=(lay.B, heads, per_b),
        in_specs=in_specs,
        out_specs=pl.BlockSpec((tq, dv), lambda b, h, i: (b * per_b + i, h)),
        out_shape=jax.ShapeDtypeStruct((lay.B * n_q_rows, heads * dv), BF16),
        compiler_params=_params(("parallel", "parallel", "arbitrary")),
        name="attn",
    )(*operands)


def _rms_rows(x, gain):
    return x * lax.rsqrt(jnp.mean(x * x, axis=-1, keepdims=True) + NORM_EPS) * gain


def _rope_lanes(x, cos, sin):
    lane = lax.broadcasted_iota(jnp.int32, x.shape, 1)
    first = (lane % MLA_ROPE) < MLA_ROPE // 2
    partner = jnp.where(first, pltpu.roll(x, LANES - MLA_ROPE // 2, axis=1), pltpu.roll(x, MLA_ROPE // 2, axis=1))
    return x * cos + partner * sin


def _mla_dkv_body(u_ref, w_ref, g_ref, gr_ref, cos_ref, sin_ref, ckv_ref, kr_ref, *, kv_lora):
    acc = jnp.dot(u_ref[...], w_ref[...], preferred_element_type=F32)
    ckv_ref[...] = _rms_rows(acc[:, :kv_lora], g_ref[...]).astype(ckv_ref.dtype)
    kr = acc[:, kv_lora:]
    ms = jnp.sum(kr * kr, axis=-1, keepdims=True) * (1.0 / MLA_ROPE)
    kr = kr * lax.rsqrt(ms + NORM_EPS) * gr_ref[...]
    kr_ref[...] = _rope_lanes(kr, cos_ref[...], sin_ref[...]).astype(kr_ref.dtype)


def _mla_ukv_body(c_ref, w_ref, g_ref, kr_ref, kcat_ref, v_ref, *, heads_per_tile):
    acc = jnp.dot(c_ref[...], w_ref[...], preferred_element_type=F32)
    kr = kr_ref[...]
    dk = MLA_NOPE + LANES
    for h in range(heads_per_tile):
        base = h * (MLA_NOPE + MLA_V)
        kcat_ref[:, h * dk:h * dk + MLA_NOPE] = _rms_rows(acc[:, base:base + MLA_NOPE], g_ref[...]).astype(kcat_ref.dtype)
        kcat_ref[:, h * dk + MLA_NOPE:(h + 1) * dk] = kr
        v_ref[:, h * MLA_V:(h + 1) * MLA_V] = acc[:, base + MLA_NOPE:base + MLA_NOPE + MLA_V].astype(v_ref.dtype)


def _mla_dq_body(u_ref, w_ref, g_ref, o_ref):
    acc = jnp.dot(u_ref[...], w_ref[...], preferred_element_type=F32)
    o_ref[...] = _rms_rows(acc, g_ref[...]).astype(o_ref.dtype)


def _mla_uq_body(c_ref, wn_ref, wr_ref, gn_ref, gr_ref, cos_ref, sin_ref, q_ref, *, heads_per_tile, qscale):
    x = c_ref[...]
    nope = jnp.dot(x, wn_ref[...], preferred_element_type=F32)
    rope = jnp.dot(x, wr_ref[...], preferred_element_type=F32)
    low = lax.broadcasted_iota(jnp.int32, (x.shape[0], LANES), 1) < MLA_ROPE
    dk = MLA_NOPE + LANES
    for b in range(heads_per_tile // 2):
        xr = rope[:, b * LANES:(b + 1) * LANES]
        sq = xr * xr
        ms = jnp.where(low, jnp.sum(jnp.where(low, sq, 0.0), axis=-1, keepdims=True),
                       jnp.sum(jnp.where(low, 0.0, sq), axis=-1, keepdims=True)) * (1.0 / MLA_ROPE)
        xr = xr * lax.rsqrt(ms + NORM_EPS) * gr_ref[...]
        xr = _rope_lanes(xr, cos_ref[...], sin_ref[...]) * qscale
        for t in range(2):
            h = 2 * b + t
            qn = _rms_rows(nope[:, h * MLA_NOPE:(h + 1) * MLA_NOPE], gn_ref[...]) * qscale
            q_ref[:, h * dk:h * dk + MLA_NOPE] = qn.astype(q_ref.dtype)
            rr = xr if t == 0 else pltpu.roll(xr, MLA_ROPE, axis=1)
            q_ref[:, h * dk + MLA_NOPE:(h + 1) * dk] = jnp.where(low, rr, 0.0).astype(q_ref.dtype)


def _pad_cols(w, n):
    return jnp.pad(w, ((0, 0), (0, n - w.shape[1])))


def _pad_rows(w, n):
    return jnp.pad(w, ((0, n - w.shape[0]), (0, 0)))


def _up128(n):
    return -(-n // LANES) * LANES


def _rope_tables(lay):
    n = lay.S
    rows = n // GRID_W
    row = jnp.broadcast_to(jnp.arange(rows)[:, None], (rows, GRID_W)).reshape(-1)
    col = jnp.broadcast_to(jnp.arange(GRID_W)[None, :], (rows, GRID_W)).reshape(-1)
    n_freq = MLA_ROPE // 4
    inv = ROPE_THETA ** (-jnp.arange(n_freq, dtype=F32) / n_freq)
    ang = jnp.concatenate([row[:, None].astype(F32) * inv, col[:, None].astype(F32) * inv], axis=-1)
    cos, sin = jnp.cos(ang), jnp.sin(ang)
    reps = LANES // MLA_ROPE
    cos_l = jnp.tile(jnp.concatenate([cos, cos], axis=1), (lay.B, reps))
    sin_l = jnp.tile(jnp.concatenate([-sin, sin], axis=1), (lay.B, reps))
    cos_t = jnp.concatenate([jnp.ones((lay.n_ctx, LANES), F32), cos_l], axis=0)
    sin_t = jnp.concatenate([jnp.zeros((lay.n_ctx, LANES), F32), sin_l], axis=0)
    return cos_t, sin_t


def _rwkv_mixer(h, norm_g, mod, lay, v_first, j, big_w, pw, vres, rk, lnx_g, lnx_b):
    wr, wk, wv = big_w
    mu, w0, w1, w2, a0, a1, a2, g1, g2, kkp, ka = pw
    xr, xw, xk, xv, xa, xg = _rw_pre(h, norm_g, mod, mu, lay)
    mm = functools.partial(_mm, lay=lay)
    r = mm(xr, wr, layer=j, name="rw_r")
    k = mm(xk, wk, layer=j, name="rw_k")
    v = mm(xv, wv, layer=j, name="rw_v")
    if vres is not None:
        v0, v1, v2 = vres
        n = _up128(v1.shape[1])
        lora = mm(mm(xv, _pad_cols(v1, n).astype(BF16), out_dtype=BF16, name="rw_v1"),
                  _pad_rows(v2, n).astype(BF16), name="rw_v2")
        v = v + (v_first - v) * jax.nn.sigmoid(v0 + lora)
    g = mm(mm(xg, g1.astype(BF16), act="sigmoid", out_dtype=BF16, name="rw_g1"), g2.astype(BF16), name="rw_g2")

    nw = _up128(w1.shape[2])
    w_mid = mm(xw, jnp.concatenate([_pad_cols(w1[e], nw) for e in range(2)], axis=1).astype(BF16),
               act="tanh", out_dtype=BF16, name="rw_w1")
    wl = [mm(w_mid, _pad_rows(w2[e], nw).astype(BF16), x_kblock=e, name="rw_w2") for e in range(2)]
    na = _up128(a1.shape[2])
    a_mid = mm(xa, jnp.concatenate([_pad_cols(a1[e], na) for e in range(2)], axis=1).astype(BF16),
               out_dtype=BF16, name="rw_a1")
    al = [mm(a_mid, _pad_rows(a2[e], na).astype(BF16), x_kblock=e, name="rw_a2") for e in range(2)]

    y_fwd = _wkv(r, k, v, wl[0], al[0], w0[0], a0[0], kkp, ka, lay, reverse=False)
    o = _wkv(r, k, v, wl[1], al[1], w0[1], a0[1], kkp, ka, lay, reverse=True,
             epilogue=(al[0], a0[0], y_fwd, g, rk.reshape(-1), lnx_g, lnx_b))
    return o, v


def _mla_mixer(u, lay, rope_tabs, wdq, qnorm, wuq, wdkv, kvnorm, wukv, qn_nope, qn_rope, kn_nope, kn_rope, need_ctx):
    M, D = u.shape
    heads = D // MLA_V
    kv_lora, q_lora = kvnorm.shape[0], qnorm.shape[0]
    cos_t, sin_t = rope_tabs
    tm = lay.row_tile()
    hpt = _pick(heads, (4, 2))
    dk = MLA_NOPE + LANES
    pad = LANES - MLA_ROPE
    perm = jnp.concatenate([jnp.arange(0, MLA_ROPE, 2), jnp.arange(1, MLA_ROPE, 2)])
    row = lambda t: t.reshape(1, -1)
    par1 = _params(("parallel",))
    par2 = _params(("parallel", "parallel"))

    w_dkv = jnp.concatenate([wdkv[:, :kv_lora], wdkv[:, kv_lora:][:, perm], jnp.zeros((D, pad), F32)], axis=1)
    c_kv, k_rope = pl.pallas_call(
        functools.partial(_mla_dkv_body, kv_lora=kv_lora),
        grid=(M // tm,),
        in_specs=[pl.BlockSpec((tm, D), lambda i: (i, 0)),
                  pl.BlockSpec((D, kv_lora + LANES), lambda i: (0, 0)),
                  pl.BlockSpec((1, kv_lora), lambda i: (0, 0)),
                  pl.BlockSpec((1, LANES), lambda i: (0, 0)),
                  pl.BlockSpec((tm, LANES), lambda i: (i, 0)),
                  pl.BlockSpec((tm, LANES), lambda i: (i, 0))],
        out_specs=[pl.BlockSpec((tm, kv_lora), lambda i: (i, 0)), pl.BlockSpec((tm, LANES), lambda i: (i, 0))],
        out_shape=[jax.ShapeDtypeStruct((M, kv_lora), BF16), jax.ShapeDtypeStruct((M, LANES), BF16)],
        compiler_params=par1, name="mla_dkv",
    )(u, w_dkv.astype(BF16), row(kvnorm), row(jnp.pad(kn_rope[perm], (0, pad))), cos_t, sin_t)

    k_cat, v = pl.pallas_call(
        functools.partial(_mla_ukv_body, heads_per_tile=hpt),
        grid=(heads // hpt, M // tm),
        in_specs=[pl.BlockSpec((tm, kv_lora), lambda j, i: (i, 0)),
                  pl.BlockSpec((kv_lora, hpt * (MLA_NOPE + MLA_V)), lambda j, i: (0, j)),
                  pl.BlockSpec((1, MLA_NOPE), lambda j, i: (0, 0)),
                  pl.BlockSpec((tm, LANES), lambda j, i: (i, 0))],
        out_specs=[pl.BlockSpec((tm, hpt * dk), lambda j, i: (i, j)),
                   pl.BlockSpec((tm, hpt * MLA_V), lambda j, i: (i, j))],
        out_shape=[jax.ShapeDtypeStruct((M, heads * dk), BF16), jax.ShapeDtypeStruct((M, heads * MLA_V), BF16)],
        compiler_params=par2, name="mla_ukv",
    )(c_kv, wukv.astype(BF16), row(kn_nope), k_rope)

    q_rows = M if need_ctx else lay.B * lay.S
    r0 = (M - q_rows) // tm
    c_q = pl.pallas_call(
        _mla_dq_body,
        grid=(q_rows // tm,),
        in_specs=[pl.BlockSpec((tm, D), lambda i: (i + r0, 0)),
                  pl.BlockSpec((D, q_lora), lambda i: (0, 0)),
                  pl.BlockSpec((1, q_lora), lambda i: (0, 0))],
        out_specs=pl.BlockSpec((tm, q_lora), lambda i: (i, 0)),
        out_shape=jax.ShapeDtypeStruct((q_rows, q_lora), BF16),
        compiler_params=par1, name="mla_dq",
    )(u, wdq.astype(BF16), row(qnorm))

    w3 = wuq.reshape(q_lora, heads, MLA_NOPE + MLA_ROPE)
    w_nope = w3[:, :, :MLA_NOPE].reshape(q_lora, heads * MLA_NOPE)
    w_rope = w3[:, :, MLA_NOPE:][:, :, perm].reshape(q_lora, heads * MLA_ROPE)
    qscale = math.log2(math.e) / math.sqrt(MLA_NOPE + MLA_ROPE)
    q_cat = pl.pallas_call(
        functools.partial(_mla_uq_body, heads_per_tile=hpt, qscale=qscale),
        grid=(heads // hpt, q_rows // tm),
        in_specs=[pl.BlockSpec((tm, q_lora), lambda j, i: (i, 0)),
                  pl.BlockSpec((q_lora, hpt * MLA_NOPE), lambda j, i: (0, j)),
                  pl.BlockSpec((q_lora, hpt * MLA_ROPE), lambda j, i: (0, j)),
                  pl.BlockSpec((1, MLA_NOPE), lambda j, i: (0, 0)),
                  pl.BlockSpec((1, LANES), lambda j, i: (0, 0)),
                  pl.BlockSpec((tm, LANES), lambda j, i: (i + r0, 0)),
                  pl.BlockSpec((tm, LANES), lambda j, i: (i + r0, 0))],
        out_specs=pl.BlockSpec((tm, hpt * dk), lambda j, i: (i, j)),
        out_shape=jax.ShapeDtypeStruct((q_rows, heads * dk), BF16),
        compiler_params=par2, name="mla_uq",
    )(c_q, w_nope.astype(BF16), w_rope.astype(BF16), row(qn_nope), row(jnp.tile(qn_rope[perm], LANES // MLA_ROPE)),
      cos_t, sin_t)

    o_lat = _attn(q_cat, k_cat, v, heads, lay, q_row0=q_rows - lay.B * lay.S, n_q_rows=lay.S, with_lat_keys=True)
    if not need_ctx:
        return o_lat
    o_ctx = _attn(q_cat, k_cat, v, heads, lay, q_row0=0, n_q_rows=lay.C, with_lat_keys=False)
    return jnp.concatenate([o_ctx, o_lat], axis=0)


def kernel(x, c, ctx, c_ctx, mod_w, mod_b, norm_g, mlp_w1, mlp_w2, rw_mu, rw_wr, rw_wk, rw_wv, rw_wo, rw_w0, rw_w1, rw_w2, rw_a0, rw_a1, rw_a2, rw_g1, rw_g2, rw_kk, rw_ka, rw_rk, rw_lnx_g, rw_lnx_b, rw_v0, rw_v1, rw_v2, mla_wdq, mla_qnorm, mla_wuq, mla_wdkv, mla_kvnorm, mla_wukv, mla_qn_nope, mla_qn_rope, mla_kn_nope, mla_kn_rope, mla_wo):
    B, S, D = x.shape
    C = ctx.shape[1]
    depth = mod_w.shape[0]
    lay = _Layout(B, C, S)
    rope = _rope_tables(lay)

    sc_all = jnp.concatenate([jax.nn.silu(c_ctx)[None], jax.nn.silu(c)], axis=0)
    h = jnp.concatenate([ctx.reshape(B * C, D), x.reshape(B * S, D)], axis=0)
    v_first = None
    mlp_w2_bf16 = mlp_w2.astype(BF16)

    for i in range(depth):
        last = i == depth - 1
        j = i // 2
        mod = (_mm(sc_all, mod_w, layer=i, name="adaln") + mod_b[i]).reshape(B + 1, N_MOD, D)
        if i % 2 == 0:
            pw = (rw_mu[j], rw_w0[j], rw_w1[j], rw_w2[j],
                  rw_a0[j], rw_a1[j], rw_a2[j], rw_g1[j], rw_g2[j], rw_kk[j], rw_ka[j])
            vres = None if j == 0 else (rw_v0[j - 1], rw_v1[j - 1], rw_v2[j - 1])
            o, v_cur = _rwkv_mixer(h, norm_g[i, 0], mod, lay, v_first, j, (rw_wr, rw_wk, rw_wv), pw, vres,
                                   rw_rk[j], rw_lnx_g[j], rw_lnx_b[j])
            if j == 0:
                v_first = v_cur
            wo = rw_wo
        else:
            u = _norm_mod(h, norm_g[i, 0], mod, lay, shift_row=0, scale_row=1, out_dtype=BF16)
            o = _mla_mixer(u, lay, rope, mla_wdq[j], mla_qnorm[j], mla_wuq[j], mla_wdkv[j], mla_kvnorm[j],
                           mla_wukv[j], mla_qn_nope[j], mla_qn_rope[j], mla_kn_nope[j], mla_kn_rope[j],
                           need_ctx=not last)
            wo = mla_wo
        res_row0 = 0
        if last:
            if o.shape[0] != B * S:
                o = o[lay.n_ctx:]
            res_row0 = lay.n_ctx
            lay = _Layout(B, C, S, with_ctx=False)
        h = _mm(o, wo, layer=j, res=h, res_row0=res_row0, gate=mod[:, 2], lay=lay, name="mix_out")
        u2 = _norm_mod(h, norm_g[i, 1], mod, lay, shift_row=3, scale_row=4, out_dtype=BF16)
        hid = _mm(u2, mlp_w1, layer=i, act="relu2", out_dtype=BF16, lay=lay, name="mlp_up")
        h = _mm(hid, mlp_w2_bf16, layer=i, res=h, gate=mod[:, 5], lay=lay, name="mlp_down")
    return h.reshape(B, S, D)
```

```python
import functools
import math

import jax
import jax.numpy as jnp
from jax import lax
from jax.experimental import pallas as pl
from jax.experimental.pallas import tpu as pltpu

F32 = jnp.float32
BF16 = jnp.bfloat16

NORM_EPS = 1e-6
N_MOD = 6
GRID_W = 64
RW_HEAD = 64
RW_GN_EPS = 64e-5
RW_L2_EPS = 1e-12
MLA_NOPE = 128
MLA_ROPE = 64
MLA_V = 128
ROPE_THETA = 10000.0

LANES = 128
SUBLANES = 8
WKV_CHUNK = 64
WKV_STAGGER = 3
VMEM_LIMIT_BYTES = 56 * 1024 * 1024

_NT = (((1,), (1,)), ((), ()))


def _pick(n, prefs):
    for p in prefs:
        if n % p == 0:
            return p
    return n


def _params(sem):
    return pltpu.CompilerParams(dimension_semantics=sem, vmem_limit_bytes=VMEM_LIMIT_BYTES)


class _Layout:
    def __init__(self, B, C, S, with_ctx=True):
        self.B, self.C, self.S = B, C, S
        self.n_ctx = B * C if with_ctx else 0
        self.M = self.n_ctx + B * S

    def row_tile(self):
        return _pick(math.gcd(self.n_ctx, self.S) if self.n_ctx else self.S, (1024, 512, 256, 128, 64, 32, 16, 8))

    def seq_tile(self):
        return _pick(math.gcd(self.C, self.S) if self.n_ctx else self.S, (256, 128, 64, 32, 16, 8))

    def mod_row(self, i, tm):
        n_ctx_tiles = self.n_ctx // tm
        per_b = self.S // tm
        lat = 1 + (i - n_ctx_tiles) // per_b
        if n_ctx_tiles == 0:
            return lat
        return jnp.where(i < n_ctx_tiles, 0, lat)


def _mm_body(*refs, nk, act, has_gate, has_blend, cache_w):
    if has_gate:
        x_ref, w_ref, res_ref, gate_ref, o_ref, *scratch = refs
    elif has_blend:
        x_ref, w_ref, cur_ref, first_ref, bias_ref, o_ref, *scratch = refs
    else:
        x_ref, w_ref, o_ref, *scratch = refs

    def finish(acc):
        if act == "relu2":
            acc = jnp.square(jnp.maximum(acc, 0.0))
        elif act == "sigmoid":
            acc = jax.nn.sigmoid(acc)
        elif act == "tanh":
            acc = jnp.tanh(acc)
        if has_gate:
            acc = res_ref[...] + gate_ref[0] * acc
        if has_blend:
            cur = cur_ref[...]
            acc = cur + (first_ref[...] - cur) * jax.nn.sigmoid(bias_ref[...] + acc)
        o_ref[...] = acc.astype(o_ref.dtype)

    if cache_w:
        wc_ref = scratch[-1]

        @pl.when(pl.program_id(1) == 0)
        def _():
            wc_ref[...] = w_ref[...].astype(BF16)

        w = wc_ref[...]
    else:
        w = w_ref[...].astype(BF16)
    part = jnp.dot(x_ref[...].astype(BF16), w, preferred_element_type=F32)
    if nk == 1:
        finish(part)
    else:
        acc_ref = scratch[0]
        k = pl.program_id(2)

        @pl.when(k == 0)
        def _():
            acc_ref[...] = part

        @pl.when(k > 0)
        def _():
            acc_ref[...] += part

        @pl.when(k == nk - 1)
        def _():
            finish(acc_ref[...])


def _mm(x, w, *, layer=None, act=None, out_dtype=F32, res=None, res_row0=0, gate=None, blend=None, lay=None,
        tm=None, x_kblock=0, name="mm"):
    M = x.shape[0]
    K, N = w.shape[-2:]
    if tm is None:
        tm = lay.row_tile() if lay is not None else _pick(M, (1024, 512, 256, 128, 64, 32, 16, 8))
    tn = _pick(N, (1024, 512, 256, 128))
    tk = K if K <= 2048 else _pick(K, (2048, 1024, 512))
    nk = K // tk
    has_gate = gate is not None
    has_blend = blend is not None
    cache_w = w.dtype == F32 and nk == 1 and M // tm > 1
    if w.ndim == 3:
        w_spec = pl.BlockSpec((None, tk, tn), lambda j, i, k: (layer, k, j))
    else:
        w_spec = pl.BlockSpec((tk, tn), lambda j, i, k: (k, j))
    in_specs = [pl.BlockSpec((tm, tk), lambda j, i, k: (i, k + x_kblock * nk)), w_spec]
    args = [x, w]
    if has_gate:
        in_specs += [pl.BlockSpec((tm, tn), lambda j, i, k: (i + res_row0 // tm, j)),
                     pl.BlockSpec((1, 1, tn), lambda j, i, k: (lay.mod_row(i, tm), 0, j))]
        args += [res, gate.reshape(gate.shape[0], 1, N)]
    if has_blend:
        in_specs += [pl.BlockSpec((tm, tn), lambda j, i, k: (i, j)), pl.BlockSpec((tm, tn), lambda j, i, k: (i, j)),
                     pl.BlockSpec((1, tn), lambda j, i, k: (0, j))]
        args += [blend[0], blend[1], blend[2].reshape(1, N)]
    scratch = [pltpu.VMEM((tm, tn), F32)] if nk > 1 else []
    if cache_w:
        scratch.append(pltpu.VMEM((tk, tn), BF16))
    return pl.pallas_call(
        functools.partial(_mm_body, nk=nk, act=act, has_gate=has_gate, has_blend=has_blend, cache_w=cache_w),
        grid=(N // tn, M // tm, nk),
        in_specs=in_specs,
        out_specs=pl.BlockSpec((tm, tn), lambda j, i, k: (i, j)),
        out_shape=jax.ShapeDtypeStruct((M, N), out_dtype),
        scratch_shapes=scratch,
        compiler_params=_params(("parallel", "arbitrary", "arbitrary")),
        name=name,
    )(*args)


def _norm_mod_rows(x, gain, mod, shift_row, scale_row):
    ms = jnp.mean(x * x, axis=-1, keepdims=True)
    y = x * lax.rsqrt(ms + NORM_EPS) * gain
    return y * (1.0 + mod[scale_row:scale_row + 1]) + mod[shift_row:shift_row + 1]


def _norm_mod_body(x_ref, g_ref, mod_ref, o_ref, *, shift_row, scale_row):
    o_ref[...] = _norm_mod_rows(x_ref[...], g_ref[...], mod_ref[0], shift_row, scale_row).astype(o_ref.dtype)


def _norm_mod(x, g, mod, lay, *, shift_row, scale_row, out_dtype):
    M, D = x.shape
    tm = min(lay.row_tile(), 512)
    return pl.pallas_call(
        functools.partial(_norm_mod_body, shift_row=shift_row, scale_row=scale_row),
        grid=(M // tm,),
        in_specs=[pl.BlockSpec((tm, D), lambda i: (i, 0)),
                  pl.BlockSpec((1, D), lambda i: (0, 0)),
                  pl.BlockSpec((1, N_MOD, D), lambda i: (lay.mod_row(i, tm), 0, 0))],
        out_specs=pl.BlockSpec((tm, D), lambda i: (i, 0)),
        out_shape=jax.ShapeDtypeStruct((M, D), out_dtype),
        compiler_params=_params(("parallel",)),
        name="norm_mod",
    )(x, g.reshape(1, D), mod)


def _rw_pre_body(h_ref, hp_ref, hn_ref, g_ref, mod_ref, mu_ref, *o_refs, tm, n_ctx_tiles, ctx_tiles, lat_tiles):
    i = pl.program_id(0)
    D = h_ref.shape[1]
    slabs = [slice(c, c + LANES) for c in range(0, D, LANES)]
    above = slice(SUBLANES - 1, SUBLANES)
    below = slice(0, 1)

    def inv_rms(ref, rows):
        sq = None
        for cs in slabs:
            x = ref[rows, cs]
            sq = x * x if sq is None else sq + x * x
        return lax.rsqrt(jnp.sum(sq, axis=-1, keepdims=True) * (1.0 / D) + NORM_EPS)

    r_tile, r_above, r_below = inv_rms(h_ref, slice(None)), inv_rms(hp_ref, above), inv_rms(hn_ref, below)
    in_ctx = i < n_ctx_tiles
    pos = jnp.where(in_ctx, i % ctx_tiles, (i - n_ctx_tiles) % lat_tiles)
    last = jnp.where(in_ctx, ctx_tiles - 1, lat_tiles - 1)
    rows = lax.broadcasted_iota(jnp.int32, (tm, 1), 0)
    for cs in slabs:
        gain, shift, scale1 = g_ref[:, cs], mod_ref[0, 0:1, cs], 1.0 + mod_ref[0, 1:2, cs]
        u = h_ref[:, cs] * r_tile * gain * scale1 + shift
        u_above = jnp.where(pos == 0, 0.0, hp_ref[above, cs] * r_above * gain * scale1 + shift)
        u_below = jnp.where(pos == last, 0.0, hn_ref[below, cs] * r_below * gain * scale1 + shift)
        prev = jnp.where(rows == 0, u_above, pltpu.roll(u, 1, axis=0))
        nxt = jnp.where(rows == tm - 1, u_below, pltpu.roll(u, tm - 1, axis=0))
        xx = 0.5 * (prev + nxt) - u
        for j, o_ref in enumerate(o_refs):
            o_ref[:, cs] = (u + xx * mu_ref[j:j + 1, cs]).astype(o_ref.dtype)


def _rw_pre(h, g, mod, mu, lay):
    M, D = h.shape
    tm = lay.seq_tile()
    nmix = mu.shape[0]
    per8 = tm // SUBLANES
    last8 = M // SUBLANES - 1
    return pl.pallas_call(
        functools.partial(_rw_pre_body, tm=tm, n_ctx_tiles=lay.n_ctx // tm, ctx_tiles=max(lay.C // tm, 1),
                          lat_tiles=lay.S // tm),
        grid=(M // tm,),
        in_specs=[pl.BlockSpec((tm, D), lambda i: (i, 0)),
                  pl.BlockSpec((SUBLANES, D), lambda i: (jnp.maximum(i * per8 - 1, 0), 0)),
                  pl.BlockSpec((SUBLANES, D), lambda i: (jnp.minimum((i + 1) * per8, last8), 0)),
                  pl.BlockSpec((1, D), lambda i: (0, 0)),
                  pl.BlockSpec((1, N_MOD, D), lambda i: (lay.mod_row(i, tm), 0, 0)),
                  pl.BlockSpec((nmix, D), lambda i: (0, 0))],
        out_specs=[pl.BlockSpec((tm, D), lambda i: (i, 0))] * nmix,
        out_shape=[jax.ShapeDtypeStruct((M, D), BF16)] * nmix,
        compiler_params=_params(("parallel",)),
        name="rw_pre",
    )(h, h, h, g.reshape(1, D), mod, mu)


def _wkv_body(*refs, reverse, npairs, nsub, epilogue):
    if epilogue:
        (r_ref, k_ref, v_ref, wl_ref, al_ref, w0_ref, a0_ref, kkp_ref, ka_ref,
         alo_ref, a0o_ref, yo_ref, g_ref, rk_ref, lng_ref, lnb_ref, o_ref, s_ref) = refs
    else:
        r_ref, k_ref, v_ref, wl_ref, al_ref, w0_ref, a0_ref, kkp_ref, ka_ref, o_ref, s_ref = refs
    L = WKV_CHUNK
    H = RW_HEAD

    @pl.when(pl.program_id(2) == 0)
    def _():
        s_ref[...] = jnp.zeros_like(s_ref)

    def order(row, col):
        return (row <= col) if reverse else (row >= col)

    row = lax.broadcasted_iota(jnp.int32, (L, L), 0)
    col = lax.broadcasted_iota(jnp.int32, (L, L), 1)
    tri = jnp.where(order(row, col), 1.0, 0.0).astype(BF16)

    prow = lax.broadcasted_iota(jnp.int32, (L, 2 * L), 0)
    pcol = lax.broadcasted_iota(jnp.int32, (L, 2 * L), 1) & (L - 1)
    incl = order(prow, pcol)
    strict = incl & (prow != pcol)
    incl2 = jnp.concatenate([incl, incl], axis=1)
    eye = jnp.where(prow == pcol, 1.0, 0.0)

    def sibling(s):
        return ((prow // (2 * s)) == (pcol // (2 * s))) & ((prow // s) != (pcol // s))

    bmask = (lax.broadcasted_iota(jnp.int32, (2 * L, LANES), 0) // L
             == lax.broadcasted_iota(jnp.int32, (2 * L, LANES), 1) // H)
    head_ones = jnp.where(lax.broadcasted_iota(jnp.int32, (LANES, LANES), 0) // H
                          == lax.broadcasted_iota(jnp.int32, (LANES, LANES), 1) // H, 1.0, 0.0).astype(BF16)

    def bdf(x):
        return jnp.where(bmask, jnp.concatenate([x, x], axis=0), 0.0)

    def bd(x):
        return bdf(x).astype(BF16)

    def dot(a, b):
        return jnp.dot(a, b, preferred_element_type=F32)

    def cat0(*xs):
        return jnp.concatenate(xs, axis=0)

    def cat1(*xs):
        return jnp.concatenate(xs, axis=1)

    prs = range(npairs)
    state = [s_ref[p] for p in prs]

    def chunk_stages(ci):
        rows = slice(ci * L, (ci + 1) * L)

        def tiles(ref):
            return [ref[rows, p * LANES:(p + 1) * LANES] for p in prs]

        def vecs(ref):
            return [ref[:, p * LANES:(p + 1) * LANES] for p in prs]

        def head_sum(xs):
            tot = dot(cat0(*xs).astype(BF16), head_ones)
            return [tot[p * L:(p + 1) * L] for p in prs]

        def store(vals):
            for p, val in zip(prs, vals):
                o_ref[rows, p * LANES:(p + 1) * LANES] = val.astype(o_ref.dtype)

        r, k, v = tiles(r_ref), tiles(k_ref), tiles(v_ref)
        ka = vecs(ka_ref)

        a = [jax.nn.sigmoid(a0 + al) for a0, al in zip(vecs(a0_ref), tiles(al_ref))]
        z = [-(w0 + wl) for w0, wl in zip(vecs(w0_ref), tiles(wl_ref))]
        softplus = [jnp.maximum(x, 0.0) + jnp.log(1.0 + jnp.exp(-jnp.abs(x))) for x in z]
        lw = [-jnp.exp(-x - 0.5) for x in softplus]
        yield
        kraw = [k[p] * kkp for p, kkp in zip(prs, vecs(kkp_ref))]
        norm2 = head_sum([x * x for x in kraw])
        kd = [k[p] * (1.0 + (a[p] - 1.0) * ka[p]) for p in prs]
        hi = [x.astype(BF16) for x in lw]
        rem = [lw[p] - hi[p].astype(F32) for p in prs]
        mid = [x.astype(BF16) for x in rem]
        lo = [(rem[p] - mid[p].astype(F32)).astype(BF16) for p in prs]
        g = [dot(tri, hi[p]) + dot(tri, mid[p]) + dot(tri, lo[p]) for p in prs]
        ee = [jnp.exp(jnp.sum(x, axis=0, keepdims=True)) for x in lw]
        yield
        kk = [kraw[p] / jnp.maximum(jnp.sqrt(norm2[p]), RW_L2_EPS) for p in prs]
        en = [jnp.exp(-x) for x in g]
        abar = [-(kk[p] * jnp.exp(g[p] - lw[p])) for p in prs]
        rbar = [r[p] * jnp.exp(g[p]) for p in prs]
        bt = [kk[p] * a[p] * en[p] for p in prs]
        kt = [kd[p] * en[p] for p in prs]
        yield
        amat = [lax.dot_general(cat0(abar[p], rbar[p]).astype(BF16), cat0(bd(bt[p]), bd(kt[p])), _NT,
                                preferred_element_type=F32) for p in prs]
        vbd = [bd(x) for x in v]
        gam = [jnp.broadcast_to(x, (LANES, LANES)).T for x in ee]
        upd_lhs = [cat1(bdf(bt[p] * ee[p]).T, bdf(kt[p] * ee[p]).T).astype(BF16) for p in prs]
        yield
        a_ab = [jnp.where(strict, x[:L, :2 * L], 0.0) for x in amat]
        a_ak = [jnp.where(strict, x[:L, 2 * L:], 0.0).astype(BF16) for x in amat]
        a_rbk = [jnp.where(incl2, x[L:], 0.0).astype(BF16) for x in amat]
        av = [dot(a_ak[p], vbd[p]) for p in prs]

        t = [eye + jnp.where(sibling(1), x, 0.0) for x in a_ab]
        s = 2
        while s < L:
            sib = sibling(s)
            half = [dot(t[p].astype(BF16), bd(jnp.where(sib, a_ab[p], 0.0))).astype(BF16) for p in prs]
            yield
            t = [t[p] + dot(half[p], bd(t[p])) for p in prs]
            yield
            s *= 2

        wu = [dot(t[p].astype(BF16), cat1(bd(abar[p]), bd(av[p]))) for p in prs]
        yield
        wr_lhs = [cat0(wu[p][:, :LANES], rbar[p]).astype(BF16) for p in prs]
        ut = [x[:, LANES:] for x in wu]
        c2_lhs = [cat0(a_rbk[p], upd_lhs[p]) for p in prs]
        yield

        wr = [dot(wr_lhs[p], state[p].astype(BF16)) for p in prs]
        yield
        u = [wr[p][:L] + ut[p] for p in prs]
        out2 = [dot(c2_lhs[p], cat0(bd(u[p]), vbd[p])) for p in prs]
        yield
        y = [wr[p][L:] + out2[p][:L] for p in prs]
        for p in prs:
            state[p] = state[p] * gam[p] + out2[p][L:]
        if not epilogue:
            store(y)
            return
        yield

        inv_n = 1.0 / H
        ytot = [y[p] + yo for p, yo in zip(prs, tiles(yo_ref))]
        mean = head_sum(ytot)
        a_o = [jax.nn.sigmoid(a0 + al) for a0, al in zip(vecs(a0o_ref), tiles(alo_ref))]
        kd_sum = [kd[p] + k[p] * (1.0 + (a_o[p] - 1.0) * ka[p]) for p in prs]
        bonus = head_sum([r[p] * rk * kd_sum[p] for p, rk in zip(prs, vecs(rk_ref))])
        yield
        dev = [ytot[p] - mean[p] * inv_n for p in prs]
        var = head_sum([x * x for x in dev])
        yield
        lng, lnb, gate = vecs(lng_ref), vecs(lnb_ref), tiles(g_ref)
        store([(dev[p] * lax.rsqrt(var[p] * inv_n + RW_GN_EPS) * lng[p] + lnb[p] + bonus[p] * v[p]) * gate[p]
               for p in prs])

    waiting = [chunk_stages(ci) for ci in (reversed(range(nsub)) if reverse else range(nsub))]
    running = []
    step = 0
    while waiting or running:
        if waiting and step % WKV_STAGGER == 0:
            running.append(waiting.pop(0))
        for gen in list(running):
            if next(gen, "done") == "done":
                running.remove(gen)
        step += 1
    for p in prs:
        s_ref[p] = state[p]


def _wkv(r, k, v, wl, al, w0, a0, kkp, ka, lay, *, reverse, epilogue=None):
    M, D = r.shape
    L = WKV_CHUNK
    td = min(D, 8 * LANES)
    npairs = td // LANES
    nsub = _pick(math.gcd(lay.C, lay.S) // L, (4, 2, 1))
    tb = nsub * L
    ctx_blk = lay.C // tb
    lat_blk = lay.S // tb
    ctx_total = lay.n_ctx // tb

    def tok_block(b, c):
        if reverse:
            ctx_c = ctx_blk - 1 - c
            lat_c = lat_blk - 1 - (c - ctx_blk)
        else:
            ctx_c = c
            lat_c = c - ctx_blk
        return jnp.where(c < ctx_blk, b * ctx_blk + ctx_c, ctx_total + b * lat_blk + lat_c)

    mat = pl.BlockSpec((tb, td), lambda b, d, c: (tok_block(b, c), d))
    vec = pl.BlockSpec((1, td), lambda b, d, c: (0, d))
    row = lambda x: x.reshape(1, D)
    args = [r, k, v, wl, al, row(w0), row(a0), row(kkp), row(ka)]
    specs = [mat] * 5 + [vec] * 4
    if epilogue is not None:
        al_o, a0_o, y_o, gate, rk, lnx_g, lnx_b = epilogue
        args += [al_o, row(a0_o), y_o, gate, row(rk), row(lnx_g), row(lnx_b)]
        specs += [mat, vec, mat, mat, vec, vec, vec]
    return pl.pallas_call(
        functools.partial(_wkv_body, reverse=reverse, npairs=npairs, nsub=nsub, epilogue=epilogue is not None),
        grid=(lay.B, D // td, ctx_blk + lat_blk),
        in_specs=specs,
        out_specs=mat,
        out_shape=jax.ShapeDtypeStruct((M, D), F32 if epilogue is None else BF16),
        scratch_shapes=[pltpu.VMEM((npairs, LANES, LANES), F32)],
        compiler_params=_params(("parallel", "parallel", "arbitrary")),
        name="wkv_bwd" if reverse else "wkv_fwd",
    )(*args)


ATTN_KEY_CHUNK = 512


def _attn_body(q_ref, *refs, n_kv):
    k_refs, v_refs = refs[:n_kv], refs[n_kv:2 * n_kv]
    o_ref = refs[2 * n_kv]
    q = q_ref[...]
    pieces = []
    for k_ref, v_ref in zip(k_refs, v_refs):
        n = k_ref.shape[0]
        for lo in range(0, n, ATTN_KEY_CHUNK):
            pieces.append((k_ref, v_ref, lo, min(lo + ATTN_KEY_CHUNK, n)))

    def scores(piece):
        k_ref, _, lo, hi = piece
        return lax.dot_general(q, k_ref[lo:hi, :], _NT, preferred_element_type=F32)

    def lane_blocks(x):
        return [x[:, c:c + LANES] for c in range(0, x.shape[1], LANES)]

    s_next = scores(pieces[0])
    m = l = acc = None
    for c, (_, v_ref, lo, hi) in enumerate(pieces):
        s = s_next
        if c + 1 < len(pieces):
            s_next = scores(pieces[c + 1])
        m_c = jnp.max(functools.reduce(jnp.maximum, lane_blocks(s)), axis=-1, keepdims=True)
        m_new = m_c if m is None else jnp.maximum(m, m_c)
        p = jnp.exp2(s - m_new)
        l_c = functools.reduce(jnp.add, lane_blocks(p))
        pv = jnp.dot(p.astype(BF16), v_ref[lo:hi, :], preferred_element_type=F32)
        if m is None:
            l, acc = l_c, pv
        else:
            alpha = jnp.exp2(m - m_new)
            l, acc = alpha * l + l_c, alpha * acc + pv
        m = m_new
    o_ref[...] = (acc / jnp.sum(l, axis=-1, keepdims=True)).astype(o_ref.dtype)


def _attn(q, k, v, heads, lay, *, q_row0, n_q_rows, with_lat_keys):
    dk = k.shape[1] // heads
    dv = v.shape[1] // heads
    tq = _pick(n_q_rows, (512, 256, 128, 64, 32, 16, 8))
    per_b = n_q_rows // tq
    q0 = q_row0 // tq
    kv = [(lay.C, 0)]
    if with_lat_keys:
        kv.append((lay.S, lay.n_ctx // lay.S))
        k_lat, v_lat = k, v
        if lay.n_ctx % lay.S:
            k_lat, v_lat, kv[1] = k[lay.n_ctx:], v[lay.n_ctx:], (lay.S, 0)
    in_specs = [pl.BlockSpec((tq, dk), lambda b, h, i: (q0 + b * per_b + i, h))]
    in_specs += [pl.BlockSpec((n, dk), functools.partial(lambda b, h, i, base: (base + b, h), base=base))
                 for n, base in kv]
    in_specs += [pl.BlockSpec((n, dv), functools.partial(lambda b, h, i, base: (base + b, h), base=base))
                 for n, base in kv]
    operands = [q, k] + ([k_lat] if with_lat_keys else []) + [v] + ([v_lat] if with_lat_keys else [])
    return pl.pallas_call(
        functools.partial(_attn_body, n_kv=len(kv)),
        grid=(lay.B, heads, per_b),
        in_specs=in_specs,
        out_specs=pl.BlockSpec((tq, dv), lambda b, h, i: (b * per_b + i, h)),
        out_shape=jax.ShapeDtypeStruct((lay.B * n_q_rows, heads * dv), BF16),
        compiler_params=_params(("parallel", "parallel", "arbitrary")),
        name="attn",
    )(*operands)


def _rms_rows(x, gain):
    return x * lax.rsqrt(jnp.mean(x * x, axis=-1, keepdims=True) + NORM_EPS) * gain


def _rope_lanes(x, cos, sin):
    lane = lax.broadcasted_iota(jnp.int32, x.shape, 1)
    first = (lane % MLA_ROPE) < MLA_ROPE // 2
    partner = jnp.where(first, pltpu.roll(x, LANES - MLA_ROPE // 2, axis=1), pltpu.roll(x, MLA_ROPE // 2, axis=1))
    return x * cos + partner * sin


def _mla_dkv_body(u_ref, w_ref, g_ref, gr_ref, cos_ref, sin_ref, ckv_ref, kr_ref, *, kv_lora):
    acc = jnp.dot(u_ref[...], w_ref[...], preferred_element_type=F32)
    ckv_ref[...] = _rms_rows(acc[:, :kv_lora], g_ref[...]).astype(ckv_ref.dtype)
    kr = acc[:, kv_lora:]
    ms = jnp.sum(kr * kr, axis=-1, keepdims=True) * (1.0 / MLA_ROPE)
    kr = kr * lax.rsqrt(ms + NORM_EPS) * gr_ref[...]
    kr_ref[...] = _rope_lanes(kr, cos_ref[...], sin_ref[...]).astype(kr_ref.dtype)


def _rms_groups(x, gain, group):
    ones = jnp.where(lax.broadcasted_iota(jnp.int32, (LANES, LANES), 0) // group
                     == lax.broadcasted_iota(jnp.int32, (LANES, LANES), 1) // group, 1.0, 0.0).astype(BF16)
    ss = jnp.dot((x * x).astype(BF16), ones, preferred_element_type=F32)
    return x * lax.rsqrt(ss * (1.0 / group) + NORM_EPS) * gain


def _mla_ukv_body(c_ref, w_ref, g_ref, kr_ref, kcat_ref, v_ref, *, heads_per_tile):
    acc = jnp.dot(c_ref[...], w_ref[...], preferred_element_type=F32)
    kr = kr_ref[...]
    dk = MLA_NOPE + LANES
    for h in range(heads_per_tile):
        base = h * (MLA_NOPE + MLA_V)
        kcat_ref[:, h * dk:h * dk + MLA_NOPE] = _rms_groups(acc[:, base:base + MLA_NOPE], g_ref[...],
                                                            MLA_NOPE).astype(kcat_ref.dtype)
        kcat_ref[:, h * dk + MLA_NOPE:(h + 1) * dk] = kr
        v_ref[:, h * MLA_V:(h + 1) * MLA_V] = acc[:, base + MLA_NOPE:base + MLA_NOPE + MLA_V].astype(v_ref.dtype)


def _mla_dq_body(u_ref, w_ref, g_ref, o_ref):
    acc = jnp.dot(u_ref[...], w_ref[...], preferred_element_type=F32)
    o_ref[...] = _rms_rows(acc, g_ref[...]).astype(o_ref.dtype)


def _mla_uq_body(c_ref, wn_ref, wr_ref, gn_ref, gr_ref, cos_ref, sin_ref, q_ref, *, heads_per_tile, qscale):
    x = c_ref[...]
    nope = jnp.dot(x, wn_ref[...], preferred_element_type=F32)
    rope = jnp.dot(x, wr_ref[...], preferred_element_type=F32)
    low = lax.broadcasted_iota(jnp.int32, (x.shape[0], LANES), 1) < MLA_ROPE
    dk = MLA_NOPE + LANES
    for b in range(heads_per_tile // 2):
        xr = _rms_groups(rope[:, b * LANES:(b + 1) * LANES], gr_ref[...], MLA_ROPE)
        xr = _rope_lanes(xr, cos_ref[...], sin_ref[...]) * qscale
        for t in range(2):
            h = 2 * b + t
            qn = _rms_groups(nope[:, h * MLA_NOPE:(h + 1) * MLA_NOPE], gn_ref[...], MLA_NOPE) * qscale
            q_ref[:, h * dk:h * dk + MLA_NOPE] = qn.astype(q_ref.dtype)
            rr = xr if t == 0 else pltpu.roll(xr, MLA_ROPE, axis=1)
            q_ref[:, h * dk + MLA_NOPE:(h + 1) * dk] = jnp.where(low, rr, 0.0).astype(q_ref.dtype)


def _pad_cols(w, n):
    return jnp.pad(w, ((0, 0), (0, n - w.shape[1])))


def _pad_rows(w, n):
    return jnp.pad(w, ((0, n - w.shape[0]), (0, 0)))


def _up128(n):
    return -(-n // LANES) * LANES


def _rope_tables(lay):
    n = lay.S
    rows = n // GRID_W
    row = jnp.broadcast_to(jnp.arange(rows)[:, None], (rows, GRID_W)).reshape(-1)
    col = jnp.broadcast_to(jnp.arange(GRID_W)[None, :], (rows, GRID_W)).reshape(-1)
    n_freq = MLA_ROPE // 4
    inv = ROPE_THETA ** (-jnp.arange(n_freq, dtype=F32) / n_freq)
    ang = jnp.concatenate([row[:, None].astype(F32) * inv, col[:, None].astype(F32) * inv], axis=-1)
    cos, sin = jnp.cos(ang), jnp.sin(ang)
    reps = LANES // MLA_ROPE
    cos_l = jnp.tile(jnp.concatenate([cos, cos], axis=1), (lay.B, reps))
    sin_l = jnp.tile(jnp.concatenate([-sin, sin], axis=1), (lay.B, reps))
    cos_t = jnp.concatenate([jnp.ones((lay.n_ctx, LANES), F32), cos_l], axis=0)
    sin_t = jnp.concatenate([jnp.zeros((lay.n_ctx, LANES), F32), sin_l], axis=0)
    return cos_t, sin_t


def _rwkv_mixer(h, norm_g, mod, lay, v_first, j, big_w, pw, vres, rk, lnx_g, lnx_b):
    wr, wk, wv = big_w
    mu, w0, w1, w2, a0, a1, a2, g1, g2, kkp, ka = pw
    xr, xw, xk, xv, xa, xg = _rw_pre(h, norm_g, mod, mu, lay)
    mm = functools.partial(_mm, lay=lay)
    r = mm(xr, wr, layer=j, name="rw_r")
    k = mm(xk, wk, layer=j, name="rw_k")
    v = mm(xv, wv, layer=j, name="rw_v")
    if vres is not None:
        v0, v1, v2 = vres
        n = _up128(v1.shape[1])
        v = mm(mm(xv, _pad_cols(v1, n).astype(BF16), out_dtype=BF16, name="rw_v1"),
               _pad_rows(v2, n).astype(BF16), blend=(v, v_first, v0), name="rw_v2")
    g = mm(mm(xg, g1.astype(BF16), act="sigmoid", out_dtype=BF16, name="rw_g1"), g2.astype(BF16),
           out_dtype=BF16, name="rw_g2")

    nw = _up128(w1.shape[2])
    w_mid = mm(xw, jnp.concatenate([_pad_cols(w1[e], nw) for e in range(2)], axis=1).astype(BF16),
               act="tanh", out_dtype=BF16, name="rw_w1")
    wl = [mm(w_mid, _pad_rows(w2[e], nw).astype(BF16), x_kblock=e, out_dtype=BF16, name="rw_w2") for e in range(2)]
    na = _up128(a1.shape[2])
    a_mid = mm(xa, jnp.concatenate([_pad_cols(a1[e], na) for e in range(2)], axis=1).astype(BF16),
               out_dtype=BF16, name="rw_a1")
    al = [mm(a_mid, _pad_rows(a2[e], na).astype(BF16), x_kblock=e, out_dtype=BF16, name="rw_a2") for e in range(2)]

    y_fwd = _wkv(r, k, v, wl[0], al[0], w0[0], a0[0], kkp, ka, lay, reverse=False)
    o = _wkv(r, k, v, wl[1], al[1], w0[1], a0[1], kkp, ka, lay, reverse=True,
             epilogue=(al[0], a0[0], y_fwd, g, rk.reshape(-1), lnx_g, lnx_b))
    return o, v


def _mla_mixer(u, lay, rope_tabs, wdq, qnorm, wuq, wdkv, kvnorm, wukv, qn_nope, qn_rope, kn_nope, kn_rope, need_ctx):
    M, D = u.shape
    heads = D // MLA_V
    kv_lora, q_lora = kvnorm.shape[0], qnorm.shape[0]
    cos_t, sin_t = rope_tabs
    tm = lay.row_tile()
    hpt = _pick(heads, (4, 2))
    dk = MLA_NOPE + LANES
    pad = LANES - MLA_ROPE
    perm = jnp.concatenate([jnp.arange(0, MLA_ROPE, 2), jnp.arange(1, MLA_ROPE, 2)])
    row = lambda t: t.reshape(1, -1)
    par1 = _params(("parallel",))
    par2 = _params(("parallel", "parallel"))

    w_dkv = jnp.concatenate([wdkv[:, :kv_lora], wdkv[:, kv_lora:][:, perm], jnp.zeros((D, pad), F32)], axis=1)
    c_kv, k_rope = pl.pallas_call(
        functools.partial(_mla_dkv_body, kv_lora=kv_lora),
        grid=(M // tm,),
        in_specs=[pl.BlockSpec((tm, D), lambda i: (i, 0)),
                  pl.BlockSpec((D, kv_lora + LANES), lambda i: (0, 0)),
                  pl.BlockSpec((1, kv_lora), lambda i: (0, 0)),
                  pl.BlockSpec((1, LANES), lambda i: (0, 0)),
                  pl.BlockSpec((tm, LANES), lambda i: (i, 0)),
                  pl.BlockSpec((tm, LANES), lambda i: (i, 0))],
        out_specs=[pl.BlockSpec((tm, kv_lora), lambda i: (i, 0)), pl.BlockSpec((tm, LANES), lambda i: (i, 0))],
        out_shape=[jax.ShapeDtypeStruct((M, kv_lora), BF16), jax.ShapeDtypeStruct((M, LANES), BF16)],
        compiler_params=par1, name="mla_dkv",
    )(u, w_dkv.astype(BF16), row(kvnorm), row(jnp.pad(kn_rope[perm], (0, pad))), cos_t, sin_t)

    k_cat, v = pl.pallas_call(
        functools.partial(_mla_ukv_body, heads_per_tile=hpt),
        grid=(heads // hpt, M // tm),
        in_specs=[pl.BlockSpec((tm, kv_lora), lambda j, i: (i, 0)),
                  pl.BlockSpec((kv_lora, hpt * (MLA_NOPE + MLA_V)), lambda j, i: (0, j)),
                  pl.BlockSpec((1, MLA_NOPE), lambda j, i: (0, 0)),
                  pl.BlockSpec((tm, LANES), lambda j, i: (i, 0))],
        out_specs=[pl.BlockSpec((tm, hpt * dk), lambda j, i: (i, j)),
                   pl.BlockSpec((tm, hpt * MLA_V), lambda j, i: (i, j))],
        out_shape=[jax.ShapeDtypeStruct((M, heads * dk), BF16), jax.ShapeDtypeStruct((M, heads * MLA_V), BF16)],
        compiler_params=par2, name="mla_ukv",
    )(c_kv, wukv.astype(BF16), row(kn_nope), k_rope)

    q_rows = M if need_ctx else lay.B * lay.S
    r0 = (M - q_rows) // tm
    c_q = pl.pallas_call(
        _mla_dq_body,
        grid=(q_rows // tm,),
        in_specs=[pl.BlockSpec((tm, D), lambda i: (i + r0, 0)),
                  pl.BlockSpec((D, q_lora), lambda i: (0, 0)),
                  pl.BlockSpec((1, q_lora), lambda i: (0, 0))],
        out_specs=pl.BlockSpec((tm, q_lora), lambda i: (i, 0)),
        out_shape=jax.ShapeDtypeStruct((q_rows, q_lora), BF16),
        compiler_params=par1, name="mla_dq",
    )(u, wdq.astype(BF16), row(qnorm))

    w3 = wuq.reshape(q_lora, heads, MLA_NOPE + MLA_ROPE)
    w_nope = w3[:, :, :MLA_NOPE].reshape(q_lora, heads * MLA_NOPE)
    w_rope = w3[:, :, MLA_NOPE:][:, :, perm].reshape(q_lora, heads * MLA_ROPE)
    qscale = math.log2(math.e) / math.sqrt(MLA_NOPE + MLA_ROPE)
    q_cat = pl.pallas_call(
        functools.partial(_mla_uq_body, heads_per_tile=hpt, qscale=qscale),
        grid=(heads // hpt, q_rows // tm),
        in_specs=[pl.BlockSpec((tm, q_lora), lambda j, i: (i, 0)),
                  pl.BlockSpec((q_lora, hpt * MLA_NOPE), lambda j, i: (0, j)),
                  pl.BlockSpec((q_lora, hpt * MLA_ROPE), lambda j, i: (0, j)),
                  pl.BlockSpec((1, MLA_NOPE), lambda j, i: (0, 0)),
                  pl.BlockSpec((1, LANES), lambda j, i: (0, 0)),
                  pl.BlockSpec((tm, LANES), lambda j, i: (i + r0, 0)),
                  pl.BlockSpec((tm, LANES), lambda j, i: (i + r0, 0))],
        out_specs=pl.BlockSpec((tm, hpt * dk), lambda j, i: (i, j)),
        out_shape=jax.ShapeDtypeStruct((q_rows, heads * dk), BF16),
        compiler_params=par2, name="mla_uq",
    )(c_q, w_nope.astype(BF16), w_rope.astype(BF16), row(qn_nope), row(jnp.tile(qn_rope[perm], LANES // MLA_ROPE)),
      cos_t, sin_t)

    o_lat = _attn(q_cat, k_cat, v, heads, lay, q_row0=q_rows - lay.B * lay.S, n_q_rows=lay.S, with_lat_keys=True)
    if not need_ctx:
        return o_lat
    o_ctx = _attn(q_cat, k_cat, v, heads, lay, q_row0=0, n_q_rows=lay.C, with_lat_keys=False)
    return jnp.concatenate([o_ctx, o_lat], axis=0)


def kernel(x, c, ctx, c_ctx, mod_w, mod_b, norm_g, mlp_w1, mlp_w2, rw_mu, rw_wr, rw_wk, rw_wv, rw_wo, rw_w0, rw_w1, rw_w2, rw_a0, rw_a1, rw_a2, rw_g1, rw_g2, rw_kk, rw_ka, rw_rk, rw_lnx_g, rw_lnx_b, rw_v0, rw_v1, rw_v2, mla_wdq, mla_qnorm, mla_wuq, mla_wdkv, mla_kvnorm, mla_wukv, mla_qn_nope, mla_qn_rope, mla_kn_nope, mla_kn_rope, mla_wo):
    B, S, D = x.shape
    C = ctx.shape[1]
    depth = mod_w.shape[0]
    lay = _Layout(B, C, S)
    rope = _rope_tables(lay)

    sc_all = jnp.concatenate([jax.nn.silu(c_ctx)[None], jax.nn.silu(c)], axis=0)
    h = jnp.concatenate([ctx.reshape(B * C, D), x.reshape(B * S, D)], axis=0)
    v_first = None
    mlp_w2_bf16 = mlp_w2.astype(BF16)

    for i in range(depth):
        last = i == depth - 1
        j = i // 2
        mod = (_mm(sc_all, mod_w, layer=i, name="adaln") + mod_b[i]).reshape(B + 1, N_MOD, D)
        if i % 2 == 0:
            pw = (rw_mu[j], rw_w0[j], rw_w1[j], rw_w2[j],
                  rw_a0[j], rw_a1[j], rw_a2[j], rw_g1[j], rw_g2[j], rw_kk[j], rw_ka[j])
            vres = None if j == 0 else (rw_v0[j - 1], rw_v1[j - 1], rw_v2[j - 1])
            o, v_cur = _rwkv_mixer(h, norm_g[i, 0], mod, lay, v_first, j, (rw_wr, rw_wk, rw_wv), pw, vres,
                                   rw_rk[j], rw_lnx_g[j], rw_lnx_b[j])
            if j == 0:
                v_first = v_cur
            wo = rw_wo
        else:
            u = _norm_mod(h, norm_g[i, 0], mod, lay, shift_row=0, scale_row=1, out_dtype=BF16)
            o = _mla_mixer(u, lay, rope, mla_wdq[j], mla_qnorm[j], mla_wuq[j], mla_wdkv[j], mla_kvnorm[j],
                           mla_wukv[j], mla_qn_nope[j], mla_qn_rope[j], mla_kn_nope[j], mla_kn_rope[j],
                           need_ctx=not last)
            wo = mla_wo
        res_row0 = 0
        if last:
            if o.shape[0] != B * S:
                o = o[lay.n_ctx:]
            res_row0 = lay.n_ctx
            lay = _Layout(B, C, S, with_ctx=False)
        h = _mm(o, wo, layer=j, res=h, res_row0=res_row0, gate=mod[:, 2], lay=lay, name="mix_out")
        u2 = _norm_mod(h, norm_g[i, 1], mod, lay, shift_row=3, scale_row=4, out_dtype=BF16)
        hid = _mm(u2, mlp_w1, layer=i, act="relu2", out_dtype=BF16, lay=lay, name="mlp_up")
        h = _mm(hid, mlp_w2_bf16, layer=i, res=h, gate=mod[:, 5], lay=lay, name="mlp_down")
    return h.reshape(B, S, D)
```

```python
import functools
import math

import jax
import jax.numpy as jnp
from jax import lax
from jax.experimental import pallas as pl
from jax.experimental.pallas import tpu as pltpu

F32 = jnp.float32
BF16 = jnp.bfloat16

NORM_EPS = 1e-6
N_MOD = 6
GRID_W = 64
RW_HEAD = 64
RW_GN_EPS = 64e-5
RW_L2_EPS = 1e-12
MLA_NOPE = 128
MLA_ROPE = 64
MLA_V = 128
ROPE_THETA = 10000.0

LANES = 128
SUBLANES = 8
WKV_CHUNK = 64
WKV_STAGGER = 3
VMEM_LIMIT_BYTES = 56 * 1024 * 1024

_NT = (((1,), (1,)), ((), ()))


def _pick(n, prefs):
    for p in prefs:
        if n % p == 0:
            return p
    return n


def _params(sem):
    return pltpu.CompilerParams(dimension_semantics=sem, vmem_limit_bytes=VMEM_LIMIT_BYTES)


class _Layout:
    def __init__(self, B, C, S, with_ctx=True):
        self.B, self.C, self.S = B, C, S
        self.n_ctx = B * C if with_ctx else 0
        self.M = self.n_ctx + B * S

    def row_tile(self):
        return _pick(math.gcd(self.n_ctx, self.S) if self.n_ctx else self.S, (1024, 512, 256, 128, 64, 32, 16, 8))

    def seq_tile(self):
        return _pick(math.gcd(self.C, self.S) if self.n_ctx else self.S, (256, 128, 64, 32, 16, 8))

    def mod_row(self, i, tm):
        n_ctx_tiles = self.n_ctx // tm
        per_b = self.S // tm
        lat = 1 + (i - n_ctx_tiles) // per_b
        if n_ctx_tiles == 0:
            return lat
        return jnp.where(i < n_ctx_tiles, 0, lat)


def _mm_body(*refs, nk, act, has_gate, has_blend, cache_w):
    if has_gate:
        x_ref, w_ref, res_ref, gate_ref, o_ref, *scratch = refs
    elif has_blend:
        x_ref, w_ref, cur_ref, first_ref, bias_ref, o_ref, *scratch = refs
    else:
        x_ref, w_ref, o_ref, *scratch = refs

    def finish(acc):
        if act == "relu2":
            acc = jnp.square(jnp.maximum(acc, 0.0))
        elif act == "sigmoid":
            acc = jax.nn.sigmoid(acc)
        elif act == "tanh":
            acc = jnp.tanh(acc)
        if has_gate:
            acc = res_ref[...] + gate_ref[0] * acc
        if has_blend:
            cur = cur_ref[...]
            acc = cur + (first_ref[...] - cur) * jax.nn.sigmoid(bias_ref[...] + acc)
        o_ref[...] = acc.astype(o_ref.dtype)

    if cache_w:
        wc_ref = scratch[-1]

        @pl.when(pl.program_id(1) == 0)
        def _():
            wc_ref[...] = w_ref[...].astype(BF16)

        w = wc_ref[...]
    else:
        w = w_ref[...].astype(BF16)
    part = jnp.dot(x_ref[...].astype(BF16), w, preferred_element_type=F32)
    if nk == 1:
        finish(part)
    else:
        acc_ref = scratch[0]
        k = pl.program_id(2)

        @pl.when(k == 0)
        def _():
            acc_ref[...] = part

        @pl.when(k > 0)
        def _():
            acc_ref[...] += part

        @pl.when(k == nk - 1)
        def _():
            finish(acc_ref[...])


def _mm(x, w, *, layer=None, act=None, out_dtype=F32, res=None, res_row0=0, gate=None, blend=None, lay=None,
        tm=None, x_kblock=0, name="mm"):
    M = x.shape[0]
    K, N = w.shape[-2:]
    if tm is None:
        tm = lay.row_tile() if lay is not None else _pick(M, (1024, 512, 256, 128, 64, 32, 16, 8))
    tn = _pick(N, (1024, 512, 256, 128))
    tk = K if K <= 2048 else _pick(K, (2048, 1024, 512))
    nk = K // tk
    has_gate = gate is not None
    has_blend = blend is not None
    cache_w = w.dtype == F32 and nk == 1 and M // tm > 1
    if w.ndim == 3:
        w_spec = pl.BlockSpec((None, tk, tn), lambda j, i, k: (layer, k, j))
    else:
        w_spec = pl.BlockSpec((tk, tn), lambda j, i, k: (k, j))
    in_specs = [pl.BlockSpec((tm, tk), lambda j, i, k: (i, k + x_kblock * nk)), w_spec]
    args = [x, w]
    if has_gate:
        in_specs += [pl.BlockSpec((tm, tn), lambda j, i, k: (i + res_row0 // tm, j)),
                     pl.BlockSpec((1, 1, tn), lambda j, i, k: (lay.mod_row(i, tm), 0, j))]
        args += [res, gate.reshape(gate.shape[0], 1, N)]
    if has_blend:
        in_specs += [pl.BlockSpec((tm, tn), lambda j, i, k: (i, j)), pl.BlockSpec((tm, tn), lambda j, i, k: (i, j)),
                     pl.BlockSpec((1, tn), lambda j, i, k: (0, j))]
        args += [blend[0], blend[1], blend[2].reshape(1, N)]
    scratch = [pltpu.VMEM((tm, tn), F32)] if nk > 1 else []
    if cache_w:
        scratch.append(pltpu.VMEM((tk, tn), BF16))
    return pl.pallas_call(
        functools.partial(_mm_body, nk=nk, act=act, has_gate=has_gate, has_blend=has_blend, cache_w=cache_w),
        grid=(N // tn, M // tm, nk),
        in_specs=in_specs,
        out_specs=pl.BlockSpec((tm, tn), lambda j, i, k: (i, j)),
        out_shape=jax.ShapeDtypeStruct((M, N), out_dtype),
        scratch_shapes=scratch,
        compiler_params=_params(("parallel", "arbitrary", "arbitrary")),
        name=name,
    )(*args)


def _norm_mod_rows(x, gain, mod, shift_row, scale_row):
    ms = jnp.mean(x * x, axis=-1, keepdims=True)
    y = x * lax.rsqrt(ms + NORM_EPS) * gain
    return y * (1.0 + mod[scale_row:scale_row + 1]) + mod[shift_row:shift_row + 1]


def _norm_mod_body(x_ref, g_ref, mod_ref, o_ref, *, shift_row, scale_row):
    o_ref[...] = _norm_mod_rows(x_ref[...], g_ref[...], mod_ref[0], shift_row, scale_row).astype(o_ref.dtype)


def _norm_mod(x, g, mod, lay, *, shift_row, scale_row, out_dtype):
    M, D = x.shape
    tm = min(lay.row_tile(), 512)
    return pl.pallas_call(
        functools.partial(_norm_mod_body, shift_row=shift_row, scale_row=scale_row),
        grid=(M // tm,),
        in_specs=[pl.BlockSpec((tm, D), lambda i: (i, 0)),
                  pl.BlockSpec((1, D), lambda i: (0, 0)),
                  pl.BlockSpec((1, N_MOD, D), lambda i: (lay.mod_row(i, tm), 0, 0))],
        out_specs=pl.BlockSpec((tm, D), lambda i: (i, 0)),
        out_shape=jax.ShapeDtypeStruct((M, D), out_dtype),
        compiler_params=_params(("parallel",)),
        name="norm_mod",
    )(x, g.reshape(1, D), mod)


MIX_R, MIX_W, MIX_K, MIX_V, MIX_A, MIX_G = range(6)


def _rw_pre_body(*refs, tm, n_ctx_tiles, ctx_tiles, lat_tiles, lora_mix, lora_act):
    n_lora = len(lora_mix)
    h_ref, hp_ref, hn_ref, g_ref, mod_ref, mu_ref = refs[:6]
    lora_refs = refs[6:6 + n_lora]
    xr_ref, xk_ref, xv_ref = refs[6 + n_lora:9 + n_lora]
    mid_refs = refs[9 + n_lora:]
    i = pl.program_id(0)
    D = h_ref.shape[1]
    slabs = [slice(c, c + LANES) for c in range(0, D, LANES)]
    above = slice(SUBLANES - 1, SUBLANES)
    below = slice(0, 1)

    def inv_rms(ref, rows):
        sq = None
        for cs in slabs:
            x = ref[rows, cs]
            sq = x * x if sq is None else sq + x * x
        return lax.rsqrt(jnp.sum(sq, axis=-1, keepdims=True) * (1.0 / D) + NORM_EPS)

    r_tile, r_above, r_below = inv_rms(h_ref, slice(None)), inv_rms(hp_ref, above), inv_rms(hn_ref, below)
    in_ctx = i < n_ctx_tiles
    pos = jnp.where(in_ctx, i % ctx_tiles, (i - n_ctx_tiles) % lat_tiles)
    last = jnp.where(in_ctx, ctx_tiles - 1, lat_tiles - 1)
    rows = lax.broadcasted_iota(jnp.int32, (tm, 1), 0)
    acc = [None] * n_lora
    for cs in slabs:
        gain, shift, scale1 = g_ref[:, cs], mod_ref[0, 0:1, cs], 1.0 + mod_ref[0, 1:2, cs]
        u = h_ref[:, cs] * r_tile * gain * scale1 + shift
        u_above = jnp.where(pos == 0, 0.0, hp_ref[above, cs] * r_above * gain * scale1 + shift)
        u_below = jnp.where(pos == last, 0.0, hn_ref[below, cs] * r_below * gain * scale1 + shift)
        prev = jnp.where(rows == 0, u_above, pltpu.roll(u, 1, axis=0))
        nxt = jnp.where(rows == tm - 1, u_below, pltpu.roll(u, tm - 1, axis=0))
        xx = 0.5 * (prev + nxt) - u

        def mix(j):
            return (u + xx * mu_ref[j:j + 1, cs]).astype(BF16)

        xv = mix(MIX_V)
        xr_ref[:, cs] = mix(MIX_R)
        xk_ref[:, cs] = mix(MIX_K)
        xv_ref[:, cs] = xv
        for n, j in enumerate(lora_mix):
            part = jnp.dot(xv if j == MIX_V else mix(j), lora_refs[n][cs, :], preferred_element_type=F32)
            acc[n] = part if acc[n] is None else acc[n] + part
    for n, act in enumerate(lora_act):
        val = acc[n]
        if act == "tanh":
            val = jnp.tanh(val)
        elif act == "sigmoid":
            val = jax.nn.sigmoid(val)
        mid_refs[n][...] = val.astype(mid_refs[n].dtype)


def _rw_pre(h, g, mod, mu, lora, lay):
    M, D = h.shape
    tm = lay.seq_tile()
    per8 = tm // SUBLANES
    last8 = M // SUBLANES - 1
    ranks = [w.shape[1] for _, w, _ in lora]
    row_spec = pl.BlockSpec((tm, D), lambda i: (i, 0))
    return pl.pallas_call(
        functools.partial(_rw_pre_body, tm=tm, n_ctx_tiles=lay.n_ctx // tm, ctx_tiles=max(lay.C // tm, 1),
                          lat_tiles=lay.S // tm, lora_mix=tuple(j for j, _, _ in lora),
                          lora_act=tuple(a for _, _, a in lora)),
        grid=(M // tm,),
        in_specs=[row_spec,
                  pl.BlockSpec((SUBLANES, D), lambda i: (jnp.maximum(i * per8 - 1, 0), 0)),
                  pl.BlockSpec((SUBLANES, D), lambda i: (jnp.minimum((i + 1) * per8, last8), 0)),
                  pl.BlockSpec((1, D), lambda i: (0, 0)),
                  pl.BlockSpec((1, N_MOD, D), lambda i: (lay.mod_row(i, tm), 0, 0)),
                  pl.BlockSpec(mu.shape, lambda i: (0, 0))]
                 + [pl.BlockSpec((D, n), lambda i: (0, 0)) for n in ranks],
        out_specs=[row_spec] * 3 + [pl.BlockSpec((tm, n), lambda i: (i, 0)) for n in ranks],
        out_shape=[jax.ShapeDtypeStruct((M, D), BF16)] * 3 + [jax.ShapeDtypeStruct((M, n), BF16) for n in ranks],
        compiler_params=_params(("parallel",)),
        name="rw_pre",
    )(h, h, h, g.reshape(1, D), mod, mu, *[w for _, w, _ in lora])


def _wkv_body(*refs, reverse, npairs, nsub, epilogue):
    if epilogue:
        (r_ref, k_ref, v_ref, wl_ref, al_ref, w0_ref, a0_ref, kkp_ref, ka_ref,
         alo_ref, a0o_ref, yo_ref, g_ref, rk_ref, lng_ref, lnb_ref, o_ref, s_ref) = refs
    else:
        r_ref, k_ref, v_ref, wl_ref, al_ref, w0_ref, a0_ref, kkp_ref, ka_ref, o_ref, s_ref = refs
    L = WKV_CHUNK
    H = RW_HEAD

    @pl.when(pl.program_id(2) == 0)
    def _():
        s_ref[...] = jnp.zeros_like(s_ref)

    def order(row, col):
        return (row <= col) if reverse else (row >= col)

    row = lax.broadcasted_iota(jnp.int32, (L, L), 0)
    col = lax.broadcasted_iota(jnp.int32, (L, L), 1)
    tri = jnp.where(order(row, col), 1.0, 0.0).astype(BF16)
    tri3 = jnp.concatenate([tri, tri, tri], axis=1)

    prow = lax.broadcasted_iota(jnp.int32, (L, 2 * L), 0)
    pcol = lax.broadcasted_iota(jnp.int32, (L, 2 * L), 1) & (L - 1)
    incl = order(prow, pcol)
    strict = incl & (prow != pcol)
    incl2 = jnp.concatenate([incl, incl], axis=1)
    eye = jnp.where(prow == pcol, 1.0, 0.0)

    def sibling(s):
        return ((prow // (2 * s)) == (pcol // (2 * s))) & ((prow // s) != (pcol // s))

    bmask = (lax.broadcasted_iota(jnp.int32, (2 * L, LANES), 0) // L
             == lax.broadcasted_iota(jnp.int32, (2 * L, LANES), 1) // H)
    head_ones = jnp.where(lax.broadcasted_iota(jnp.int32, (LANES, LANES), 0) // H
                          == lax.broadcasted_iota(jnp.int32, (LANES, LANES), 1) // H, 1.0, 0.0).astype(BF16)

    def bdf(x):
        return jnp.where(bmask, jnp.concatenate([x, x], axis=0), 0.0)

    def bd(x):
        return bdf(x).astype(BF16)

    def dot(a, b):
        return jnp.dot(a, b, preferred_element_type=F32)

    def cat0(*xs):
        return jnp.concatenate(xs, axis=0)

    def cat1(*xs):
        return jnp.concatenate(xs, axis=1)

    prs = range(npairs)
    state = [s_ref[p] for p in prs]

    def chunk_stages(ci):
        rows = slice(ci * L, (ci + 1) * L)

        def tiles(ref):
            return [ref[rows, p * LANES:(p + 1) * LANES] for p in prs]

        def vecs(ref):
            return [ref[:, p * LANES:(p + 1) * LANES] for p in prs]

        def head_sum(xs):
            tot = dot(cat0(*xs).astype(BF16), head_ones)
            return [tot[p * L:(p + 1) * L] for p in prs]

        def store(vals):
            for p, val in zip(prs, vals):
                o_ref[rows, p * LANES:(p + 1) * LANES] = val.astype(o_ref.dtype)

        r, k, v = tiles(r_ref), tiles(k_ref), tiles(v_ref)
        ka = vecs(ka_ref)

        a = [jax.nn.sigmoid(a0 + al) for a0, al in zip(vecs(a0_ref), tiles(al_ref))]
        z = [-(w0 + wl) for w0, wl in zip(vecs(w0_ref), tiles(wl_ref))]
        softplus = [jnp.maximum(x, 0.0) + jnp.log(1.0 + jnp.exp(-jnp.abs(x))) for x in z]
        lw = [-jnp.exp(-x - 0.5) for x in softplus]
        yield
        kraw = [k[p] * kkp for p, kkp in zip(prs, vecs(kkp_ref))]
        norm2 = head_sum([x * x for x in kraw])
        kd = [k[p] * (1.0 + (a[p] - 1.0) * ka[p]) for p in prs]
        hi = [x.astype(BF16) for x in lw]
        rem = [lw[p] - hi[p].astype(F32) for p in prs]
        mid = [x.astype(BF16) for x in rem]
        lo = [(rem[p] - mid[p].astype(F32)).astype(BF16) for p in prs]
        g = [dot(tri3, cat0(hi[p], mid[p], lo[p])) for p in prs]
        ee = [jnp.exp(jnp.sum(x, axis=0, keepdims=True)) for x in lw]
        yield
        kk = [kraw[p] / jnp.maximum(jnp.sqrt(norm2[p]), RW_L2_EPS) for p in prs]
        en = [jnp.exp(-x) for x in g]
        abar = [-(kk[p] * jnp.exp(g[p] - lw[p])) for p in prs]
        rbar = [r[p] * jnp.exp(g[p]) for p in prs]
        bt = [kk[p] * a[p] * en[p] for p in prs]
        kt = [kd[p] * en[p] for p in prs]
        yield
        amat = [lax.dot_general(cat0(abar[p], rbar[p]).astype(BF16), cat0(bd(bt[p]), bd(kt[p])), _NT,
                                preferred_element_type=F32) for p in prs]
        vbd = [bd(x) for x in v]
        gam = [jnp.broadcast_to(x, (LANES, LANES)).T for x in ee]
        upd_lhs = [cat1(bdf(bt[p] * ee[p]).T, bdf(kt[p] * ee[p]).T).astype(BF16) for p in prs]
        yield
        a_ab = [jnp.where(strict, x[:L, :2 * L], 0.0) for x in amat]
        a_ak = [jnp.where(strict, x[:L, 2 * L:], 0.0).astype(BF16) for x in amat]
        a_rbk = [jnp.where(incl2, x[L:], 0.0).astype(BF16) for x in amat]
        av = [dot(a_ak[p], vbd[p]) for p in prs]

        t = [eye + jnp.where(sibling(1), x, 0.0) for x in a_ab]
        s = 2
        while s < L:
            sib = sibling(s)
            a_l = [bd(jnp.where(sib, a_ab[p], 0.0)) for p in prs]
            if s < SUBLANES:
                half = [dot(t[p].astype(BF16), a_l[p]).astype(BF16) for p in prs]
                yield
                t = [t[p] + dot(half[p], bd(t[p])) for p in prs]
            else:
                blocks = range(L // s)
                moving = [b for b in blocks if (b % 2 == 1) != reverse]

                def pick(x, n):
                    return x[n * s:(n + 1) * s]

                half = [dot(cat0(*[pick(t[p], b) for b in moving]).astype(BF16), a_l[p]).astype(BF16) for p in prs]
                yield
                upd = [dot(half[p], bd(t[p])) for p in prs]
                t = [cat0(*[pick(t[p], b) + pick(upd[p], moving.index(b)) if b in moving else pick(t[p], b)
                            for b in blocks]) for p in prs]
            yield
            s *= 2

        wu = [dot(t[p].astype(BF16), cat1(bd(abar[p]), bd(av[p]))) for p in prs]
        yield
        wr_lhs = [cat0(wu[p][:, :LANES], rbar[p]).astype(BF16) for p in prs]
        ut = [x[:, LANES:] for x in wu]
        c2_lhs = [cat0(a_rbk[p], upd_lhs[p]) for p in prs]
        yield

        wr = [dot(wr_lhs[p], state[p].astype(BF16)) for p in prs]
        yield
        u = [wr[p][:L] + ut[p] for p in prs]
        out2 = [dot(c2_lhs[p], cat0(bd(u[p]), vbd[p])) for p in prs]
        yield
        y = [wr[p][L:] + out2[p][:L] for p in prs]
        for p in prs:
            state[p] = state[p] * gam[p] + out2[p][L:]
        if not epilogue:
            store(y)
            return
        yield

        inv_n = 1.0 / H
        ytot = [y[p] + yo for p, yo in zip(prs, tiles(yo_ref))]
        mean = head_sum(ytot)
        a_o = [jax.nn.sigmoid(a0 + al) for a0, al in zip(vecs(a0o_ref), tiles(alo_ref))]
        kd_sum = [kd[p] + k[p] * (1.0 + (a_o[p] - 1.0) * ka[p]) for p in prs]
        bonus = head_sum([r[p] * rk * kd_sum[p] for p, rk in zip(prs, vecs(rk_ref))])
        yield
        dev = [ytot[p] - mean[p] * inv_n for p in prs]
        var = head_sum([x * x for x in dev])
        yield
        lng, lnb, gate = vecs(lng_ref), vecs(lnb_ref), tiles(g_ref)
        store([(dev[p] * lax.rsqrt(var[p] * inv_n + RW_GN_EPS) * lng[p] + lnb[p] + bonus[p] * v[p]) * gate[p]
               for p in prs])

    waiting = [chunk_stages(ci) for ci in (reversed(range(nsub)) if reverse else range(nsub))]
    running = []
    step = 0
    while waiting or running:
        if waiting and step % WKV_STAGGER == 0:
            running.append(waiting.pop(0))
        for gen in list(running):
            if next(gen, "done") == "done":
                running.remove(gen)
        step += 1
    for p in prs:
        s_ref[p] = state[p]


def _wkv(r, k, v, wl, al, w0, a0, kkp, ka, lay, *, reverse, epilogue=None):
    M, D = r.shape
    L = WKV_CHUNK
    td = min(D, 8 * LANES)
    npairs = td // LANES
    nsub = _pick(math.gcd(lay.C, lay.S) // L, (4, 2, 1))
    tb = nsub * L
    ctx_blk = lay.C // tb
    lat_blk = lay.S // tb
    ctx_total = lay.n_ctx // tb

    def tok_block(b, c):
        if reverse:
            ctx_c = ctx_blk - 1 - c
            lat_c = lat_blk - 1 - (c - ctx_blk)
        else:
            ctx_c = c
            lat_c = c - ctx_blk
        return jnp.where(c < ctx_blk, b * ctx_blk + ctx_c, ctx_total + b * lat_blk + lat_c)

    mat = pl.BlockSpec((tb, td), lambda b, d, c: (tok_block(b, c), d))
    vec = pl.BlockSpec((1, td), lambda b, d, c: (0, d))
    row = lambda x: x.reshape(1, D)
    args = [r, k, v, wl, al, row(w0), row(a0), row(kkp), row(ka)]
    specs = [mat] * 5 + [vec] * 4
    if epilogue is not None:
        al_o, a0_o, y_o, gate, rk, lnx_g, lnx_b = epilogue
        args += [al_o, row(a0_o), y_o, gate, row(rk), row(lnx_g), row(lnx_b)]
        specs += [mat, vec, mat, mat, vec, vec, vec]
    return pl.pallas_call(
        functools.partial(_wkv_body, reverse=reverse, npairs=npairs, nsub=nsub, epilogue=epilogue is not None),
        grid=(lay.B, D // td, ctx_blk + lat_blk),
        in_specs=specs,
        out_specs=mat,
        out_shape=jax.ShapeDtypeStruct((M, D), F32 if epilogue is None else BF16),
        scratch_shapes=[pltpu.VMEM((npairs, LANES, LANES), F32)],
        compiler_params=_params(("parallel", "parallel", "arbitrary")),
        name="wkv_bwd" if reverse else "wkv_fwd",
    )(*args)


ATTN_KEY_CHUNK = 512


def _attn_body(q_ref, *refs, n_kv):
    k_refs, v_refs = refs[:n_kv], refs[n_kv:2 * n_kv]
    o_ref = refs[2 * n_kv]
    q = q_ref[...]
    pieces = []
    for k_ref, v_ref in zip(k_refs, v_refs):
        n = k_ref.shape[0]
        for lo in range(0, n, ATTN_KEY_CHUNK):
            pieces.append((k_ref, v_ref, lo, min(lo + ATTN_KEY_CHUNK, n)))

    def scores(piece):
        k_ref, _, lo, hi = piece
        return lax.dot_general(q, k_ref[lo:hi, :], _NT, preferred_element_type=F32)

    def lane_blocks(x):
        return [x[:, c:c + LANES] for c in range(0, x.shape[1], LANES)]

    s_next = scores(pieces[0])
    m = l = acc = None
    for c, (_, v_ref, lo, hi) in enumerate(pieces):
        s = s_next
        if c + 1 < len(pieces):
            s_next = scores(pieces[c + 1])
        m_c = jnp.max(functools.reduce(jnp.maximum, lane_blocks(s)), axis=-1, keepdims=True)
        m_new = m_c if m is None else jnp.maximum(m, m_c)
        p = jnp.exp2(s - m_new)
        l_c = functools.reduce(jnp.add, lane_blocks(p))
        pv = jnp.dot(p.astype(BF16), v_ref[lo:hi, :], preferred_element_type=F32)
        if m is None:
            l, acc = l_c, pv
        else:
            alpha = jnp.exp2(m - m_new)
            l, acc = alpha * l + l_c, alpha * acc + pv
        m = m_new
    o_ref[...] = (acc / jnp.sum(l, axis=-1, keepdims=True)).astype(o_ref.dtype)


def _attn(q, k, v, heads, lay, *, q_row0, n_q_rows, with_lat_keys):
    dk = k.shape[1] // heads
    dv = v.shape[1] // heads
    tq = _pick(n_q_rows, (512, 256, 128, 64, 32, 16, 8))
    per_b = n_q_rows // tq
    q0 = q_row0 // tq
    kv = [(lay.C, 0)]
    if with_lat_keys:
        kv.append((lay.S, lay.n_ctx // lay.S))
        k_lat, v_lat = k, v
        if lay.n_ctx % lay.S:
            k_lat, v_lat, kv[1] = k[lay.n_ctx:], v[lay.n_ctx:], (lay.S, 0)
    in_specs = [pl.BlockSpec((tq, dk), lambda b, h, i: (q0 + b * per_b + i, h))]
    in_specs += [pl.BlockSpec((n, dk), functools.partial(lambda b, h, i, base: (base + b, h), base=base))
                 for n, base in kv]
    in_specs += [pl.BlockSpec((n, dv), functools.partial(lambda b, h, i, base: (base + b, h), base=base))
                 for n, base in kv]
    operands = [q, k] + ([k_lat] if with_lat_keys else []) + [v] + ([v_lat] if with_lat_keys else [])
    return pl.pallas_call(
        functools.partial(_attn_body, n_kv=len(kv)),
        grid=(lay.B, heads, per_b),
        in_specs=in_specs,
        out_specs=pl.BlockSpec((tq, dv), lambda b, h, i: (b * per_b + i, h)),
        out_shape=jax.ShapeDtypeStruct((lay.B * n_q_rows, heads * dv), BF16),
        compiler_params=_params(("parallel", "parallel", "arbitrary")),
        name="attn",
    )(*operands)


def _rms_rows(x, gain):
    return x * lax.rsqrt(jnp.mean(x * x, axis=-1, keepdims=True) + NORM_EPS) * gain


def _rope_lanes(x, cos, sin):
    lane = lax.broadcasted_iota(jnp.int32, x.shape, 1)
    first = (lane % MLA_ROPE) < MLA_ROPE // 2
    partner = jnp.where(first, pltpu.roll(x, LANES - MLA_ROPE // 2, axis=1), pltpu.roll(x, MLA_ROPE // 2, axis=1))
    return x * cos + partner * sin


def _mla_down_body(h_ref, g_ref, mod_ref, wkv_ref, wq_ref, gkv_ref, gr_ref, gq_ref, cos_ref, sin_ref,
                   ckv_ref, kr_ref, cq_ref, *, kv_lora):
    u = _norm_mod_rows(h_ref[...], g_ref[...], mod_ref[0], 0, 1).astype(BF16)
    acc = jnp.dot(u, wkv_ref[...], preferred_element_type=F32)
    ckv_ref[...] = _rms_rows(acc[:, :kv_lora], gkv_ref[...]).astype(ckv_ref.dtype)
    kr = acc[:, kv_lora:]
    ms = jnp.sum(kr * kr, axis=-1, keepdims=True) * (1.0 / MLA_ROPE)
    kr = kr * lax.rsqrt(ms + NORM_EPS) * gr_ref[...]
    kr_ref[...] = _rope_lanes(kr, cos_ref[...], sin_ref[...]).astype(kr_ref.dtype)
    acc_q = jnp.dot(u, wq_ref[...], preferred_element_type=F32)
    cq_ref[...] = _rms_rows(acc_q, gq_ref[...]).astype(cq_ref.dtype)


def _rms_groups(x, gain, group):
    ones = jnp.where(lax.broadcasted_iota(jnp.int32, (LANES, LANES), 0) // group
                     == lax.broadcasted_iota(jnp.int32, (LANES, LANES), 1) // group, 1.0, 0.0).astype(BF16)
    ss = jnp.dot((x * x).astype(BF16), ones, preferred_element_type=F32)
    return x * lax.rsqrt(ss * (1.0 / group) + NORM_EPS) * gain


def _mla_ukv_body(c_ref, w_ref, g_ref, kr_ref, kcat_ref, v_ref, *, heads_per_tile):
    acc = jnp.dot(c_ref[...], w_ref[...], preferred_element_type=F32)
    kr = kr_ref[...]
    dk = MLA_NOPE + LANES
    for h in range(heads_per_tile):
        base = h * (MLA_NOPE + MLA_V)
        kcat_ref[:, h * dk:h * dk + MLA_NOPE] = _rms_groups(acc[:, base:base + MLA_NOPE], g_ref[...],
                                                            MLA_NOPE).astype(kcat_ref.dtype)
        kcat_ref[:, h * dk + MLA_NOPE:(h + 1) * dk] = kr
        v_ref[:, h * MLA_V:(h + 1) * MLA_V] = acc[:, base + MLA_NOPE:base + MLA_NOPE + MLA_V].astype(v_ref.dtype)


def _mla_uq_body(c_ref, wn_ref, wr_ref, gn_ref, gr_ref, cos_ref, sin_ref, q_ref, *, heads_per_tile, qscale):
    x = c_ref[...]
    nope = jnp.dot(x, wn_ref[...], preferred_element_type=F32)
    rope = jnp.dot(x, wr_ref[...], preferred_element_type=F32)
    low = lax.broadcasted_iota(jnp.int32, (x.shape[0], LANES), 1) < MLA_ROPE
    dk = MLA_NOPE + LANES
    for b in range(heads_per_tile // 2):
        xr = _rms_groups(rope[:, b * LANES:(b + 1) * LANES], gr_ref[...], MLA_ROPE)
        xr = _rope_lanes(xr, cos_ref[...], sin_ref[...]) * qscale
        for t in range(2):
            h = 2 * b + t
            qn = _rms_groups(nope[:, h * MLA_NOPE:(h + 1) * MLA_NOPE], gn_ref[...], MLA_NOPE) * qscale
            q_ref[:, h * dk:h * dk + MLA_NOPE] = qn.astype(q_ref.dtype)
            rr = xr if t == 0 else pltpu.roll(xr, MLA_ROPE, axis=1)
            q_ref[:, h * dk + MLA_NOPE:(h + 1) * dk] = jnp.where(low, rr, 0.0).astype(q_ref.dtype)


def _pad_cols(w, n):
    return jnp.pad(w, ((0, 0), (0, n - w.shape[1])))


def _pad_rows(w, n):
    return jnp.pad(w, ((0, n - w.shape[0]), (0, 0)))


def _up128(n):
    return -(-n // LANES) * LANES


def _rope_tables(lay):
    n = lay.S
    rows = n // GRID_W
    row = jnp.broadcast_to(jnp.arange(rows)[:, None], (rows, GRID_W)).reshape(-1)
    col = jnp.broadcast_to(jnp.arange(GRID_W)[None, :], (rows, GRID_W)).reshape(-1)
    n_freq = MLA_ROPE // 4
    inv = ROPE_THETA ** (-jnp.arange(n_freq, dtype=F32) / n_freq)
    ang = jnp.concatenate([row[:, None].astype(F32) * inv, col[:, None].astype(F32) * inv], axis=-1)
    cos, sin = jnp.cos(ang), jnp.sin(ang)
    reps = LANES // MLA_ROPE
    cos_l = jnp.tile(jnp.concatenate([cos, cos], axis=1), (lay.B, reps))
    sin_l = jnp.tile(jnp.concatenate([-sin, sin], axis=1), (lay.B, reps))
    cos_t = jnp.concatenate([jnp.ones((lay.n_ctx, LANES), F32), cos_l], axis=0)
    sin_t = jnp.concatenate([jnp.zeros((lay.n_ctx, LANES), F32), sin_l], axis=0)
    return cos_t, sin_t


def _rwkv_mixer(h, norm_g, mod, lay, v_first, j, big_w, pw, vres, rk, lnx_g, lnx_b):
    wr, wk, wv = big_w
    mu, w0, w1, w2, a0, a1, a2, g1, g2, kkp, ka = pw
    nw, na = _up128(w1.shape[2]), _up128(a1.shape[2])
    lora = [(MIX_W, jnp.concatenate([_pad_cols(w1[e], nw) for e in range(2)], axis=1).astype(BF16), "tanh"),
            (MIX_A, jnp.concatenate([_pad_cols(a1[e], na) for e in range(2)], axis=1).astype(BF16), None),
            (MIX_G, g1.astype(BF16), "sigmoid")]
    if vres is not None:
        v0, v1, v2 = vres
        nv = _up128(v1.shape[1])
        lora.append((MIX_V, _pad_cols(v1, nv).astype(BF16), None))
    xr, xk, xv, w_mid, a_mid, g_mid, *v_mid = _rw_pre(h, norm_g, mod, mu, lora, lay)
    mm = functools.partial(_mm, lay=lay)
    r = mm(xr, wr, layer=j, name="rw_r")
    k = mm(xk, wk, layer=j, name="rw_k")
    v = mm(xv, wv, layer=j, name="rw_v")
    if vres is not None:
        v = mm(v_mid[0], _pad_rows(v2, nv).astype(BF16), blend=(v, v_first, v0), name="rw_v2")
    g = mm(g_mid, g2.astype(BF16), out_dtype=BF16, name="rw_g2")
    wl = [mm(w_mid, _pad_rows(w2[e], nw).astype(BF16), x_kblock=e, out_dtype=BF16, name="rw_w2") for e in range(2)]
    al = [mm(a_mid, _pad_rows(a2[e], na).astype(BF16), x_kblock=e, out_dtype=BF16, name="rw_a2") for e in range(2)]

    y_fwd = _wkv(r, k, v, wl[0], al[0], w0[0], a0[0], kkp, ka, lay, reverse=False)
    o = _wkv(r, k, v, wl[1], al[1], w0[1], a0[1], kkp, ka, lay, reverse=True,
             epilogue=(al[0], a0[0], y_fwd, g, rk.reshape(-1), lnx_g, lnx_b))
    return o, v


def _mla_mixer(h, norm_g, mod, lay, rope_tabs, wdq, qnorm, wuq, wdkv, kvnorm, wukv, qn_nope, qn_rope, kn_nope,
               kn_rope, need_ctx):
    M, D = h.shape
    heads = D // MLA_V
    kv_lora, q_lora = kvnorm.shape[0], qnorm.shape[0]
    cos_t, sin_t = rope_tabs
    tm = lay.row_tile()
    hpt = _pick(heads, (4, 2))
    dk = MLA_NOPE + LANES
    pad = LANES - MLA_ROPE
    perm = jnp.concatenate([jnp.arange(0, MLA_ROPE, 2), jnp.arange(1, MLA_ROPE, 2)])
    row = lambda t: t.reshape(1, -1)
    par1 = _params(("parallel",))
    par2 = _params(("parallel", "parallel"))

    w_dkv = jnp.concatenate([wdkv[:, :kv_lora], wdkv[:, kv_lora:][:, perm], jnp.zeros((D, pad), F32)], axis=1)
    tmd = min(tm, 512)
    c_kv, k_rope, c_q = pl.pallas_call(
        functools.partial(_mla_down_body, kv_lora=kv_lora),
        grid=(M // tmd,),
        in_specs=[pl.BlockSpec((tmd, D), lambda i: (i, 0)),
                  pl.BlockSpec((1, D), lambda i: (0, 0)),
                  pl.BlockSpec((1, N_MOD, D), lambda i: (lay.mod_row(i, tmd), 0, 0)),
                  pl.BlockSpec((D, kv_lora + LANES), lambda i: (0, 0)),
                  pl.BlockSpec((D, q_lora), lambda i: (0, 0)),
                  pl.BlockSpec((1, kv_lora), lambda i: (0, 0)),
                  pl.BlockSpec((1, LANES), lambda i: (0, 0)),
                  pl.BlockSpec((1, q_lora), lambda i: (0, 0)),
                  pl.BlockSpec((tmd, LANES), lambda i: (i, 0)),
                  pl.BlockSpec((tmd, LANES), lambda i: (i, 0))],
        out_specs=[pl.BlockSpec((tmd, kv_lora), lambda i: (i, 0)), pl.BlockSpec((tmd, LANES), lambda i: (i, 0)),
                   pl.BlockSpec((tmd, q_lora), lambda i: (i, 0))],
        out_shape=[jax.ShapeDtypeStruct((M, kv_lora), BF16), jax.ShapeDtypeStruct((M, LANES), BF16),
                   jax.ShapeDtypeStruct((M, q_lora), BF16)],
        compiler_params=par1, name="mla_down",
    )(h, row(norm_g), mod, w_dkv.astype(BF16), wdq.astype(BF16), row(kvnorm),
      row(jnp.pad(kn_rope[perm], (0, pad))), row(qnorm), cos_t, sin_t)

    k_cat, v = pl.pallas_call(
        functools.partial(_mla_ukv_body, heads_per_tile=hpt),
        grid=(heads // hpt, M // tm),
        in_specs=[pl.BlockSpec((tm, kv_lora), lambda j, i: (i, 0)),
                  pl.BlockSpec((kv_lora, hpt * (MLA_NOPE + MLA_V)), lambda j, i: (0, j)),
                  pl.BlockSpec((1, MLA_NOPE), lambda j, i: (0, 0)),
                  pl.BlockSpec((tm, LANES), lambda j, i: (i, 0))],
        out_specs=[pl.BlockSpec((tm, hpt * dk), lambda j, i: (i, j)),
                   pl.BlockSpec((tm, hpt * MLA_V), lambda j, i: (i, j))],
        out_shape=[jax.ShapeDtypeStruct((M, heads * dk), BF16), jax.ShapeDtypeStruct((M, heads * MLA_V), BF16)],
        compiler_params=par2, name="mla_ukv",
    )(c_kv, wukv.astype(BF16), row(kn_nope), k_rope)

    q_rows = M if need_ctx else lay.B * lay.S
    r0 = (M - q_rows) // tm
    w3 = wuq.reshape(q_lora, heads, MLA_NOPE + MLA_ROPE)
    w_nope = w3[:, :, :MLA_NOPE].reshape(q_lora, heads * MLA_NOPE)
    w_rope = w3[:, :, MLA_NOPE:][:, :, perm].reshape(q_lora, heads * MLA_ROPE)
    qscale = math.log2(math.e) / math.sqrt(MLA_NOPE + MLA_ROPE)
    q_cat = pl.pallas_call(
        functools.partial(_mla_uq_body, heads_per_tile=hpt, qscale=qscale),
        grid=(heads // hpt, q_rows // tm),
        in_specs=[pl.BlockSpec((tm, q_lora), lambda j, i: (i + r0, 0)),
                  pl.BlockSpec((q_lora, hpt * MLA_NOPE), lambda j, i: (0, j)),
                  pl.BlockSpec((q_lora, hpt * MLA_ROPE), lambda j, i: (0, j)),
                  pl.BlockSpec((1, MLA_NOPE), lambda j, i: (0, 0)),
                  pl.BlockSpec((1, LANES), lambda j, i: (0, 0)),
                  pl.BlockSpec((tm, LANES), lambda j, i: (i + r0, 0)),
                  pl.BlockSpec((tm, LANES), lambda j, i: (i + r0, 0))],
        out_specs=pl.BlockSpec((tm, hpt * dk), lambda j, i: (i, j)),
        out_shape=jax.ShapeDtypeStruct((q_rows, heads * dk), BF16),
        compiler_params=par2, name="mla_uq",
    )(c_q, w_nope.astype(BF16), w_rope.astype(BF16), row(qn_nope), row(jnp.tile(qn_rope[perm], LANES // MLA_ROPE)),
      cos_t, sin_t)

    o_lat = _attn(q_cat, k_cat, v, heads, lay, q_row0=q_rows - lay.B * lay.S, n_q_rows=lay.S, with_lat_keys=True)
    if not need_ctx:
        return o_lat
    o_ctx = _attn(q_cat, k_cat, v, heads, lay, q_row0=0, n_q_rows=lay.C, with_lat_keys=False)
    return jnp.concatenate([o_ctx, o_lat], axis=0)


def kernel(x, c, ctx, c_ctx, mod_w, mod_b, norm_g, mlp_w1, mlp_w2, rw_mu, rw_wr, rw_wk, rw_wv, rw_wo, rw_w0, rw_w1, rw_w2, rw_a0, rw_a1, rw_a2, rw_g1, rw_g2, rw_kk, rw_ka, rw_rk, rw_lnx_g, rw_lnx_b, rw_v0, rw_v1, rw_v2, mla_wdq, mla_qnorm, mla_wuq, mla_wdkv, mla_kvnorm, mla_wukv, mla_qn_nope, mla_qn_rope, mla_kn_nope, mla_kn_rope, mla_wo):
    B, S, D = x.shape
    C = ctx.shape[1]
    depth = mod_w.shape[0]
    lay = _Layout(B, C, S)
    rope = _rope_tables(lay)

    sc_all = jnp.concatenate([jax.nn.silu(c_ctx)[None], jax.nn.silu(c)], axis=0)
    h = jnp.concatenate([ctx.reshape(B * C, D), x.reshape(B * S, D)], axis=0)
    v_first = None
    mlp_w2_bf16 = mlp_w2.astype(BF16)

    for i in range(depth):
        last = i == depth - 1
        j = i // 2
        mod = (_mm(sc_all, mod_w, layer=i, name="adaln") + mod_b[i]).reshape(B + 1, N_MOD, D)
        if i % 2 == 0:
            pw = (rw_mu[j], rw_w0[j], rw_w1[j], rw_w2[j],
                  rw_a0[j], rw_a1[j], rw_a2[j], rw_g1[j], rw_g2[j], rw_kk[j], rw_ka[j])
            vres = None if j == 0 else (rw_v0[j - 1], rw_v1[j - 1], rw_v2[j - 1])
            o, v_cur = _rwkv_mixer(h, norm_g[i, 0], mod, lay, v_first, j, (rw_wr, rw_wk, rw_wv), pw, vres,
                                   rw_rk[j], rw_lnx_g[j], rw_lnx_b[j])
            if j == 0:
                v_first = v_cur
            wo = rw_wo
        else:
            o = _mla_mixer(h, norm_g[i, 0], mod, lay, rope, mla_wdq[j], mla_qnorm[j], mla_wuq[j], mla_wdkv[j], mla_kvnorm[j],
                           mla_wukv[j], mla_qn_nope[j], mla_qn_rope[j], mla_kn_nope[j], mla_kn_rope[j],
                           need_ctx=not last)
            wo = mla_wo
        res_row0 = 0
        if last:
            if o.shape[0] != B * S:
                o = o[lay.n_ctx:]
            res_row0 = lay.n_ctx
            lay = _Layout(B, C, S, with_ctx=False)
        h = _mm(o, wo, layer=j, res=h, res_row0=res_row0, gate=mod[:, 2], lay=lay, name="mix_out")
        u2 = _norm_mod(h, norm_g[i, 1], mod, lay, shift_row=3, scale_row=4, out_dtype=BF16)
        hid = _mm(u2, mlp_w1, layer=i, act="relu2", out_dtype=BF16, lay=lay, name="mlp_up")
        h = _mm(hid, mlp_w2_bf16, layer=i, res=h, gate=mod[:, 5], lay=lay, name="mlp_down")
    return h.reshape(B, S, D)
```

```python
import functools
import math

import jax
import jax.numpy as jnp
from jax import lax
from jax.experimental import pallas as pl
from jax.experimental.pallas import tpu as pltpu

F32 = jnp.float32
BF16 = jnp.bfloat16

NORM_EPS = 1e-6
N_MOD = 6
GRID_W = 64
RW_HEAD = 64
RW_GN_EPS = 64e-5
RW_L2_EPS = 1e-12
MLA_NOPE = 128
MLA_ROPE = 64
MLA_V = 128
ROPE_THETA = 10000.0

LANES = 128
SUBLANES = 8
WKV_CHUNK = 64
WKV_STAGGER = 3
VMEM_LIMIT_BYTES = 56 * 1024 * 1024

_NT = (((1,), (1,)), ((), ()))


def _pick(n, prefs):
    for p in prefs:
        if n % p == 0:
            return p
    return n


def _params(sem):
    return pltpu.CompilerParams(dimension_semantics=sem, vmem_limit_bytes=VMEM_LIMIT_BYTES)


class _Layout:
    def __init__(self, B, C, S, with_ctx=True):
        self.B, self.C, self.S = B, C, S
        self.n_ctx = B * C if with_ctx else 0
        self.M = self.n_ctx + B * S

    def row_tile(self):
        return _pick(math.gcd(self.n_ctx, self.S) if self.n_ctx else self.S, (1024, 512, 256, 128, 64, 32, 16, 8))

    def seq_tile(self):
        return _pick(math.gcd(self.C, self.S) if self.n_ctx else self.S, (256, 128, 64, 32, 16, 8))

    def mod_row(self, i, tm):
        n_ctx_tiles = self.n_ctx // tm
        per_b = self.S // tm
        lat = 1 + (i - n_ctx_tiles) // per_b
        if n_ctx_tiles == 0:
            return lat
        return jnp.where(i < n_ctx_tiles, 0, lat)


def _mm_body(*refs, nk, act, has_gate, has_blend, cache_w):
    if has_gate:
        x_ref, w_ref, res_ref, gate_ref, o_ref, *scratch = refs
    elif has_blend:
        x_ref, w_ref, cur_ref, first_ref, bias_ref, o_ref, *scratch = refs
    else:
        x_ref, w_ref, o_ref, *scratch = refs

    def finish(acc):
        if act == "relu2":
            acc = jnp.square(jnp.maximum(acc, 0.0))
        elif act == "sigmoid":
            acc = jax.nn.sigmoid(acc)
        elif act == "tanh":
            acc = jnp.tanh(acc)
        if has_gate:
            acc = res_ref[...] + gate_ref[0] * acc
        if has_blend:
            cur = cur_ref[...]
            acc = cur + (first_ref[...] - cur) * jax.nn.sigmoid(bias_ref[...] + acc)
        o_ref[...] = acc.astype(o_ref.dtype)

    if cache_w:
        wc_ref = scratch[-1]

        @pl.when(pl.program_id(1) == 0)
        def _():
            wc_ref[...] = w_ref[...].astype(BF16)

        w = wc_ref[...]
    else:
        w = w_ref[...].astype(BF16)
    part = jnp.dot(x_ref[...].astype(BF16), w, preferred_element_type=F32)
    if nk == 1:
        finish(part)
    else:
        acc_ref = scratch[0]
        k = pl.program_id(2)

        @pl.when(k == 0)
        def _():
            acc_ref[...] = part

        @pl.when(k > 0)
        def _():
            acc_ref[...] += part

        @pl.when(k == nk - 1)
        def _():
            finish(acc_ref[...])


def _mm(x, w, *, layer=None, act=None, out_dtype=F32, res=None, gate=None, blend=None, lay=None,
        tm=None, x_kblock=0, name="mm"):
    M = x.shape[0]
    K, N = w.shape[-2:]
    if tm is None:
        tm = lay.row_tile() if lay is not None else _pick(M, (1024, 512, 256, 128, 64, 32, 16, 8))
    tn = _pick(N, (1024, 512, 256, 128))
    tk = K if K <= 2048 else _pick(K, (2048, 1024, 512))
    nk = K // tk
    has_gate = gate is not None
    has_blend = blend is not None
    cache_w = w.dtype == F32 and nk == 1 and M // tm > 1
    if w.ndim == 3:
        w_spec = pl.BlockSpec((None, tk, tn), lambda j, i, k: (layer, k, j))
    else:
        w_spec = pl.BlockSpec((tk, tn), lambda j, i, k: (k, j))
    in_specs = [pl.BlockSpec((tm, tk), lambda j, i, k: (i, k + x_kblock * nk)), w_spec]
    args = [x, w]
    if has_gate:
        in_specs += [pl.BlockSpec((tm, tn), lambda j, i, k: (i, j)),
                     pl.BlockSpec((1, 1, tn), lambda j, i, k: (lay.mod_row(i, tm), 0, j))]
        args += [res, gate.reshape(gate.shape[0], 1, N)]
    if has_blend:
        in_specs += [pl.BlockSpec((tm, tn), lambda j, i, k: (i, j)), pl.BlockSpec((tm, tn), lambda j, i, k: (i, j)),
                     pl.BlockSpec((1, tn), lambda j, i, k: (0, j))]
        args += [blend[0], blend[1], blend[2].reshape(1, N)]
    scratch = [pltpu.VMEM((tm, tn), F32)] if nk > 1 else []
    if cache_w:
        scratch.append(pltpu.VMEM((tk, tn), BF16))
    return pl.pallas_call(
        functools.partial(_mm_body, nk=nk, act=act, has_gate=has_gate, has_blend=has_blend, cache_w=cache_w),
        grid=(N // tn, M // tm, nk),
        in_specs=in_specs,
        out_specs=pl.BlockSpec((tm, tn), lambda j, i, k: (i, j)),
        out_shape=jax.ShapeDtypeStruct((M, N), out_dtype),
        scratch_shapes=scratch,
        compiler_params=_params(("parallel", "arbitrary", "arbitrary")),
        name=name,
    )(*args)


def _norm_mod_rows(x, gain, mod, shift_row, scale_row):
    ms = jnp.mean(x * x, axis=-1, keepdims=True)
    y = x * lax.rsqrt(ms + NORM_EPS) * gain
    return y * (1.0 + mod[scale_row:scale_row + 1]) + mod[shift_row:shift_row + 1]


def _mix_out_body(o_ref, w_ref, res_ref, g_ref, mod_ref, h_ref, u_ref):
    mod = mod_ref[0]
    acc = jnp.dot(o_ref[...], w_ref[...], preferred_element_type=F32)
    h = res_ref[...] + mod[2:3] * acc
    h_ref[...] = h
    u_ref[...] = _norm_mod_rows(h, g_ref[...], mod, 3, 4).astype(u_ref.dtype)


def _mix_out(o, w, layer, res, res_row0, g, mod, lay):
    M, D = o.shape
    tm = min(lay.row_tile(), 512)
    r0 = res_row0 // tm
    row_spec = pl.BlockSpec((tm, D), lambda i: (i, 0))
    return pl.pallas_call(
        _mix_out_body,
        grid=(M // tm,),
        in_specs=[row_spec,
                  pl.BlockSpec((None, D, D), lambda i: (layer, 0, 0)),
                  pl.BlockSpec((tm, D), lambda i: (i + r0, 0)),
                  pl.BlockSpec((1, D), lambda i: (0, 0)),
                  pl.BlockSpec((1, N_MOD, D), lambda i: (lay.mod_row(i, tm), 0, 0))],
        out_specs=[row_spec, row_spec],
        out_shape=[jax.ShapeDtypeStruct((M, D), F32), jax.ShapeDtypeStruct((M, D), BF16)],
        compiler_params=_params(("arbitrary",)),
        name="mix_out",
    )(o, w, res, g.reshape(1, D), mod)


MIX_R, MIX_W, MIX_K, MIX_V, MIX_A, MIX_G = range(6)


def _rw_pre_body(*refs, tm, n_ctx_tiles, ctx_tiles, lat_tiles, lora_mix, lora_act):
    n_lora = len(lora_mix)
    h_ref, hp_ref, hn_ref, g_ref, mod_ref, mu_ref = refs[:6]
    lora_refs = refs[6:6 + n_lora]
    xr_ref, xk_ref, xv_ref = refs[6 + n_lora:9 + n_lora]
    mid_refs = refs[9 + n_lora:]
    i = pl.program_id(0)
    D = h_ref.shape[1]
    slabs = [slice(c, c + LANES) for c in range(0, D, LANES)]
    above = slice(SUBLANES - 1, SUBLANES)
    below = slice(0, 1)

    def inv_rms(ref, rows):
        sq = None
        for cs in slabs:
            x = ref[rows, cs]
            sq = x * x if sq is None else sq + x * x
        return lax.rsqrt(jnp.sum(sq, axis=-1, keepdims=True) * (1.0 / D) + NORM_EPS)

    r_tile, r_above, r_below = inv_rms(h_ref, slice(None)), inv_rms(hp_ref, above), inv_rms(hn_ref, below)
    in_ctx = i < n_ctx_tiles
    pos = jnp.where(in_ctx, i % ctx_tiles, (i - n_ctx_tiles) % lat_tiles)
    last = jnp.where(in_ctx, ctx_tiles - 1, lat_tiles - 1)
    rows = lax.broadcasted_iota(jnp.int32, (tm, 1), 0)
    acc = [None] * n_lora
    for cs in slabs:
        gain, shift, scale1 = g_ref[:, cs], mod_ref[0, 0:1, cs], 1.0 + mod_ref[0, 1:2, cs]
        u = h_ref[:, cs] * r_tile * gain * scale1 + shift
        u_above = jnp.where(pos == 0, 0.0, hp_ref[above, cs] * r_above * gain * scale1 + shift)
        u_below = jnp.where(pos == last, 0.0, hn_ref[below, cs] * r_below * gain * scale1 + shift)
        prev = jnp.where(rows == 0, u_above, pltpu.roll(u, 1, axis=0))
        nxt = jnp.where(rows == tm - 1, u_below, pltpu.roll(u, tm - 1, axis=0))
        xx = 0.5 * (prev + nxt) - u

        def mix(j):
            return (u + xx * mu_ref[j:j + 1, cs]).astype(BF16)

        xv = mix(MIX_V)
        xr_ref[:, cs] = mix(MIX_R)
        xk_ref[:, cs] = mix(MIX_K)
        xv_ref[:, cs] = xv
        for n, j in enumerate(lora_mix):
            part = jnp.dot(xv if j == MIX_V else mix(j), lora_refs[n][cs, :], preferred_element_type=F32)
            acc[n] = part if acc[n] is None else acc[n] + part
    for n, act in enumerate(lora_act):
        val = acc[n]
        if act == "tanh":
            val = jnp.tanh(val)
        elif act == "sigmoid":
            val = jax.nn.sigmoid(val)
        mid_refs[n][...] = val.astype(mid_refs[n].dtype)


def _rw_pre(h, g, mod, mu, lora, lay):
    M, D = h.shape
    tm = lay.seq_tile()
    per8 = tm // SUBLANES
    last8 = M // SUBLANES - 1
    ranks = [w.shape[1] for _, w, _ in lora]
    row_spec = pl.BlockSpec((tm, D), lambda i: (i, 0))
    return pl.pallas_call(
        functools.partial(_rw_pre_body, tm=tm, n_ctx_tiles=lay.n_ctx // tm, ctx_tiles=max(lay.C // tm, 1),
                          lat_tiles=lay.S // tm, lora_mix=tuple(j for j, _, _ in lora),
                          lora_act=tuple(a for _, _, a in lora)),
        grid=(M // tm,),
        in_specs=[row_spec,
                  pl.BlockSpec((SUBLANES, D), lambda i: (jnp.maximum(i * per8 - 1, 0), 0)),
                  pl.BlockSpec((SUBLANES, D), lambda i: (jnp.minimum((i + 1) * per8, last8), 0)),
                  pl.BlockSpec((1, D), lambda i: (0, 0)),
                  pl.BlockSpec((1, N_MOD, D), lambda i: (lay.mod_row(i, tm), 0, 0)),
                  pl.BlockSpec(mu.shape, lambda i: (0, 0))]
                 + [pl.BlockSpec((D, n), lambda i: (0, 0)) for n in ranks],
        out_specs=[row_spec] * 3 + [pl.BlockSpec((tm, n), lambda i: (i, 0)) for n in ranks],
        out_shape=[jax.ShapeDtypeStruct((M, D), BF16)] * 3 + [jax.ShapeDtypeStruct((M, n), BF16) for n in ranks],
        compiler_params=_params(("parallel",)),
        name="rw_pre",
    )(h, h, h, g.reshape(1, D), mod, mu, *[w for _, w, _ in lora])


def _wkv_body(*refs, reverse, npairs, nsub, epilogue):
    if epilogue:
        (r_ref, k_ref, v_ref, wl_ref, al_ref, w0_ref, a0_ref, kkp_ref, ka_ref,
         alo_ref, a0o_ref, yo_ref, g_ref, rk_ref, lng_ref, lnb_ref, o_ref, s_ref) = refs
    else:
        r_ref, k_ref, v_ref, wl_ref, al_ref, w0_ref, a0_ref, kkp_ref, ka_ref, o_ref, s_ref = refs
    L = WKV_CHUNK
    H = RW_HEAD

    @pl.when(pl.program_id(2) == 0)
    def _():
        s_ref[...] = jnp.zeros_like(s_ref)

    def order(row, col):
        return (row <= col) if reverse else (row >= col)

    row = lax.broadcasted_iota(jnp.int32, (L, L), 0)
    col = lax.broadcasted_iota(jnp.int32, (L, L), 1)
    tri = jnp.where(order(row, col), 1.0, 0.0).astype(BF16)
    tri3 = jnp.concatenate([tri, tri, tri], axis=1)

    prow = lax.broadcasted_iota(jnp.int32, (L, 2 * L), 0)
    pcol = lax.broadcasted_iota(jnp.int32, (L, 2 * L), 1) & (L - 1)
    incl = order(prow, pcol)
    strict = incl & (prow != pcol)
    incl2 = jnp.concatenate([incl, incl], axis=1)
    eye = jnp.where(prow == pcol, 1.0, 0.0)

    def sibling(s):
        return ((prow // (2 * s)) == (pcol // (2 * s))) & ((prow // s) != (pcol // s))

    bmask = (lax.broadcasted_iota(jnp.int32, (2 * L, LANES), 0) // L
             == lax.broadcasted_iota(jnp.int32, (2 * L, LANES), 1) // H)
    head_ones = jnp.where(lax.broadcasted_iota(jnp.int32, (LANES, LANES), 0) // H
                          == lax.broadcasted_iota(jnp.int32, (LANES, LANES), 1) // H, 1.0, 0.0).astype(BF16)

    def bdf(x):
        return jnp.where(bmask, jnp.concatenate([x, x], axis=0), 0.0)

    def bd(x):
        return bdf(x).astype(BF16)

    def dot(a, b):
        return jnp.dot(a, b, preferred_element_type=F32)

    def cat0(*xs):
        return jnp.concatenate(xs, axis=0)

    def cat1(*xs):
        return jnp.concatenate(xs, axis=1)

    prs = range(npairs)
    state = [s_ref[p] for p in prs]

    def chunk_stages(ci):
        rows = slice(ci * L, (ci + 1) * L)

        def tiles(ref):
            return [ref[rows, p * LANES:(p + 1) * LANES] for p in prs]

        def vecs(ref):
            return [ref[:, p * LANES:(p + 1) * LANES] for p in prs]

        def head_sum(xs):
            tot = dot(cat0(*xs).astype(BF16), head_ones)
            return [tot[p * L:(p + 1) * L] for p in prs]

        def store(vals):
            for p, val in zip(prs, vals):
                o_ref[rows, p * LANES:(p + 1) * LANES] = val.astype(o_ref.dtype)

        r, k, v = tiles(r_ref), tiles(k_ref), tiles(v_ref)
        ka = vecs(ka_ref)

        a = [jax.nn.sigmoid(a0 + al) for a0, al in zip(vecs(a0_ref), tiles(al_ref))]
        z = [-(w0 + wl) for w0, wl in zip(vecs(w0_ref), tiles(wl_ref))]
        softplus = [jnp.maximum(x, 0.0) + jnp.log(1.0 + jnp.exp(-jnp.abs(x))) for x in z]
        lw = [-jnp.exp(-x - 0.5) for x in softplus]
        yield
        kraw = [k[p] * kkp for p, kkp in zip(prs, vecs(kkp_ref))]
        norm2 = head_sum([x * x for x in kraw])
        kd = [k[p] * (1.0 + (a[p] - 1.0) * ka[p]) for p in prs]
        hi = [x.astype(BF16) for x in lw]
        rem = [lw[p] - hi[p].astype(F32) for p in prs]
        mid = [x.astype(BF16) for x in rem]
        lo = [(rem[p] - mid[p].astype(F32)).astype(BF16) for p in prs]
        g = [dot(tri3, cat0(hi[p], mid[p], lo[p])) for p in prs]
        ee = [jnp.exp(jnp.sum(x, axis=0, keepdims=True)) for x in lw]
        yield
        kk = [kraw[p] / jnp.maximum(jnp.sqrt(norm2[p]), RW_L2_EPS) for p in prs]
        en = [jnp.exp(-x) for x in g]
        abar = [-(kk[p] * jnp.exp(g[p] - lw[p])) for p in prs]
        rbar = [r[p] * jnp.exp(g[p]) for p in prs]
        bt = [kk[p] * a[p] * en[p] for p in prs]
        kt = [kd[p] * en[p] for p in prs]
        yield
        amat = [lax.dot_general(cat0(abar[p], rbar[p]).astype(BF16), cat0(bd(bt[p]), bd(kt[p])), _NT,
                                preferred_element_type=F32) for p in prs]
        vbd = [bd(x) for x in v]
        gam = [jnp.broadcast_to(x, (LANES, LANES)).T for x in ee]
        upd_lhs = [cat1(bdf(bt[p] * ee[p]).T, bdf(kt[p] * ee[p]).T).astype(BF16) for p in prs]
        yield
        a_ab = [jnp.where(strict, x[:L, :2 * L], 0.0) for x in amat]
        a_ak = [jnp.where(strict, x[:L, 2 * L:], 0.0).astype(BF16) for x in amat]
        a_rbk = [jnp.where(incl2, x[L:], 0.0).astype(BF16) for x in amat]
        av = [dot(a_ak[p], vbd[p]) for p in prs]

        t = [eye + jnp.where(sibling(1), x, 0.0) for x in a_ab]
        s = 2
        while s < L:
            sib = sibling(s)
            a_l = [bd(jnp.where(sib, a_ab[p], 0.0)) for p in prs]
            if s < SUBLANES:
                half = [dot(t[p].astype(BF16), a_l[p]).astype(BF16) for p in prs]
                yield
                t = [t[p] + dot(half[p], bd(t[p])) for p in prs]
            else:
                blocks = range(L // s)
                moving = [b for b in blocks if (b % 2 == 1) != reverse]

                def pick(x, n):
                    return x[n * s:(n + 1) * s]

                half = [dot(cat0(*[pick(t[p], b) for b in moving]).astype(BF16), a_l[p]).astype(BF16) for p in prs]
                yield
                upd = [dot(half[p], bd(t[p])) for p in prs]
                t = [cat0(*[pick(t[p], b) + pick(upd[p], moving.index(b)) if b in moving else pick(t[p], b)
                            for b in blocks]) for p in prs]
            yield
            s *= 2

        wu = [dot(t[p].astype(BF16), cat1(bd(abar[p]), bd(av[p]))) for p in prs]
        yield
        wr_lhs = [cat0(wu[p][:, :LANES], rbar[p]).astype(BF16) for p in prs]
        ut = [x[:, LANES:] for x in wu]
        c2_lhs = [cat0(a_rbk[p], upd_lhs[p]) for p in prs]
        yield

        wr = [dot(wr_lhs[p], state[p].astype(BF16)) for p in prs]
        yield
        u = [wr[p][:L] + ut[p] for p in prs]
        out2 = [dot(c2_lhs[p], cat0(bd(u[p]), vbd[p])) for p in prs]
        yield
        y = [wr[p][L:] + out2[p][:L] for p in prs]
        for p in prs:
            state[p] = state[p] * gam[p] + out2[p][L:]
        if not epilogue:
            store(y)
            return
        yield

        inv_n = 1.0 / H
        ytot = [y[p] + yo for p, yo in zip(prs, tiles(yo_ref))]
        mean = head_sum(ytot)
        a_o = [jax.nn.sigmoid(a0 + al) for a0, al in zip(vecs(a0o_ref), tiles(alo_ref))]
        kd_sum = [kd[p] + k[p] * (1.0 + (a_o[p] - 1.0) * ka[p]) for p in prs]
        bonus = head_sum([r[p] * rk * kd_sum[p] for p, rk in zip(prs, vecs(rk_ref))])
        yield
        dev = [ytot[p] - mean[p] * inv_n for p in prs]
        var = head_sum([x * x for x in dev])
        yield
        lng, lnb, gate = vecs(lng_ref), vecs(lnb_ref), tiles(g_ref)
        store([(dev[p] * lax.rsqrt(var[p] * inv_n + RW_GN_EPS) * lng[p] + lnb[p] + bonus[p] * v[p]) * gate[p]
               for p in prs])

    waiting = [chunk_stages(ci) for ci in (reversed(range(nsub)) if reverse else range(nsub))]
    running = []
    step = 0
    while waiting or running:
        if waiting and step % WKV_STAGGER == 0:
            running.append(waiting.pop(0))
        for gen in list(running):
            if next(gen, "done") == "done":
                running.remove(gen)
        step += 1
    for p in prs:
        s_ref[p] = state[p]


def _wkv(r, k, v, wl, al, w0, a0, kkp, ka, lay, *, reverse, epilogue=None):
    M, D = r.shape
    L = WKV_CHUNK
    td = min(D, 8 * LANES)
    npairs = td // LANES
    nsub = _pick(math.gcd(lay.C, lay.S) // L, (4, 2, 1))
    tb = nsub * L
    ctx_blk = lay.C // tb
    lat_blk = lay.S // tb
    ctx_total = lay.n_ctx // tb

    def tok_block(b, c):
        if reverse:
            ctx_c = ctx_blk - 1 - c
            lat_c = lat_blk - 1 - (c - ctx_blk)
        else:
            ctx_c = c
            lat_c = c - ctx_blk
        return jnp.where(c < ctx_blk, b * ctx_blk + ctx_c, ctx_total + b * lat_blk + lat_c)

    mat = pl.BlockSpec((tb, td), lambda b, d, c: (tok_block(b, c), d))
    vec = pl.BlockSpec((1, td), lambda b, d, c: (0, d))
    row = lambda x: x.reshape(1, D)
    args = [r, k, v, wl, al, row(w0), row(a0), row(kkp), row(ka)]
    specs = [mat] * 5 + [vec] * 4
    if epilogue is not None:
        al_o, a0_o, y_o, gate, rk, lnx_g, lnx_b = epilogue
        args += [al_o, row(a0_o), y_o, gate, row(rk), row(lnx_g), row(lnx_b)]
        specs += [mat, vec, mat, mat, vec, vec, vec]
    return pl.pallas_call(
        functools.partial(_wkv_body, reverse=reverse, npairs=npairs, nsub=nsub, epilogue=epilogue is not None),
        grid=(lay.B, D // td, ctx_blk + lat_blk),
        in_specs=specs,
        out_specs=mat,
        out_shape=jax.ShapeDtypeStruct((M, D), F32 if epilogue is None else BF16),
        scratch_shapes=[pltpu.VMEM((npairs, LANES, LANES), F32)],
        compiler_params=_params(("parallel", "parallel", "arbitrary")),
        name="wkv_bwd" if reverse else "wkv_fwd",
    )(*args)


ATTN_KEY_CHUNK = 1024


def _attn_body(q_ref, *refs, n_kv):
    k_refs, vt_refs = refs[:n_kv], refs[n_kv:2 * n_kv]
    o_ref = refs[2 * n_kv]
    q = q_ref[...]
    pieces = []
    for k_ref, vt_ref in zip(k_refs, vt_refs):
        n = k_ref.shape[0]
        for lo in range(0, n, ATTN_KEY_CHUNK):
            pieces.append((k_ref, vt_ref, lo, min(lo + ATTN_KEY_CHUNK, n)))

    def scores(piece):
        k_ref, _, lo, hi = piece
        return lax.dot_general(k_ref[lo:hi, :], q, _NT, preferred_element_type=F32)

    s_next = scores(pieces[0])
    m = l = acc = None
    for c, (_, vt_ref, lo, hi) in enumerate(pieces):
        s = s_next
        if c + 1 < len(pieces):
            s_next = scores(pieces[c + 1])
        m_c = jnp.max(s, axis=0, keepdims=True)
        m_new = m_c if m is None else jnp.maximum(m, m_c)
        p = jnp.exp2(s - m_new)
        l_c = jnp.sum(p, axis=0, keepdims=True)
        pv = jnp.dot(vt_ref[:, lo:hi], p.astype(BF16), preferred_element_type=F32)
        if m is None:
            l, acc = l_c, pv
        else:
            alpha = jnp.exp2(m - m_new)
            l, acc = alpha * l + l_c, alpha * acc + pv
        m = m_new
    o_ref[...] = (acc / l).T.astype(o_ref.dtype)


def _attn(q, k, vt, heads, lay, *, q_row0, n_q_rows, with_lat_keys):
    dk = k.shape[1] // heads
    dv = vt.shape[0] // heads
    tq = _pick(n_q_rows, (2048, 1024, 512, 256, 128, 64, 32, 16, 8))
    per_b = n_q_rows // tq
    q0 = q_row0 // tq
    kv = [(lay.C, 0)]
    k_lat, vt_lat = k, vt
    if with_lat_keys:
        kv.append((lay.S, lay.n_ctx // lay.S))
        if lay.n_ctx % lay.S:
            k_lat, vt_lat, kv[1] = k[lay.n_ctx:], vt[:, lay.n_ctx:], (lay.S, 0)
    in_specs = [pl.BlockSpec((tq, dk), lambda b, h, i: (q0 + b * per_b + i, h))]
    in_specs += [pl.BlockSpec((n, dk), functools.partial(lambda b, h, i, base: (base + b, h), base=base))
                 for n, base in kv]
    in_specs += [pl.BlockSpec((dv, n), functools.partial(lambda b, h, i, base: (h, base + b), base=base))
                 for n, base in kv]
    operands = [q, k] + ([k_lat] if with_lat_keys else []) + [vt] + ([vt_lat] if with_lat_keys else [])
    return pl.pallas_call(
        functools.partial(_attn_body, n_kv=len(kv)),
        grid=(lay.B, heads, per_b),
        in_specs=in_specs,
        out_specs=pl.BlockSpec((tq, dv), lambda b, h, i: (b * per_b + i, h)),
        out_shape=jax.ShapeDtypeStruct((lay.B * n_q_rows, heads * dv), BF16),
        compiler_params=_params(("parallel", "parallel", "arbitrary")),
        name="attn",
    )(*operands)


def _rms_rows(x, gain):
    return x * lax.rsqrt(jnp.mean(x * x, axis=-1, keepdims=True) + NORM_EPS) * gain


def _rope_lanes(x, cos, sin):
    lane = lax.broadcasted_iota(jnp.int32, x.shape, 1)
    first = (lane % MLA_ROPE) < MLA_ROPE // 2
    partner = jnp.where(first, pltpu.roll(x, LANES - MLA_ROPE // 2, axis=1), pltpu.roll(x, MLA_ROPE // 2, axis=1))
    return x * cos + partner * sin


def _mla_down_body(h_ref, g_ref, mod_ref, wkv_ref, wq_ref, gkv_ref, gr_ref, gq_ref, cos_ref, sin_ref,
                   ckv_ref, kr_ref, cq_ref, *, kv_lora):
    u = _norm_mod_rows(h_ref[...], g_ref[...], mod_ref[0], 0, 1).astype(BF16)
    acc = jnp.dot(u, wkv_ref[...], preferred_element_type=F32)
    ckv_ref[...] = _rms_rows(acc[:, :kv_lora], gkv_ref[...]).astype(ckv_ref.dtype)
    kr = acc[:, kv_lora:]
    ms = jnp.sum(kr * kr, axis=-1, keepdims=True) * (1.0 / MLA_ROPE)
    kr = kr * lax.rsqrt(ms + NORM_EPS) * gr_ref[...]
    kr_ref[...] = _rope_lanes(kr, cos_ref[...], sin_ref[...]).astype(kr_ref.dtype)
    acc_q = jnp.dot(u, wq_ref[...], preferred_element_type=F32)
    cq_ref[...] = _rms_rows(acc_q, gq_ref[...]).astype(cq_ref.dtype)


def _rms_groups(x, gain, group):
    ones = jnp.where(lax.broadcasted_iota(jnp.int32, (LANES, LANES), 0) // group
                     == lax.broadcasted_iota(jnp.int32, (LANES, LANES), 1) // group, 1.0, 0.0).astype(BF16)
    ss = jnp.dot((x * x).astype(BF16), ones, preferred_element_type=F32)
    return x * lax.rsqrt(ss * (1.0 / group) + NORM_EPS) * gain


def _mla_ukv_body(c_ref, w_ref, g_ref, kr_ref, kcat_ref, vt_ref, *, heads_per_tile):
    acc = jnp.dot(c_ref[...], w_ref[...], preferred_element_type=F32)
    kr = kr_ref[...]
    dk = MLA_NOPE + LANES
    for h in range(heads_per_tile):
        base = h * (MLA_NOPE + MLA_V)
        kcat_ref[:, h * dk:h * dk + MLA_NOPE] = _rms_groups(acc[:, base:base + MLA_NOPE], g_ref[...],
                                                            MLA_NOPE).astype(kcat_ref.dtype)
        kcat_ref[:, h * dk + MLA_NOPE:(h + 1) * dk] = kr
        vt_ref[h * MLA_V:(h + 1) * MLA_V, :] = acc[:, base + MLA_NOPE:base + MLA_NOPE + MLA_V].T.astype(vt_ref.dtype)


def _mla_uq_body(c_ref, wn_ref, wr_ref, gn_ref, gr_ref, cos_ref, sin_ref, q_ref, *, heads_per_tile, qscale):
    x = c_ref[...]
    nope = jnp.dot(x, wn_ref[...], preferred_element_type=F32)
    rope = jnp.dot(x, wr_ref[...], preferred_element_type=F32)
    low = lax.broadcasted_iota(jnp.int32, (x.shape[0], LANES), 1) < MLA_ROPE
    dk = MLA_NOPE + LANES
    for b in range(heads_per_tile // 2):
        xr = _rms_groups(rope[:, b * LANES:(b + 1) * LANES], gr_ref[...], MLA_ROPE)
        xr = _rope_lanes(xr, cos_ref[...], sin_ref[...]) * qscale
        for t in range(2):
            h = 2 * b + t
            qn = _rms_groups(nope[:, h * MLA_NOPE:(h + 1) * MLA_NOPE], gn_ref[...], MLA_NOPE) * qscale
            q_ref[:, h * dk:h * dk + MLA_NOPE] = qn.astype(q_ref.dtype)
            rr = xr if t == 0 else pltpu.roll(xr, MLA_ROPE, axis=1)
            q_ref[:, h * dk + MLA_NOPE:(h + 1) * dk] = jnp.where(low, rr, 0.0).astype(q_ref.dtype)


def _pad_cols(w, n):
    return jnp.pad(w, ((0, 0), (0, n - w.shape[1])))


def _pad_rows(w, n):
    return jnp.pad(w, ((0, n - w.shape[0]), (0, 0)))


def _up128(n):
    return -(-n // LANES) * LANES


def _rope_tables(lay):
    n = lay.S
    rows = n // GRID_W
    row = jnp.broadcast_to(jnp.arange(rows)[:, None], (rows, GRID_W)).reshape(-1)
    col = jnp.broadcast_to(jnp.arange(GRID_W)[None, :], (rows, GRID_W)).reshape(-1)
    n_freq = MLA_ROPE // 4
    inv = ROPE_THETA ** (-jnp.arange(n_freq, dtype=F32) / n_freq)
    ang = jnp.concatenate([row[:, None].astype(F32) * inv, col[:, None].astype(F32) * inv], axis=-1)
    cos, sin = jnp.cos(ang), jnp.sin(ang)
    reps = LANES // MLA_ROPE
    cos_l = jnp.tile(jnp.concatenate([cos, cos], axis=1), (lay.B, reps))
    sin_l = jnp.tile(jnp.concatenate([-sin, sin], axis=1), (lay.B, reps))
    cos_t = jnp.concatenate([jnp.ones((lay.n_ctx, LANES), F32), cos_l], axis=0)
    sin_t = jnp.concatenate([jnp.zeros((lay.n_ctx, LANES), F32), sin_l], axis=0)
    return cos_t, sin_t


def _rwkv_mixer(h, norm_g, mod, lay, v_first, j, big_w, pw, vres, rk, lnx_g, lnx_b):
    wr, wk, wv = big_w
    mu, w0, w1, w2, a0, a1, a2, g1, g2, kkp, ka = pw
    nw, na = _up128(w1.shape[2]), _up128(a1.shape[2])
    lora = [(MIX_W, jnp.concatenate([_pad_cols(w1[e], nw) for e in range(2)], axis=1).astype(BF16), "tanh"),
            (MIX_A, jnp.concatenate([_pad_cols(a1[e], na) for e in range(2)], axis=1).astype(BF16), None),
            (MIX_G, g1.astype(BF16), "sigmoid")]
    if vres is not None:
        v0, v1, v2 = vres
        nv = _up128(v1.shape[1])
        lora.append((MIX_V, _pad_cols(v1, nv).astype(BF16), None))
    xr, xk, xv, w_mid, a_mid, g_mid, *v_mid = _rw_pre(h, norm_g, mod, mu, lora, lay)
    mm = functools.partial(_mm, lay=lay)
    r = mm(xr, wr, layer=j, name="rw_r")
    k = mm(xk, wk, layer=j, name="rw_k")
    v = mm(xv, wv, layer=j, name="rw_v")
    if vres is not None:
        v = mm(v_mid[0], _pad_rows(v2, nv).astype(BF16), blend=(v, v_first, v0), name="rw_v2")
    g = mm(g_mid, g2.astype(BF16), out_dtype=BF16, name="rw_g2")
    wl = [mm(w_mid, _pad_rows(w2[e], nw).astype(BF16), x_kblock=e, out_dtype=BF16, name="rw_w2") for e in range(2)]
    al = [mm(a_mid, _pad_rows(a2[e], na).astype(BF16), x_kblock=e, out_dtype=BF16, name="rw_a2") for e in range(2)]

    y_fwd = _wkv(r, k, v, wl[0], al[0], w0[0], a0[0], kkp, ka, lay, reverse=False)
    o = _wkv(r, k, v, wl[1], al[1], w0[1], a0[1], kkp, ka, lay, reverse=True,
             epilogue=(al[0], a0[0], y_fwd, g, rk.reshape(-1), lnx_g, lnx_b))
    return o, v


def _mla_mixer(h, norm_g, mod, lay, rope_tabs, wdq, qnorm, wuq, wdkv, kvnorm, wukv, qn_nope, qn_rope, kn_nope,
               kn_rope, need_ctx):
    M, D = h.shape
    heads = D // MLA_V
    kv_lora, q_lora = kvnorm.shape[0], qnorm.shape[0]
    cos_t, sin_t = rope_tabs
    tm = lay.row_tile()
    hpt = _pick(heads, (4, 2))
    dk = MLA_NOPE + LANES
    pad = LANES - MLA_ROPE
    perm = jnp.concatenate([jnp.arange(0, MLA_ROPE, 2), jnp.arange(1, MLA_ROPE, 2)])
    row = lambda t: t.reshape(1, -1)
    par1 = _params(("parallel",))
    par2 = _params(("parallel", "parallel"))

    w_dkv = jnp.concatenate([wdkv[:, :kv_lora], wdkv[:, kv_lora:][:, perm], jnp.zeros((D, pad), F32)], axis=1)
    tmd = min(tm, 512)
    c_kv, k_rope, c_q = pl.pallas_call(
        functools.partial(_mla_down_body, kv_lora=kv_lora),
        grid=(M // tmd,),
        in_specs=[pl.BlockSpec((tmd, D), lambda i: (i, 0)),
                  pl.BlockSpec((1, D), lambda i: (0, 0)),
                  pl.BlockSpec((1, N_MOD, D), lambda i: (lay.mod_row(i, tmd), 0, 0)),
                  pl.BlockSpec((D, kv_lora + LANES), lambda i: (0, 0)),
                  pl.BlockSpec((D, q_lora), lambda i: (0, 0)),
                  pl.BlockSpec((1, kv_lora), lambda i: (0, 0)),
                  pl.BlockSpec((1, LANES), lambda i: (0, 0)),
                  pl.BlockSpec((1, q_lora), lambda i: (0, 0)),
                  pl.BlockSpec((tmd, LANES), lambda i: (i, 0)),
                  pl.BlockSpec((tmd, LANES), lambda i: (i, 0))],
        out_specs=[pl.BlockSpec((tmd, kv_lora), lambda i: (i, 0)), pl.BlockSpec((tmd, LANES), lambda i: (i, 0)),
                   pl.BlockSpec((tmd, q_lora), lambda i: (i, 0))],
        out_shape=[jax.ShapeDtypeStruct((M, kv_lora), BF16), jax.ShapeDtypeStruct((M, LANES), BF16),
                   jax.ShapeDtypeStruct((M, q_lora), BF16)],
        compiler_params=par1, name="mla_down",
    )(h, row(norm_g), mod, w_dkv.astype(BF16), wdq.astype(BF16), row(kvnorm),
      row(jnp.pad(kn_rope[perm], (0, pad))), row(qnorm), cos_t, sin_t)

    k_cat, vt = pl.pallas_call(
        functools.partial(_mla_ukv_body, heads_per_tile=hpt),
        grid=(heads // hpt, M // tm),
        in_specs=[pl.BlockSpec((tm, kv_lora), lambda j, i: (i, 0)),
                  pl.BlockSpec((kv_lora, hpt * (MLA_NOPE + MLA_V)), lambda j, i: (0, j)),
                  pl.BlockSpec((1, MLA_NOPE), lambda j, i: (0, 0)),
                  pl.BlockSpec((tm, LANES), lambda j, i: (i, 0))],
        out_specs=[pl.BlockSpec((tm, hpt * dk), lambda j, i: (i, j)),
                   pl.BlockSpec((hpt * MLA_V, tm), lambda j, i: (j, i))],
        out_shape=[jax.ShapeDtypeStruct((M, heads * dk), BF16), jax.ShapeDtypeStruct((heads * MLA_V, M), BF16)],
        compiler_params=par2, name="mla_ukv",
    )(c_kv, wukv.astype(BF16), row(kn_nope), k_rope)

    q_rows = M if need_ctx else lay.B * lay.S
    r0 = (M - q_rows) // tm
    w3 = wuq.reshape(q_lora, heads, MLA_NOPE + MLA_ROPE)
    w_nope = w3[:, :, :MLA_NOPE].reshape(q_lora, heads * MLA_NOPE)
    w_rope = w3[:, :, MLA_NOPE:][:, :, perm].reshape(q_lora, heads * MLA_ROPE)
    qscale = math.log2(math.e) / math.sqrt(MLA_NOPE + MLA_ROPE)
    q_cat = pl.pallas_call(
        functools.partial(_mla_uq_body, heads_per_tile=hpt, qscale=qscale),
        grid=(heads // hpt, q_rows // tm),
        in_specs=[pl.BlockSpec((tm, q_lora), lambda j, i: (i + r0, 0)),
                  pl.BlockSpec((q_lora, hpt * MLA_NOPE), lambda j, i: (0, j)),
                  pl.BlockSpec((q_lora, hpt * MLA_ROPE), lambda j, i: (0, j)),
                  pl.BlockSpec((1, MLA_NOPE), lambda j, i: (0, 0)),
                  pl.BlockSpec((1, LANES), lambda j, i: (0, 0)),
                  pl.BlockSpec((tm, LANES), lambda j, i: (i + r0, 0)),
                  pl.BlockSpec((tm, LANES), lambda j, i: (i + r0, 0))],
        out_specs=pl.BlockSpec((tm, hpt * dk), lambda j, i: (i, j)),
        out_shape=jax.ShapeDtypeStruct((q_rows, heads * dk), BF16),
        compiler_params=par2, name="mla_uq",
    )(c_q, w_nope.astype(BF16), w_rope.astype(BF16), row(qn_nope), row(jnp.tile(qn_rope[perm], LANES // MLA_ROPE)),
      cos_t, sin_t)

    o_lat = _attn(q_cat, k_cat, vt, heads, lay, q_row0=q_rows - lay.B * lay.S, n_q_rows=lay.S, with_lat_keys=True)
    if not need_ctx:
        return o_lat
    o_ctx = _attn(q_cat, k_cat, vt, heads, lay, q_row0=0, n_q_rows=lay.C, with_lat_keys=False)
    return jnp.concatenate([o_ctx, o_lat], axis=0)


def kernel(x, c, ctx, c_ctx, mod_w, mod_b, norm_g, mlp_w1, mlp_w2, rw_mu, rw_wr, rw_wk, rw_wv, rw_wo, rw_w0, rw_w1, rw_w2, rw_a0, rw_a1, rw_a2, rw_g1, rw_g2, rw_kk, rw_ka, rw_rk, rw_lnx_g, rw_lnx_b, rw_v0, rw_v1, rw_v2, mla_wdq, mla_qnorm, mla_wuq, mla_wdkv, mla_kvnorm, mla_wukv, mla_qn_nope, mla_qn_rope, mla_kn_nope, mla_kn_rope, mla_wo):
    B, S, D = x.shape
    C = ctx.shape[1]
    depth = mod_w.shape[0]
    lay = _Layout(B, C, S)
    rope = _rope_tables(lay)

    sc_all = jnp.concatenate([jax.nn.silu(c_ctx)[None], jax.nn.silu(c)], axis=0)
    h = jnp.concatenate([ctx.reshape(B * C, D), x.reshape(B * S, D)], axis=0)
    v_first = None
    mlp_w2_bf16 = mlp_w2.astype(BF16)
    rw_wo_bf16, mla_wo_bf16 = rw_wo.astype(BF16), mla_wo.astype(BF16)

    for i in range(depth):
        last = i == depth - 1
        j = i // 2
        mod = (_mm(sc_all, mod_w, layer=i, name="adaln") + mod_b[i]).reshape(B + 1, N_MOD, D)
        if i % 2 == 0:
            pw = (rw_mu[j], rw_w0[j], rw_w1[j], rw_w2[j],
                  rw_a0[j], rw_a1[j], rw_a2[j], rw_g1[j], rw_g2[j], rw_kk[j], rw_ka[j])
            vres = None if j == 0 else (rw_v0[j - 1], rw_v1[j - 1], rw_v2[j - 1])
            o, v_cur = _rwkv_mixer(h, norm_g[i, 0], mod, lay, v_first, j, (rw_wr, rw_wk, rw_wv), pw, vres,
                                   rw_rk[j], rw_lnx_g[j], rw_lnx_b[j])
            if j == 0:
                v_first = v_cur
            wo = rw_wo_bf16
        else:
            o = _mla_mixer(h, norm_g[i, 0], mod, lay, rope, mla_wdq[j], mla_qnorm[j], mla_wuq[j], mla_wdkv[j], mla_kvnorm[j],
                           mla_wukv[j], mla_qn_nope[j], mla_qn_rope[j], mla_kn_nope[j], mla_kn_rope[j],
                           need_ctx=not last)
            wo = mla_wo_bf16
        res_row0 = 0
        if last:
            if o.shape[0] != B * S:
                o = o[lay.n_ctx:]
            res_row0 = lay.n_ctx
            lay = _Layout(B, C, S, with_ctx=False)
        h, u2 = _mix_out(o, wo, j, h, res_row0, norm_g[i, 1], mod, lay)
        hid = _mm(u2, mlp_w1, layer=i, act="relu2", out_dtype=BF16, lay=lay, name="mlp_up")
        h = _mm(hid, mlp_w2_bf16, layer=i, res=h, gate=mod[:, 5], lay=lay, name="mlp_down")
    return h.reshape(B, S, D)
```

```python
import functools
import math

import jax
import jax.numpy as jnp
from jax import lax
from jax.experimental import pallas as pl
from jax.experimental.pallas import tpu as pltpu

F32 = jnp.float32
BF16 = jnp.bfloat16

NORM_EPS = 1e-6
N_MOD = 6
GRID_W = 64
RW_HEAD = 64
RW_GN_EPS = 64e-5
RW_L2_EPS = 1e-12
MLA_NOPE = 128
MLA_ROPE = 64
MLA_V = 128
ROPE_THETA = 10000.0

LANES = 128
SUBLANES = 8
WKV_CHUNK = 64
WKV_STAGGER = 3
VMEM_LIMIT_BYTES = 56 * 1024 * 1024

_NT = (((1,), (1,)), ((), ()))


def _pick(n, prefs):
    for p in prefs:
        if n % p == 0:
            return p
    return n


def _params(sem):
    return pltpu.CompilerParams(dimension_semantics=sem, vmem_limit_bytes=VMEM_LIMIT_BYTES)


class _Layout:
    def __init__(self, B, C, S, with_ctx=True):
        self.B, self.C, self.S = B, C, S
        self.n_ctx = B * C if with_ctx else 0
        self.M = self.n_ctx + B * S

    def row_tile(self):
        return _pick(math.gcd(self.n_ctx, self.S) if self.n_ctx else self.S, (1024, 512, 256, 128, 64, 32, 16, 8))

    def seq_tile(self):
        return _pick(math.gcd(self.C, self.S) if self.n_ctx else self.S, (256, 128, 64, 32, 16, 8))

    def mod_row(self, i, tm):
        n_ctx_tiles = self.n_ctx // tm
        per_b = self.S // tm
        lat = 1 + (i - n_ctx_tiles) // per_b
        if n_ctx_tiles == 0:
            return lat
        return jnp.where(i < n_ctx_tiles, 0, lat)


def _mm_body(*refs, nk, act, has_gate, has_blend, cache_w):
    if has_gate:
        x_ref, w_ref, res_ref, gate_ref, o_ref, *scratch = refs
    elif has_blend:
        x_ref, w_ref, cur_ref, first_ref, bias_ref, o_ref, *scratch = refs
    else:
        x_ref, w_ref, o_ref, *scratch = refs

    def finish(acc):
        if act == "relu2":
            acc = jnp.square(jnp.maximum(acc, 0.0))
        elif act == "sigmoid":
            acc = jax.nn.sigmoid(acc)
        elif act == "tanh":
            acc = jnp.tanh(acc)
        if has_gate:
            acc = res_ref[...] + gate_ref[0] * acc
        if has_blend:
            cur = cur_ref[...]
            acc = cur + (first_ref[...] - cur) * jax.nn.sigmoid(bias_ref[...] + acc)
        o_ref[...] = acc.astype(o_ref.dtype)

    if cache_w:
        wc_ref = scratch[-1]

        @pl.when(pl.program_id(1) == 0)
        def _():
            wc_ref[...] = w_ref[...].astype(BF16)

        w = wc_ref[...]
    else:
        w = w_ref[...].astype(BF16)
    part = jnp.dot(x_ref[...].astype(BF16), w, preferred_element_type=F32)
    if nk == 1:
        finish(part)
    else:
        acc_ref = scratch[0]
        k = pl.program_id(2)

        @pl.when(k == 0)
        def _():
            acc_ref[...] = part

        @pl.when(k > 0)
        def _():
            acc_ref[...] += part

        @pl.when(k == nk - 1)
        def _():
            finish(acc_ref[...])


def _mm(x, w, *, layer=None, act=None, out_dtype=F32, res=None, gate=None, blend=None, lay=None,
        tm=None, x_kblock=0, name="mm"):
    M = x.shape[0]
    K, N = w.shape[-2:]
    if tm is None:
        tm = lay.row_tile() if lay is not None else _pick(M, (1024, 512, 256, 128, 64, 32, 16, 8))
    tn = _pick(N, (1024, 512, 256, 128))
    tk = K if K <= 2048 else _pick(K, (2048, 1024, 512))
    if K > 2048 and w.dtype == BF16 and tm % 512 == 0 and tn % 512 == 0:
        tm, tn, tk = 256, 1024, K
    nk = K // tk
    has_gate = gate is not None
    has_blend = blend is not None
    cache_w = w.dtype == F32 and nk == 1 and M // tm > 1
    if w.ndim == 3:
        w_spec = pl.BlockSpec((None, tk, tn), lambda j, i, k: (layer, k, j))
    else:
        w_spec = pl.BlockSpec((tk, tn), lambda j, i, k: (k, j))
    in_specs = [pl.BlockSpec((tm, tk), lambda j, i, k: (i, k + x_kblock * nk)), w_spec]
    args = [x, w]
    if has_gate:
        in_specs += [pl.BlockSpec((tm, tn), lambda j, i, k: (i, j)),
                     pl.BlockSpec((1, 1, tn), lambda j, i, k: (lay.mod_row(i, tm), 0, j))]
        args += [res, gate.reshape(gate.shape[0], 1, N)]
    if has_blend:
        in_specs += [pl.BlockSpec((tm, tn), lambda j, i, k: (i, j)), pl.BlockSpec((tm, tn), lambda j, i, k: (i, j)),
                     pl.BlockSpec((1, tn), lambda j, i, k: (0, j))]
        args += [blend[0], blend[1], blend[2].reshape(1, N)]
    scratch = [pltpu.VMEM((tm, tn), F32)] if nk > 1 else []
    if cache_w:
        scratch.append(pltpu.VMEM((tk, tn), BF16))
    return pl.pallas_call(
        functools.partial(_mm_body, nk=nk, act=act, has_gate=has_gate, has_blend=has_blend, cache_w=cache_w),
        grid=(N // tn, M // tm, nk),
        in_specs=in_specs,
        out_specs=pl.BlockSpec((tm, tn), lambda j, i, k: (i, j)),
        out_shape=jax.ShapeDtypeStruct((M, N), out_dtype),
        scratch_shapes=scratch,
        compiler_params=_params(("parallel", "arbitrary", "arbitrary")),
        name=name,
    )(*args)


def _norm_mod_rows(x, gain, mod, shift_row, scale_row):
    ms = jnp.mean(x * x, axis=-1, keepdims=True)
    y = x * lax.rsqrt(ms + NORM_EPS) * gain
    return y * (1.0 + mod[scale_row:scale_row + 1]) + mod[shift_row:shift_row + 1]


def _mix_out_body(o_ref, w_ref, res_ref, g_ref, mod_ref, h_ref, u_ref):
    mod = mod_ref[0]
    acc = jnp.dot(o_ref[...], w_ref[...], preferred_element_type=F32)
    h = res_ref[...] + mod[2:3] * acc
    h_ref[...] = h
    u_ref[...] = _norm_mod_rows(h, g_ref[...], mod, 3, 4).astype(u_ref.dtype)


def _mix_out(o, w, layer, res, res_row0, g, mod, lay):
    M, D = o.shape
    tm = min(lay.row_tile(), 512)
    r0 = res_row0 // tm
    row_spec = pl.BlockSpec((tm, D), lambda i: (i, 0))
    return pl.pallas_call(
        _mix_out_body,
        grid=(M // tm,),
        in_specs=[row_spec,
                  pl.BlockSpec((None, D, D), lambda i: (layer, 0, 0)),
                  pl.BlockSpec((tm, D), lambda i: (i + r0, 0)),
                  pl.BlockSpec((1, D), lambda i: (0, 0)),
                  pl.BlockSpec((1, N_MOD, D), lambda i: (lay.mod_row(i, tm), 0, 0))],
        out_specs=[row_spec, row_spec],
        out_shape=[jax.ShapeDtypeStruct((M, D), F32), jax.ShapeDtypeStruct((M, D), BF16)],
        compiler_params=_params(("arbitrary",)),
        name="mix_out",
    )(o, w, res, g.reshape(1, D), mod)


MIX_R, MIX_W, MIX_K, MIX_V, MIX_A, MIX_G = range(6)


def _rw_pre_body(*refs, tm, n_ctx_tiles, ctx_tiles, lat_tiles, lora_mix, lora_act):
    n_lora = len(lora_mix)
    h_ref, hp_ref, hn_ref, g_ref, mod_ref, mu_ref = refs[:6]
    lora_refs = refs[6:6 + n_lora]
    xr_ref, xk_ref, xv_ref = refs[6 + n_lora:9 + n_lora]
    mid_refs = refs[9 + n_lora:]
    i = pl.program_id(0)
    D = h_ref.shape[1]
    slabs = [slice(c, c + LANES) for c in range(0, D, LANES)]
    above = slice(SUBLANES - 1, SUBLANES)
    below = slice(0, 1)

    def inv_rms(ref, rows):
        sq = None
        for cs in slabs:
            x = ref[rows, cs]
            sq = x * x if sq is None else sq + x * x
        return lax.rsqrt(jnp.sum(sq, axis=-1, keepdims=True) * (1.0 / D) + NORM_EPS)

    r_tile, r_above, r_below = inv_rms(h_ref, slice(None)), inv_rms(hp_ref, above), inv_rms(hn_ref, below)
    in_ctx = i < n_ctx_tiles
    pos = jnp.where(in_ctx, i % ctx_tiles, (i - n_ctx_tiles) % lat_tiles)
    last = jnp.where(in_ctx, ctx_tiles - 1, lat_tiles - 1)
    rows = lax.broadcasted_iota(jnp.int32, (tm, 1), 0)
    acc = [None] * n_lora
    for cs in slabs:
        gain, shift, scale1 = g_ref[:, cs], mod_ref[0, 0:1, cs], 1.0 + mod_ref[0, 1:2, cs]
        u = h_ref[:, cs] * r_tile * gain * scale1 + shift
        u_above = jnp.where(pos == 0, 0.0, hp_ref[above, cs] * r_above * gain * scale1 + shift)
        u_below = jnp.where(pos == last, 0.0, hn_ref[below, cs] * r_below * gain * scale1 + shift)
        prev = jnp.where(rows == 0, u_above, pltpu.roll(u, 1, axis=0))
        nxt = jnp.where(rows == tm - 1, u_below, pltpu.roll(u, tm - 1, axis=0))
        xx = 0.5 * (prev + nxt) - u

        def mix(j):
            return (u + xx * mu_ref[j:j + 1, cs]).astype(BF16)

        xv = mix(MIX_V)
        xr_ref[:, cs] = mix(MIX_R)
        xk_ref[:, cs] = mix(MIX_K)
        xv_ref[:, cs] = xv
        for n, j in enumerate(lora_mix):
            part = jnp.dot(xv if j == MIX_V else mix(j), lora_refs[n][cs, :], preferred_element_type=F32)
            acc[n] = part if acc[n] is None else acc[n] + part
    for n, act in enumerate(lora_act):
        val = acc[n]
        if act == "tanh":
            val = jnp.tanh(val)
        elif act == "sigmoid":
            val = jax.nn.sigmoid(val)
        mid_refs[n][...] = val.astype(mid_refs[n].dtype)


def _rw_pre(h, g, mod, mu, lora, lay):
    M, D = h.shape
    tm = lay.seq_tile()
    per8 = tm // SUBLANES
    last8 = M // SUBLANES - 1
    ranks = [w.shape[1] for _, w, _ in lora]
    row_spec = pl.BlockSpec((tm, D), lambda i: (i, 0))
    return pl.pallas_call(
        functools.partial(_rw_pre_body, tm=tm, n_ctx_tiles=lay.n_ctx // tm, ctx_tiles=max(lay.C // tm, 1),
                          lat_tiles=lay.S // tm, lora_mix=tuple(j for j, _, _ in lora),
                          lora_act=tuple(a for _, _, a in lora)),
        grid=(M // tm,),
        in_specs=[row_spec,
                  pl.BlockSpec((SUBLANES, D), lambda i: (jnp.maximum(i * per8 - 1, 0), 0)),
                  pl.BlockSpec((SUBLANES, D), lambda i: (jnp.minimum((i + 1) * per8, last8), 0)),
                  pl.BlockSpec((1, D), lambda i: (0, 0)),
                  pl.BlockSpec((1, N_MOD, D), lambda i: (lay.mod_row(i, tm), 0, 0)),
                  pl.BlockSpec(mu.shape, lambda i: (0, 0))]
                 + [pl.BlockSpec((D, n), lambda i: (0, 0)) for n in ranks],
        out_specs=[row_spec] * 3 + [pl.BlockSpec((tm, n), lambda i: (i, 0)) for n in ranks],
        out_shape=[jax.ShapeDtypeStruct((M, D), BF16)] * 3 + [jax.ShapeDtypeStruct((M, n), BF16) for n in ranks],
        compiler_params=_params(("parallel",)),
        name="rw_pre",
    )(h, h, h, g.reshape(1, D), mod, mu, *[w for _, w, _ in lora])


def _wkv_body(*refs, reverse, npairs, nsub, epilogue):
    if epilogue:
        (r_ref, k_ref, v_ref, wl_ref, al_ref, w0_ref, a0_ref, kkp_ref, ka_ref,
         alo_ref, a0o_ref, yo_ref, g_ref, rk_ref, lng_ref, lnb_ref, o_ref, s_ref) = refs
    else:
        r_ref, k_ref, v_ref, wl_ref, al_ref, w0_ref, a0_ref, kkp_ref, ka_ref, o_ref, s_ref = refs
    L = WKV_CHUNK
    H = RW_HEAD

    @pl.when(pl.program_id(2) == 0)
    def _():
        s_ref[...] = jnp.zeros_like(s_ref)

    def order(row, col):
        return (row <= col) if reverse else (row >= col)

    row = lax.broadcasted_iota(jnp.int32, (L, L), 0)
    col = lax.broadcasted_iota(jnp.int32, (L, L), 1)
    tri = jnp.where(order(row, col), 1.0, 0.0).astype(BF16)
    tri3 = jnp.concatenate([tri, tri, tri], axis=1)

    prow = lax.broadcasted_iota(jnp.int32, (L, 2 * L), 0)
    pcol = lax.broadcasted_iota(jnp.int32, (L, 2 * L), 1) & (L - 1)
    incl = order(prow, pcol)
    strict = incl & (prow != pcol)
    incl2 = jnp.concatenate([incl, incl], axis=1)
    eye = jnp.where(prow == pcol, 1.0, 0.0)

    def sibling(s):
        return ((prow // (2 * s)) == (pcol // (2 * s))) & ((prow // s) != (pcol // s))

    bmask = (lax.broadcasted_iota(jnp.int32, (2 * L, LANES), 0) // L
             == lax.broadcasted_iota(jnp.int32, (2 * L, LANES), 1) // H)
    head_ones = jnp.where(lax.broadcasted_iota(jnp.int32, (LANES, LANES), 0) // H
                          == lax.broadcasted_iota(jnp.int32, (LANES, LANES), 1) // H, 1.0, 0.0).astype(BF16)

    def bdf(x):
        return jnp.where(bmask, jnp.concatenate([x, x], axis=0), 0.0)

    def bd(x):
        return bdf(x).astype(BF16)

    def dot(a, b):
        return jnp.dot(a, b, preferred_element_type=F32)

    def cat0(*xs):
        return jnp.concatenate(xs, axis=0)

    def cat1(*xs):
        return jnp.concatenate(xs, axis=1)

    prs = range(npairs)
    state = [s_ref[p] for p in prs]

    def chunk_stages(ci):
        rows = slice(ci * L, (ci + 1) * L)

        def tiles(ref):
            return [ref[rows, p * LANES:(p + 1) * LANES] for p in prs]

        def vecs(ref):
            return [ref[:, p * LANES:(p + 1) * LANES] for p in prs]

        def head_sum(xs):
            tot = dot(cat0(*xs).astype(BF16), head_ones)
            return [tot[p * L:(p + 1) * L] for p in prs]

        def store(vals):
            for p, val in zip(prs, vals):
                o_ref[rows, p * LANES:(p + 1) * LANES] = val.astype(o_ref.dtype)

        r, k, v = tiles(r_ref), tiles(k_ref), tiles(v_ref)
        ka = vecs(ka_ref)

        a = [jax.nn.sigmoid(a0 + al) for a0, al in zip(vecs(a0_ref), tiles(al_ref))]
        z = [-(w0 + wl) for w0, wl in zip(vecs(w0_ref), tiles(wl_ref))]
        softplus = [jnp.maximum(x, 0.0) + jnp.log(1.0 + jnp.exp(-jnp.abs(x))) for x in z]
        lw = [-jnp.exp(-x - 0.5) for x in softplus]
        yield
        kraw = [k[p] * kkp for p, kkp in zip(prs, vecs(kkp_ref))]
        norm2 = head_sum([x * x for x in kraw])
        kd = [k[p] * (1.0 + (a[p] - 1.0) * ka[p]) for p in prs]
        hi = [x.astype(BF16) for x in lw]
        rem = [lw[p] - hi[p].astype(F32) for p in prs]
        mid = [x.astype(BF16) for x in rem]
        lo = [(rem[p] - mid[p].astype(F32)).astype(BF16) for p in prs]
        g = [dot(tri3, cat0(hi[p], mid[p], lo[p])) for p in prs]
        ee = [jnp.exp(jnp.sum(x, axis=0, keepdims=True)) for x in lw]
        yield
        kk = [kraw[p] / jnp.maximum(jnp.sqrt(norm2[p]), RW_L2_EPS) for p in prs]
        en = [jnp.exp(-x) for x in g]
        abar = [-(kk[p] * jnp.exp(g[p] - lw[p])) for p in prs]
        rbar = [r[p] * jnp.exp(g[p]) for p in prs]
        bt = [kk[p] * a[p] * en[p] for p in prs]
        kt = [kd[p] * en[p] for p in prs]
        yield
        amat = [lax.dot_general(cat0(abar[p], rbar[p]).astype(BF16), cat0(bd(bt[p]), bd(kt[p])), _NT,
                                preferred_element_type=F32) for p in prs]
        vbd = [bd(x) for x in v]
        gam = [jnp.broadcast_to(x, (LANES, LANES)).T for x in ee]
        upd_lhs = [cat1(bdf(bt[p] * ee[p]).T, bdf(kt[p] * ee[p]).T).astype(BF16) for p in prs]
        yield
        a_ab = [jnp.where(strict, x[:L, :2 * L], 0.0) for x in amat]
        a_ak = [jnp.where(strict, x[:L, 2 * L:], 0.0).astype(BF16) for x in amat]
        a_rbk = [jnp.where(incl2, x[L:], 0.0).astype(BF16) for x in amat]
        av = [dot(a_ak[p], vbd[p]) for p in prs]

        t = [eye + jnp.where(sibling(1), x, 0.0) for x in a_ab]
        s = 2
        while s < L:
            sib = sibling(s)
            a_l = [bd(jnp.where(sib, a_ab[p], 0.0)) for p in prs]
            if s < SUBLANES:
                half = [dot(t[p].astype(BF16), a_l[p]).astype(BF16) for p in prs]
                yield
                t = [t[p] + dot(half[p], bd(t[p])) for p in prs]
            else:
                blocks = range(L // s)
                moving = [b for b in blocks if (b % 2 == 1) != reverse]

                def pick(x, n):
                    return x[n * s:(n + 1) * s]

                half = [dot(cat0(*[pick(t[p], b) for b in moving]).astype(BF16), a_l[p]).astype(BF16) for p in prs]
                yield
                upd = [dot(half[p], bd(t[p])) for p in prs]
                t = [cat0(*[pick(t[p], b) + pick(upd[p], moving.index(b)) if b in moving else pick(t[p], b)
                            for b in blocks]) for p in prs]
            yield
            s *= 2

        wu = [dot(t[p].astype(BF16), cat1(bd(abar[p]), bd(av[p]))) for p in prs]
        yield
        wr_lhs = [cat0(wu[p][:, :LANES], rbar[p]).astype(BF16) for p in prs]
        ut = [x[:, LANES:] for x in wu]
        c2_lhs = [cat0(a_rbk[p], upd_lhs[p]) for p in prs]
        yield

        wr = [dot(wr_lhs[p], state[p].astype(BF16)) for p in prs]
        yield
        u = [wr[p][:L] + ut[p] for p in prs]
        out2 = [dot(c2_lhs[p], cat0(bd(u[p]), vbd[p])) for p in prs]
        yield
        y = [wr[p][L:] + out2[p][:L] for p in prs]
        for p in prs:
            state[p] = state[p] * gam[p] + out2[p][L:]
        if not epilogue:
            store(y)
            return
        yield

        inv_n = 1.0 / H
        ytot = [y[p] + yo for p, yo in zip(prs, tiles(yo_ref))]
        mean = head_sum(ytot)
        a_o = [jax.nn.sigmoid(a0 + al) for a0, al in zip(vecs(a0o_ref), tiles(alo_ref))]
        kd_sum = [kd[p] + k[p] * (1.0 + (a_o[p] - 1.0) * ka[p]) for p in prs]
        bonus = head_sum([r[p] * rk * kd_sum[p] for p, rk in zip(prs, vecs(rk_ref))])
        yield
        dev = [ytot[p] - mean[p] * inv_n for p in prs]
        var = head_sum([x * x for x in dev])
        yield
        lng, lnb, gate = vecs(lng_ref), vecs(lnb_ref), tiles(g_ref)
        store([(dev[p] * lax.rsqrt(var[p] * inv_n + RW_GN_EPS) * lng[p] + lnb[p] + bonus[p] * v[p]) * gate[p]
               for p in prs])

    waiting = [chunk_stages(ci) for ci in (reversed(range(nsub)) if reverse else range(nsub))]
    running = []
    step = 0
    while waiting or running:
        if waiting and step % WKV_STAGGER == 0:
            running.append(waiting.pop(0))
        for gen in list(running):
            if next(gen, "done") == "done":
                running.remove(gen)
        step += 1
    for p in prs:
        s_ref[p] = state[p]


def _wkv(r, k, v, wl, al, w0, a0, kkp, ka, lay, *, reverse, epilogue=None):
    M, D = r.shape
    L = WKV_CHUNK
    td = min(D, 8 * LANES)
    npairs = td // LANES
    nsub = _pick(math.gcd(lay.C, lay.S) // L, (4, 2, 1))
    tb = nsub * L
    ctx_blk = lay.C // tb
    lat_blk = lay.S // tb
    ctx_total = lay.n_ctx // tb

    def tok_block(b, c):
        if reverse:
            ctx_c = ctx_blk - 1 - c
            lat_c = lat_blk - 1 - (c - ctx_blk)
        else:
            ctx_c = c
            lat_c = c - ctx_blk
        return jnp.where(c < ctx_blk, b * ctx_blk + ctx_c, ctx_total + b * lat_blk + lat_c)

    mat = pl.BlockSpec((tb, td), lambda b, d, c: (tok_block(b, c), d))
    vec = pl.BlockSpec((1, td), lambda b, d, c: (0, d))
    row = lambda x: x.reshape(1, D)
    args = [r, k, v, wl, al, row(w0), row(a0), row(kkp), row(ka)]
    specs = [mat] * 5 + [vec] * 4
    if epilogue is not None:
        al_o, a0_o, y_o, gate, rk, lnx_g, lnx_b = epilogue
        args += [al_o, row(a0_o), y_o, gate, row(rk), row(lnx_g), row(lnx_b)]
        specs += [mat, vec, mat, mat, vec, vec, vec]
    return pl.pallas_call(
        functools.partial(_wkv_body, reverse=reverse, npairs=npairs, nsub=nsub, epilogue=epilogue is not None),
        grid=(lay.B, D // td, ctx_blk + lat_blk),
        in_specs=specs,
        out_specs=mat,
        out_shape=jax.ShapeDtypeStruct((M, D), F32 if epilogue is None else BF16),
        scratch_shapes=[pltpu.VMEM((npairs, LANES, LANES), F32)],
        compiler_params=_params(("parallel", "parallel", "arbitrary")),
        name="wkv_bwd" if reverse else "wkv_fwd",
    )(*args)


ATTN_KEY_CHUNK = 1024


def _attn_body(q_ref, *refs, n_kv):
    k_refs, vt_refs = refs[:n_kv], refs[n_kv:2 * n_kv]
    o_ref = refs[2 * n_kv]
    q = q_ref[...]
    pieces = []
    for k_ref, vt_ref in zip(k_refs, vt_refs):
        n = k_ref.shape[0]
        for lo in range(0, n, ATTN_KEY_CHUNK):
            pieces.append((k_ref, vt_ref, lo, min(lo + ATTN_KEY_CHUNK, n)))

    def scores(piece):
        k_ref, _, lo, hi = piece
        return lax.dot_general(k_ref[lo:hi, :], q, _NT, preferred_element_type=F32)

    s_next = scores(pieces[0])
    m = l = acc = None
    for c, (_, vt_ref, lo, hi) in enumerate(pieces):
        s = s_next
        if c + 1 < len(pieces):
            s_next = scores(pieces[c + 1])
        m_c = jnp.max(s, axis=0, keepdims=True)
        m_new = m_c if m is None else jnp.maximum(m, m_c)
        p = jnp.exp2(s - m_new)
        l_c = jnp.sum(p, axis=0, keepdims=True)
        pv = jnp.dot(vt_ref[:, lo:hi], p.astype(BF16), preferred_element_type=F32)
        if m is None:
            l, acc = l_c, pv
        else:
            alpha = jnp.exp2(m - m_new)
            l, acc = alpha * l + l_c, alpha * acc + pv
        m = m_new
    o_ref[...] = (acc / l).T.astype(o_ref.dtype)


def _attn(q, k, vt, heads, lay, *, q_row0, n_q_rows, with_lat_keys):
    dk = k.shape[1] // heads
    dv = vt.shape[0] // heads
    tq = _pick(n_q_rows, (2048, 1024, 512, 256, 128, 64, 32, 16, 8))
    per_b = n_q_rows // tq
    q0 = q_row0 // tq
    kv = [(lay.C, 0)]
    k_lat, vt_lat = k, vt
    if with_lat_keys:
        kv.append((lay.S, lay.n_ctx // lay.S))
        if lay.n_ctx % lay.S:
            k_lat, vt_lat, kv[1] = k[lay.n_ctx:], vt[:, lay.n_ctx:], (lay.S, 0)
    in_specs = [pl.BlockSpec((tq, dk), lambda b, h, i: (q0 + b * per_b + i, h))]
    in_specs += [pl.BlockSpec((n, dk), functools.partial(lambda b, h, i, base: (base + b, h), base=base))
                 for n, base in kv]
    in_specs += [pl.BlockSpec((dv, n), functools.partial(lambda b, h, i, base: (h, base + b), base=base))
                 for n, base in kv]
    operands = [q, k] + ([k_lat] if with_lat_keys else []) + [vt] + ([vt_lat] if with_lat_keys else [])
    return pl.pallas_call(
        functools.partial(_attn_body, n_kv=len(kv)),
        grid=(lay.B, heads, per_b),
        in_specs=in_specs,
        out_specs=pl.BlockSpec((tq, dv), lambda b, h, i: (b * per_b + i, h)),
        out_shape=jax.ShapeDtypeStruct((lay.B * n_q_rows, heads * dv), BF16),
        compiler_params=_params(("parallel", "parallel", "arbitrary")),
        name="attn",
    )(*operands)


def _rms_rows(x, gain):
    return x * lax.rsqrt(jnp.mean(x * x, axis=-1, keepdims=True) + NORM_EPS) * gain


def _rope_lanes(x, cos, sin):
    lane = lax.broadcasted_iota(jnp.int32, x.shape, 1)
    first = (lane % MLA_ROPE) < MLA_ROPE // 2
    partner = jnp.where(first, pltpu.roll(x, LANES - MLA_ROPE // 2, axis=1), pltpu.roll(x, MLA_ROPE // 2, axis=1))
    return x * cos + partner * sin


def _mla_down_body(h_ref, g_ref, mod_ref, wkv_ref, wq_ref, gkv_ref, gr_ref, gq_ref, cos_ref, sin_ref,
                   ckv_ref, kr_ref, cq_ref, *, kv_lora):
    u = _norm_mod_rows(h_ref[...], g_ref[...], mod_ref[0], 0, 1).astype(BF16)
    acc = jnp.dot(u, wkv_ref[...], preferred_element_type=F32)
    ckv_ref[...] = _rms_rows(acc[:, :kv_lora], gkv_ref[...]).astype(ckv_ref.dtype)
    kr = acc[:, kv_lora:]
    ms = jnp.sum(kr * kr, axis=-1, keepdims=True) * (1.0 / MLA_ROPE)
    kr = kr * lax.rsqrt(ms + NORM_EPS) * gr_ref[...]
    kr_ref[...] = _rope_lanes(kr, cos_ref[...], sin_ref[...]).astype(kr_ref.dtype)
    acc_q = jnp.dot(u, wq_ref[...], preferred_element_type=F32)
    cq_ref[...] = _rms_rows(acc_q, gq_ref[...]).astype(cq_ref.dtype)


def _rms_groups(x, gain, group):
    ones = jnp.where(lax.broadcasted_iota(jnp.int32, (LANES, LANES), 0) // group
                     == lax.broadcasted_iota(jnp.int32, (LANES, LANES), 1) // group, 1.0, 0.0).astype(BF16)
    ss = jnp.dot((x * x).astype(BF16), ones, preferred_element_type=F32)
    return x * lax.rsqrt(ss * (1.0 / group) + NORM_EPS) * gain


def _mla_ukv_body(c_ref, w_ref, g_ref, kr_ref, kcat_ref, vt_ref, *, heads_per_tile):
    acc = jnp.dot(c_ref[...], w_ref[...], preferred_element_type=F32)
    kr = kr_ref[...]
    dk = MLA_NOPE + LANES
    for h in range(heads_per_tile):
        base = h * (MLA_NOPE + MLA_V)
        kcat_ref[:, h * dk:h * dk + MLA_NOPE] = _rms_groups(acc[:, base:base + MLA_NOPE], g_ref[...],
                                                            MLA_NOPE).astype(kcat_ref.dtype)
        kcat_ref[:, h * dk + MLA_NOPE:(h + 1) * dk] = kr
        vt_ref[h * MLA_V:(h + 1) * MLA_V, :] = acc[:, base + MLA_NOPE:base + MLA_NOPE + MLA_V].T.astype(vt_ref.dtype)


def _mla_uq_body(c_ref, wn_ref, wr_ref, gn_ref, gr_ref, cos_ref, sin_ref, q_ref, *, heads_per_tile, qscale):
    tm = c_ref.shape[0]
    rb = _pick(tm, (256,))
    dk = MLA_NOPE + LANES
    low = lax.broadcasted_iota(jnp.int32, (rb, LANES), 1) < MLA_ROPE

    def project(r0):
        x = c_ref[r0:r0 + rb, :]
        return (jnp.dot(x, wn_ref[...], preferred_element_type=F32),
                jnp.dot(x, wr_ref[...], preferred_element_type=F32))

    def finish(r0, nope, rope):
        rows = slice(r0, r0 + rb)
        for b in range(heads_per_tile // 2):
            xr = _rms_groups(rope[:, b * LANES:(b + 1) * LANES], gr_ref[...], MLA_ROPE)
            xr = _rope_lanes(xr, cos_ref[rows, :], sin_ref[rows, :]) * qscale
            for t in range(2):
                h = 2 * b + t
                qn = _rms_groups(nope[:, h * MLA_NOPE:(h + 1) * MLA_NOPE], gn_ref[...], MLA_NOPE) * qscale
                q_ref[rows, h * dk:h * dk + MLA_NOPE] = qn.astype(q_ref.dtype)
                rr = xr if t == 0 else pltpu.roll(xr, MLA_ROPE, axis=1)
                q_ref[rows, h * dk + MLA_NOPE:(h + 1) * dk] = jnp.where(low, rr, 0.0).astype(q_ref.dtype)

    starts = list(range(0, tm, rb))
    ahead = project(starts[0])
    for n, r0 in enumerate(starts):
        cur = ahead
        if n + 1 < len(starts):
            ahead = project(starts[n + 1])
        finish(r0, *cur)


def _pad_cols(w, n):
    return jnp.pad(w, ((0, 0), (0, n - w.shape[1])))


def _pad_rows(w, n):
    return jnp.pad(w, ((0, n - w.shape[0]), (0, 0)))


def _up128(n):
    return -(-n // LANES) * LANES


def _rope_tables(lay):
    n = lay.S
    rows = n // GRID_W
    row = jnp.broadcast_to(jnp.arange(rows)[:, None], (rows, GRID_W)).reshape(-1)
    col = jnp.broadcast_to(jnp.arange(GRID_W)[None, :], (rows, GRID_W)).reshape(-1)
    n_freq = MLA_ROPE // 4
    inv = ROPE_THETA ** (-jnp.arange(n_freq, dtype=F32) / n_freq)
    ang = jnp.concatenate([row[:, None].astype(F32) * inv, col[:, None].astype(F32) * inv], axis=-1)
    cos, sin = jnp.cos(ang), jnp.sin(ang)
    reps = LANES // MLA_ROPE
    cos_l = jnp.tile(jnp.concatenate([cos, cos], axis=1), (lay.B, reps))
    sin_l = jnp.tile(jnp.concatenate([-sin, sin], axis=1), (lay.B, reps))
    cos_t = jnp.concatenate([jnp.ones((lay.n_ctx, LANES), F32), cos_l], axis=0)
    sin_t = jnp.concatenate([jnp.zeros((lay.n_ctx, LANES), F32), sin_l], axis=0)
    return cos_t, sin_t


def _lora_up_body(*refs, n):
    for x_ref, w_ref, o_ref in zip(refs[:n], refs[n:2 * n], refs[2 * n:]):
        o_ref[...] = jnp.dot(x_ref[...], w_ref[...], preferred_element_type=F32).astype(o_ref.dtype)


def _lora_up(items, lay):
    n = len(items)
    M = items[0][0].shape[0]
    N = items[0][2].shape[1]
    tm = lay.row_tile()
    tn = _pick(N, (1024, 512, 256, 128))
    x_specs = [pl.BlockSpec((tm, w.shape[0]), functools.partial(lambda j, i, blk: (i, blk), blk=blk))
               for _, blk, w in items]
    w_specs = [pl.BlockSpec((w.shape[0], tn), lambda j, i: (0, j)) for _, _, w in items]
    return pl.pallas_call(
        functools.partial(_lora_up_body, n=n),
        grid=(N // tn, M // tm),
        in_specs=x_specs + w_specs,
        out_specs=[pl.BlockSpec((tm, tn), lambda j, i: (i, j))] * n,
        out_shape=[jax.ShapeDtypeStruct((M, N), BF16)] * n,
        compiler_params=_params(("parallel", "parallel")),
        name="lora_up",
    )(*[x for x, _, _ in items], *[w for _, _, w in items])


def _rwkv_mixer(h, norm_g, mod, lay, v_first, j, big_w, pw, vres, rk, lnx_g, lnx_b):
    wr, wk, wv = big_w
    mu, w0, w1, w2, a0, a1, a2, g1, g2, kkp, ka = pw
    nw, na = _up128(w1.shape[2]), _up128(a1.shape[2])
    lora = [(MIX_W, jnp.concatenate([_pad_cols(w1[e], nw) for e in range(2)], axis=1).astype(BF16), "tanh"),
            (MIX_A, jnp.concatenate([_pad_cols(a1[e], na) for e in range(2)], axis=1).astype(BF16), None),
            (MIX_G, g1.astype(BF16), "sigmoid")]
    if vres is not None:
        v0, v1, v2 = vres
        nv = _up128(v1.shape[1])
        lora.append((MIX_V, _pad_cols(v1, nv).astype(BF16), None))
    xr, xk, xv, w_mid, a_mid, g_mid, *v_mid = _rw_pre(h, norm_g, mod, mu, lora, lay)
    mm = functools.partial(_mm, lay=lay)
    r = mm(xr, wr, layer=j, name="rw_r")
    k = mm(xk, wk, layer=j, name="rw_k")
    v = mm(xv, wv, layer=j, name="rw_v")
    if vres is not None:
        v = mm(v_mid[0], _pad_rows(v2, nv).astype(BF16), blend=(v, v_first, v0), name="rw_v2")
    g, wl0, wl1, al0, al1 = _lora_up(
        [(g_mid, 0, g2.astype(BF16))]
        + [(w_mid, e, _pad_rows(w2[e], nw).astype(BF16)) for e in range(2)]
        + [(a_mid, e, _pad_rows(a2[e], na).astype(BF16)) for e in range(2)], lay)
    wl, al = (wl0, wl1), (al0, al1)

    y_fwd = _wkv(r, k, v, wl[0], al[0], w0[0], a0[0], kkp, ka, lay, reverse=False)
    o = _wkv(r, k, v, wl[1], al[1], w0[1], a0[1], kkp, ka, lay, reverse=True,
             epilogue=(al[0], a0[0], y_fwd, g, rk.reshape(-1), lnx_g, lnx_b))
    return o, v


def _mla_mixer(h, norm_g, mod, lay, rope_tabs, wdq, qnorm, wuq, wdkv, kvnorm, wukv, qn_nope, qn_rope, kn_nope,
               kn_rope, need_ctx):
    M, D = h.shape
    heads = D // MLA_V
    kv_lora, q_lora = kvnorm.shape[0], qnorm.shape[0]
    cos_t, sin_t = rope_tabs
    tm = lay.row_tile()
    hpt = _pick(heads, (4, 2))
    dk = MLA_NOPE + LANES
    pad = LANES - MLA_ROPE
    perm = jnp.concatenate([jnp.arange(0, MLA_ROPE, 2), jnp.arange(1, MLA_ROPE, 2)])
    row = lambda t: t.reshape(1, -1)
    par1 = _params(("parallel",))
    par2 = _params(("parallel", "parallel"))

    w_dkv = jnp.concatenate([wdkv[:, :kv_lora], wdkv[:, kv_lora:][:, perm], jnp.zeros((D, pad), F32)], axis=1)
    tmd = min(tm, 512)
    c_kv, k_rope, c_q = pl.pallas_call(
        functools.partial(_mla_down_body, kv_lora=kv_lora),
        grid=(M // tmd,),
        in_specs=[pl.BlockSpec((tmd, D), lambda i: (i, 0)),
                  pl.BlockSpec((1, D), lambda i: (0, 0)),
                  pl.BlockSpec((1, N_MOD, D), lambda i: (lay.mod_row(i, tmd), 0, 0)),
                  pl.BlockSpec((D, kv_lora + LANES), lambda i: (0, 0)),
                  pl.BlockSpec((D, q_lora), lambda i: (0, 0)),
                  pl.BlockSpec((1, kv_lora), lambda i: (0, 0)),
                  pl.BlockSpec((1, LANES), lambda i: (0, 0)),
                  pl.BlockSpec((1, q_lora), lambda i: (0, 0)),
                  pl.BlockSpec((tmd, LANES), lambda i: (i, 0)),
                  pl.BlockSpec((tmd, LANES), lambda i: (i, 0))],
        out_specs=[pl.BlockSpec((tmd, kv_lora), lambda i: (i, 0)), pl.BlockSpec((tmd, LANES), lambda i: (i, 0)),
                   pl.BlockSpec((tmd, q_lora), lambda i: (i, 0))],
        out_shape=[jax.ShapeDtypeStruct((M, kv_lora), BF16), jax.ShapeDtypeStruct((M, LANES), BF16),
                   jax.ShapeDtypeStruct((M, q_lora), BF16)],
        compiler_params=par1, name="mla_down",
    )(h, row(norm_g), mod, w_dkv.astype(BF16), wdq.astype(BF16), row(kvnorm),
      row(jnp.pad(kn_rope[perm], (0, pad))), row(qnorm), cos_t, sin_t)

    k_cat, vt = pl.pallas_call(
        functools.partial(_mla_ukv_body, heads_per_tile=hpt),
        grid=(heads // hpt, M // tm),
        in_specs=[pl.BlockSpec((tm, kv_lora), lambda j, i: (i, 0)),
                  pl.BlockSpec((kv_lora, hpt * (MLA_NOPE + MLA_V)), lambda j, i: (0, j)),
                  pl.BlockSpec((1, MLA_NOPE), lambda j, i: (0, 0)),
                  pl.BlockSpec((tm, LANES), lambda j, i: (i, 0))],
        out_specs=[pl.BlockSpec((tm, hpt * dk), lambda j, i: (i, j)),
                   pl.BlockSpec((hpt * MLA_V, tm), lambda j, i: (j, i))],
        out_shape=[jax.ShapeDtypeStruct((M, heads * dk), BF16), jax.ShapeDtypeStruct((heads * MLA_V, M), BF16)],
        compiler_params=par2, name="mla_ukv",
    )(c_kv, wukv.astype(BF16), row(kn_nope), k_rope)

    q_rows = M if need_ctx else lay.B * lay.S
    r0 = (M - q_rows) // tm
    w3 = wuq.reshape(q_lora, heads, MLA_NOPE + MLA_ROPE)
    w_nope = w3[:, :, :MLA_NOPE].reshape(q_lora, heads * MLA_NOPE)
    w_rope = w3[:, :, MLA_NOPE:][:, :, perm].reshape(q_lora, heads * MLA_ROPE)
    qscale = math.log2(math.e) / math.sqrt(MLA_NOPE + MLA_ROPE)
    q_cat = pl.pallas_call(
        functools.partial(_mla_uq_body, heads_per_tile=hpt, qscale=qscale),
        grid=(heads // hpt, q_rows // tm),
        in_specs=[pl.BlockSpec((tm, q_lora), lambda j, i: (i + r0, 0)),
                  pl.BlockSpec((q_lora, hpt * MLA_NOPE), lambda j, i: (0, j)),
                  pl.BlockSpec((q_lora, hpt * MLA_ROPE), lambda j, i: (0, j)),
                  pl.BlockSpec((1, MLA_NOPE), lambda j, i: (0, 0)),
                  pl.BlockSpec((1, LANES), lambda j, i: (0, 0)),
                  pl.BlockSpec((tm, LANES), lambda j, i: (i + r0, 0)),
                  pl.BlockSpec((tm, LANES), lambda j, i: (i + r0, 0))],
        out_specs=pl.BlockSpec((tm, hpt * dk), lambda j, i: (i, j)),
        out_shape=jax.ShapeDtypeStruct((q_rows, heads * dk), BF16),
        compiler_params=par2, name="mla_uq",
    )(c_q, w_nope.astype(BF16), w_rope.astype(BF16), row(qn_nope), row(jnp.tile(qn_rope[perm], LANES // MLA_ROPE)),
      cos_t, sin_t)

    o_lat = _attn(q_cat, k_cat, vt, heads, lay, q_row0=q_rows - lay.B * lay.S, n_q_rows=lay.S, with_lat_keys=True)
    if not need_ctx:
        return o_lat
    o_ctx = _attn(q_cat, k_cat, vt, heads, lay, q_row0=0, n_q_rows=lay.C, with_lat_keys=False)
    return jnp.concatenate([o_ctx, o_lat], axis=0)


def kernel(x, c, ctx, c_ctx, mod_w, mod_b, norm_g, mlp_w1, mlp_w2, rw_mu, rw_wr, rw_wk, rw_wv, rw_wo, rw_w0, rw_w1, rw_w2, rw_a0, rw_a1, rw_a2, rw_g1, rw_g2, rw_kk, rw_ka, rw_rk, rw_lnx_g, rw_lnx_b, rw_v0, rw_v1, rw_v2, mla_wdq, mla_qnorm, mla_wuq, mla_wdkv, mla_kvnorm, mla_wukv, mla_qn_nope, mla_qn_rope, mla_kn_nope, mla_kn_rope, mla_wo):
    B, S, D = x.shape
    C = ctx.shape[1]
    depth = mod_w.shape[0]
    lay = _Layout(B, C, S)
    rope = _rope_tables(lay)

    sc_all = jnp.concatenate([jax.nn.silu(c_ctx)[None], jax.nn.silu(c)], axis=0)
    h = jnp.concatenate([ctx.reshape(B * C, D), x.reshape(B * S, D)], axis=0)
    v_first = None
    mlp_w2_bf16 = mlp_w2.astype(BF16)
    rw_wo_bf16, mla_wo_bf16 = rw_wo.astype(BF16), mla_wo.astype(BF16)

    for i in range(depth):
        last = i == depth - 1
        j = i // 2
        mod = (_mm(sc_all, mod_w, layer=i, name="adaln") + mod_b[i]).reshape(B + 1, N_MOD, D)
        if i % 2 == 0:
            pw = (rw_mu[j], rw_w0[j], rw_w1[j], rw_w2[j],
                  rw_a0[j], rw_a1[j], rw_a2[j], rw_g1[j], rw_g2[j], rw_kk[j], rw_ka[j])
            vres = None if j == 0 else (rw_v0[j - 1], rw_v1[j - 1], rw_v2[j - 1])
            o, v_cur = _rwkv_mixer(h, norm_g[i, 0], mod, lay, v_first, j, (rw_wr, rw_wk, rw_wv), pw, vres,
                                   rw_rk[j], rw_lnx_g[j], rw_lnx_b[j])
            if j == 0:
                v_first = v_cur
            wo = rw_wo_bf16
        else:
            o = _mla_mixer(h, norm_g[i, 0], mod, lay, rope, mla_wdq[j], mla_qnorm[j], mla_wuq[j], mla_wdkv[j], mla_kvnorm[j],
                           mla_wukv[j], mla_qn_nope[j], mla_qn_rope[j], mla_kn_nope[j], mla_kn_rope[j],
                           need_ctx=not last)
            wo = mla_wo_bf16
        res_row0 = 0
        if last:
            if o.shape[0] != B * S:
                o = o[lay.n_ctx:]
            res_row0 = lay.n_ctx
            lay = _Layout(B, C, S, with_ctx=False)
        h, u2 = _mix_out(o, wo, j, h, res_row0, norm_g[i, 1], mod, lay)
        hid = _mm(u2, mlp_w1, layer=i, act="relu2", out_dtype=BF16, lay=lay, name="mlp_up")
        h = _mm(hid, mlp_w2_bf16, layer=i, res=h, gate=mod[:, 5], lay=lay, name="mlp_down")
    return h.reshape(B, S, D)
```

```python
import functools
import math

import jax
import jax.numpy as jnp
from jax import lax
from jax.experimental import pallas as pl
from jax.experimental.pallas import tpu as pltpu

F32 = jnp.float32
BF16 = jnp.bfloat16

NORM_EPS = 1e-6
N_MOD = 6
GRID_W = 64
RW_HEAD = 64
RW_GN_EPS = 64e-5
RW_L2_EPS = 1e-12
MLA_NOPE = 128
MLA_ROPE = 64
MLA_V = 128
ROPE_THETA = 10000.0

LANES = 128
SUBLANES = 8
BF16_ROWS = 16
WKV_CHUNK = 64
WKV_STAGGER = 3
VMEM_LIMIT_BYTES = 56 * 1024 * 1024

_NT = (((1,), (1,)), ((), ()))


def _pick(n, prefs):
    for p in prefs:
        if n % p == 0:
            return p
    return n


def _params(sem):
    return pltpu.CompilerParams(dimension_semantics=sem, vmem_limit_bytes=VMEM_LIMIT_BYTES)


class _Layout:
    def __init__(self, B, C, S, with_ctx=True):
        self.B, self.C, self.S = B, C, S
        self.n_ctx = B * C if with_ctx else 0
        self.M = self.n_ctx + B * S

    def row_tile(self):
        return _pick(math.gcd(self.n_ctx, self.S) if self.n_ctx else self.S, (1024, 512, 256, 128, 64, 32, 16, 8))

    def seq_tile(self):
        return _pick(math.gcd(self.C, self.S) if self.n_ctx else self.S, (256, 128, 64, 32, 16, 8))

    def mod_row(self, i, tm):
        n_ctx_tiles = self.n_ctx // tm
        per_b = self.S // tm
        lat = 1 + (i - n_ctx_tiles) // per_b
        if n_ctx_tiles == 0:
            return lat
        return jnp.where(i < n_ctx_tiles, 0, lat)


def _mm_body(*refs, nk, act, has_gate, has_blend, cache_w):
    if has_gate:
        x_ref, w_ref, res_ref, gate_ref, o_ref, *scratch = refs
    elif has_blend:
        x_ref, w_ref, cur_ref, first_ref, bias_ref, o_ref, *scratch = refs
    else:
        x_ref, w_ref, o_ref, *scratch = refs

    def finish(acc):
        if act == "relu2":
            acc = jnp.square(jnp.maximum(acc, 0.0))
        elif act == "sigmoid":
            acc = jax.nn.sigmoid(acc)
        elif act == "tanh":
            acc = jnp.tanh(acc)
        if has_gate:
            acc = res_ref[...] + gate_ref[0] * acc
        if has_blend:
            cur = cur_ref[...]
            acc = cur + (first_ref[...] - cur) * jax.nn.sigmoid(bias_ref[...] + acc)
        o_ref[...] = acc.astype(o_ref.dtype)

    if cache_w:
        wc_ref = scratch[-1]

        @pl.when(pl.program_id(1) == 0)
        def _():
            wc_ref[...] = w_ref[...].astype(BF16)

        w = wc_ref[...]
    else:
        w = w_ref[...].astype(BF16)
    part = jnp.dot(x_ref[...].astype(BF16), w, preferred_element_type=F32)
    if nk == 1:
        finish(part)
    else:
        acc_ref = scratch[0]
        k = pl.program_id(2)

        @pl.when(k == 0)
        def _():
            acc_ref[...] = part

        @pl.when(k > 0)
        def _():
            acc_ref[...] += part

        @pl.when(k == nk - 1)
        def _():
            finish(acc_ref[...])


def _mm(x, w, *, layer=None, act=None, out_dtype=F32, res=None, gate=None, blend=None, lay=None,
        tm=None, x_kblock=0, name="mm"):
    M = x.shape[0]
    K, N = w.shape[-2:]
    if tm is None:
        tm = lay.row_tile() if lay is not None else _pick(M, (1024, 512, 256, 128, 64, 32, 16, 8))
    tn = _pick(N, (1024, 512, 256, 128))
    tk = K if K <= 2048 else _pick(K, (2048, 1024, 512))
    if K > 2048 and w.dtype == BF16 and tm % 512 == 0 and tn % 512 == 0:
        tm, tn, tk = 256, 1024, K
    nk = K // tk
    has_gate = gate is not None
    has_blend = blend is not None
    cache_w = w.dtype == F32 and nk == 1 and M // tm > 1
    if w.ndim == 3:
        w_spec = pl.BlockSpec((None, tk, tn), lambda j, i, k: (layer, k, j))
    else:
        w_spec = pl.BlockSpec((tk, tn), lambda j, i, k: (k, j))
    in_specs = [pl.BlockSpec((tm, tk), lambda j, i, k: (i, k + x_kblock * nk)), w_spec]
    args = [x, w]
    if has_gate:
        in_specs += [pl.BlockSpec((tm, tn), lambda j, i, k: (i, j)),
                     pl.BlockSpec((1, 1, tn), lambda j, i, k: (lay.mod_row(i, tm), 0, j))]
        args += [res, gate.reshape(gate.shape[0], 1, N)]
    if has_blend:
        in_specs += [pl.BlockSpec((tm, tn), lambda j, i, k: (i, j)), pl.BlockSpec((tm, tn), lambda j, i, k: (i, j)),
                     pl.BlockSpec((1, tn), lambda j, i, k: (0, j))]
        args += [blend[0], blend[1], blend[2].reshape(1, N)]
    scratch = [pltpu.VMEM((tm, tn), F32)] if nk > 1 else []
    if cache_w:
        scratch.append(pltpu.VMEM((tk, tn), BF16))
    return pl.pallas_call(
        functools.partial(_mm_body, nk=nk, act=act, has_gate=has_gate, has_blend=has_blend, cache_w=cache_w),
        grid=(N // tn, M // tm, nk),
        in_specs=in_specs,
        out_specs=pl.BlockSpec((tm, tn), lambda j, i, k: (i, j)),
        out_shape=jax.ShapeDtypeStruct((M, N), out_dtype),
        scratch_shapes=scratch,
        compiler_params=_params(("parallel", "arbitrary", "arbitrary")),
        name=name,
    )(*args)


def _norm_mod_rows(x, gain, mod, shift_row, scale_row):
    ms = jnp.mean(x * x, axis=-1, keepdims=True)
    y = x * lax.rsqrt(ms + NORM_EPS) * gain
    return y * (1.0 + mod[scale_row:scale_row + 1]) + mod[shift_row:shift_row + 1]


def _mix_out_body(o_ref, w_ref, res_ref, g_ref, mod_ref, h_ref, u_ref):
    mod = mod_ref[0]
    tm = o_ref.shape[0]
    rb = _pick(tm, (256,))
    starts = list(range(0, tm, rb))

    def project(r0):
        return jnp.dot(o_ref[r0:r0 + rb, :], w_ref[...], preferred_element_type=F32)

    ahead = project(starts[0])
    for n, r0 in enumerate(starts):
        acc = ahead
        if n + 1 < len(starts):
            ahead = project(starts[n + 1])
        rows = slice(r0, r0 + rb)
        h = res_ref[rows, :] + mod[2:3] * acc
        h_ref[rows, :] = h
        u_ref[rows, :] = _norm_mod_rows(h, g_ref[...], mod, 3, 4).astype(u_ref.dtype)


def _mix_out(o, w, layer, res, res_row0, g, mod, lay):
    M, D = o.shape
    tm = min(lay.row_tile(), 512)
    r0 = res_row0 // tm
    row_spec = pl.BlockSpec((tm, D), lambda i: (i, 0))
    return pl.pallas_call(
        _mix_out_body,
        grid=(M // tm,),
        in_specs=[row_spec,
                  pl.BlockSpec((None, D, D), lambda i: (layer, 0, 0)),
                  pl.BlockSpec((tm, D), lambda i: (i + r0, 0)),
                  pl.BlockSpec((1, D), lambda i: (0, 0)),
                  pl.BlockSpec((1, N_MOD, D), lambda i: (lay.mod_row(i, tm), 0, 0))],
        out_specs=[row_spec, row_spec],
        out_shape=[jax.ShapeDtypeStruct((M, D), F32), jax.ShapeDtypeStruct((M, D), BF16)],
        compiler_params=_params(("arbitrary",)),
        name="mix_out",
    )(o, w, res, g.reshape(1, D), mod)


MIX_R, MIX_W, MIX_K, MIX_V, MIX_A, MIX_G = range(6)


def _rw_pre_body(*refs, tm, n_ctx_tiles, ctx_tiles, lat_tiles, lora_mix, lora_act):
    n_lora = len(lora_mix)
    h_ref, hp_ref, hn_ref, g_ref, mod_ref, mu_ref = refs[:6]
    lora_refs = refs[6:6 + n_lora]
    xr_ref, xk_ref, xv_ref = refs[6 + n_lora:9 + n_lora]
    mid_refs = refs[9 + n_lora:]
    i = pl.program_id(0)
    D = h_ref.shape[1]
    slabs = [slice(c, c + LANES) for c in range(0, D, LANES)]
    above = slice(SUBLANES - 1, SUBLANES)
    below = slice(0, 1)

    def inv_rms(ref, rows):
        sq = None
        for cs in slabs:
            x = ref[rows, cs]
            sq = x * x if sq is None else sq + x * x
        return lax.rsqrt(jnp.sum(sq, axis=-1, keepdims=True) * (1.0 / D) + NORM_EPS)

    r_tile, r_above, r_below = inv_rms(h_ref, slice(None)), inv_rms(hp_ref, above), inv_rms(hn_ref, below)
    in_ctx = i < n_ctx_tiles
    pos = jnp.where(in_ctx, i % ctx_tiles, (i - n_ctx_tiles) % lat_tiles)
    last = jnp.where(in_ctx, ctx_tiles - 1, lat_tiles - 1)
    rows = lax.broadcasted_iota(jnp.int32, (tm, 1), 0)
    acc = [None] * n_lora
    for cs in slabs:
        gain, shift, scale1 = g_ref[:, cs], mod_ref[0, 0:1, cs], 1.0 + mod_ref[0, 1:2, cs]
        u = h_ref[:, cs] * r_tile * gain * scale1 + shift
        u_above = jnp.where(pos == 0, 0.0, hp_ref[above, cs] * r_above * gain * scale1 + shift)
        u_below = jnp.where(pos == last, 0.0, hn_ref[below, cs] * r_below * gain * scale1 + shift)
        prev = jnp.where(rows == 0, u_above, pltpu.roll(u, 1, axis=0))
        nxt = jnp.where(rows == tm - 1, u_below, pltpu.roll(u, tm - 1, axis=0))
        xx = 0.5 * (prev + nxt) - u

        def mix(j):
            return (u + xx * mu_ref[j:j + 1, cs]).astype(BF16)

        xv = mix(MIX_V)
        xr_ref[:, cs] = mix(MIX_R)
        xk_ref[:, cs] = mix(MIX_K)
        xv_ref[:, cs] = xv
        for n, j in enumerate(lora_mix):
            part = jnp.dot(xv if j == MIX_V else mix(j), lora_refs[n][cs, :], preferred_element_type=F32)
            acc[n] = part if acc[n] is None else acc[n] + part
    for n, act in enumerate(lora_act):
        val = acc[n]
        if act == "tanh":
            val = jnp.tanh(val)
        elif act == "sigmoid":
            val = jax.nn.sigmoid(val)
        mid_refs[n][...] = val.astype(mid_refs[n].dtype)


def _rw_pre(h, g, mod, mu, lora, lay):
    M, D = h.shape
    tm = lay.seq_tile()
    per8 = tm // SUBLANES
    last8 = M // SUBLANES - 1
    ranks = [w.shape[1] for _, w, _ in lora]
    row_spec = pl.BlockSpec((tm, D), lambda i: (i, 0))
    return pl.pallas_call(
        functools.partial(_rw_pre_body, tm=tm, n_ctx_tiles=lay.n_ctx // tm, ctx_tiles=max(lay.C // tm, 1),
                          lat_tiles=lay.S // tm, lora_mix=tuple(j for j, _, _ in lora),
                          lora_act=tuple(a for _, _, a in lora)),
        grid=(M // tm,),
        in_specs=[row_spec,
                  pl.BlockSpec((SUBLANES, D), lambda i: (jnp.maximum(i * per8 - 1, 0), 0)),
                  pl.BlockSpec((SUBLANES, D), lambda i: (jnp.minimum((i + 1) * per8, last8), 0)),
                  pl.BlockSpec((1, D), lambda i: (0, 0)),
                  pl.BlockSpec((1, N_MOD, D), lambda i: (lay.mod_row(i, tm), 0, 0)),
                  pl.BlockSpec(mu.shape, lambda i: (0, 0))]
                 + [pl.BlockSpec((D, n), lambda i: (0, 0)) for n in ranks],
        out_specs=[row_spec] * 3 + [pl.BlockSpec((tm, n), lambda i: (i, 0)) for n in ranks],
        out_shape=[jax.ShapeDtypeStruct((M, D), BF16)] * 3 + [jax.ShapeDtypeStruct((M, n), BF16) for n in ranks],
        compiler_params=_params(("parallel",)),
        name="rw_pre",
    )(h, h, h, g.reshape(1, D), mod, mu, *[w for _, w, _ in lora])


def _wkv_body(*refs, reverse, npairs, nsub, epilogue):
    if epilogue:
        (r_ref, k_ref, v_ref, wl_ref, al_ref, w0_ref, a0_ref, kkp_ref, ka_ref,
         alo_ref, a0o_ref, yo_ref, g_ref, rk_ref, lng_ref, lnb_ref, o_ref, s_ref) = refs
    else:
        r_ref, k_ref, v_ref, wl_ref, al_ref, w0_ref, a0_ref, kkp_ref, ka_ref, o_ref, s_ref = refs
    L = WKV_CHUNK
    H = RW_HEAD

    @pl.when(pl.program_id(2) == 0)
    def _():
        s_ref[...] = jnp.zeros_like(s_ref)

    def order(row, col):
        return (row <= col) if reverse else (row >= col)

    row = lax.broadcasted_iota(jnp.int32, (L, L), 0)
    col = lax.broadcasted_iota(jnp.int32, (L, L), 1)
    tri = jnp.where(order(row, col), 1.0, 0.0).astype(BF16)
    tri2 = jnp.concatenate([tri, tri], axis=1)

    prow = lax.broadcasted_iota(jnp.int32, (L, 2 * L), 0)
    pcol = lax.broadcasted_iota(jnp.int32, (L, 2 * L), 1) & (L - 1)
    incl = order(prow, pcol)
    strict = incl & (prow != pcol)
    incl2 = jnp.concatenate([incl, incl], axis=1)
    eye = jnp.where(prow == pcol, 1.0, 0.0)

    def sibling(s):
        return ((prow // (2 * s)) == (pcol // (2 * s))) & ((prow // s) != (pcol // s))

    bmask = (lax.broadcasted_iota(jnp.int32, (2 * L, LANES), 0) // L
             == lax.broadcasted_iota(jnp.int32, (2 * L, LANES), 1) // H)
    head_ones = jnp.where(lax.broadcasted_iota(jnp.int32, (LANES, LANES), 0) // H
                          == lax.broadcasted_iota(jnp.int32, (LANES, LANES), 1) // H, 1.0, 0.0).astype(BF16)

    def bdf(x):
        return jnp.where(bmask, jnp.concatenate([x, x], axis=0), 0.0)

    def bd(x):
        return bdf(x).astype(BF16)

    def dot(a, b):
        return jnp.dot(a, b, preferred_element_type=F32)

    sigmoid = jax.nn.sigmoid

    def cat0(*xs):
        return jnp.concatenate(xs, axis=0)

    def cat1(*xs):
        return jnp.concatenate(xs, axis=1)

    prs = range(npairs)
    state = [s_ref[p] for p in prs]

    def chunk_stages(ci):
        rows = slice(ci * L, (ci + 1) * L)

        def tiles(ref):
            return [ref[rows, p * LANES:(p + 1) * LANES] for p in prs]

        def vecs(ref):
            return [ref[:, p * LANES:(p + 1) * LANES] for p in prs]

        def head_sum(xs):
            tot = dot(cat0(*xs).astype(BF16), head_ones)
            return [tot[p * L:(p + 1) * L] for p in prs]

        def store(vals):
            for p, val in zip(prs, vals):
                o_ref[rows, p * LANES:(p + 1) * LANES] = val.astype(o_ref.dtype)

        r, k, v = tiles(r_ref), tiles(k_ref), tiles(v_ref)
        ka = vecs(ka_ref)

        a = [sigmoid(a0 + al) for a0, al in zip(vecs(a0_ref), tiles(al_ref))]
        lw = [-math.exp(-0.5) * sigmoid(w0 + wl) for w0, wl in zip(vecs(w0_ref), tiles(wl_ref))]
        yield
        kraw = [k[p] * kkp for p, kkp in zip(prs, vecs(kkp_ref))]
        norm2 = head_sum([x * x for x in kraw])
        kd = [k[p] * (1.0 + (a[p] - 1.0) * ka[p]) for p in prs]
        hi = [x.astype(BF16) for x in lw]
        lo = [(lw[p] - hi[p].astype(F32)).astype(BF16) for p in prs]
        g = [dot(tri2, cat0(hi[p], lo[p])) for p in prs]
        ee = [jnp.exp(jnp.sum(x, axis=0, keepdims=True)) for x in lw]
        yield
        kk = [kraw[p] * lax.rsqrt(jnp.maximum(norm2[p], RW_L2_EPS * RW_L2_EPS)) for p in prs]
        en = [jnp.exp(-x) for x in g]
        abar = [-(kk[p] * jnp.exp(g[p] - lw[p])) for p in prs]
        rbar = [r[p] * jnp.exp(g[p]) for p in prs]
        bt = [kk[p] * a[p] * en[p] for p in prs]
        kt = [kd[p] * en[p] for p in prs]
        yield
        amat = [lax.dot_general(cat0(abar[p], rbar[p]).astype(BF16), cat0(bd(bt[p]), bd(kt[p])), _NT,
                                preferred_element_type=F32) for p in prs]
        vbd = [bd(x) for x in v]
        gam = [jnp.broadcast_to(x, (LANES, LANES)).T for x in ee]
        upd_lhs = [cat1(bdf(bt[p] * ee[p]).T, bdf(kt[p] * ee[p]).T).astype(BF16) for p in prs]
        yield
        a_ab = [jnp.where(strict, x[:L, :2 * L], 0.0) for x in amat]
        a_ak = [jnp.where(strict, x[:L, 2 * L:], 0.0).astype(BF16) for x in amat]
        a_rbk = [jnp.where(incl2, x[L:], 0.0).astype(BF16) for x in amat]
        av = [dot(a_ak[p], vbd[p]) for p in prs]

        t = [eye + jnp.where(sibling(1), x, 0.0) for x in a_ab]
        s = 2
        while s < L:
            sib = sibling(s)
            a_l = [bd(jnp.where(sib, a_ab[p], 0.0)) for p in prs]
            if s < SUBLANES:
                half = [dot(t[p].astype(BF16), a_l[p]).astype(BF16) for p in prs]
                yield
                t = [t[p] + dot(half[p], bd(t[p])) for p in prs]
            else:
                blocks = range(L // s)
                moving = [b for b in blocks if (b % 2 == 1) != reverse]

                def pick(x, n):
                    return x[n * s:(n + 1) * s]

                half = [dot(cat0(*[pick(t[p], b) for b in moving]).astype(BF16), a_l[p]).astype(BF16) for p in prs]
                yield
                upd = [dot(half[p], bd(t[p])) for p in prs]
                t = [cat0(*[pick(t[p], b) + pick(upd[p], moving.index(b)) if b in moving else pick(t[p], b)
                            for b in blocks]) for p in prs]
            yield
            s *= 2

        wu = [dot(t[p].astype(BF16), cat1(bd(abar[p]), bd(av[p]))) for p in prs]
        yield
        wr_lhs = [cat0(wu[p][:, :LANES], rbar[p]).astype(BF16) for p in prs]
        ut = [x[:, LANES:] for x in wu]
        c2_lhs = [cat0(a_rbk[p], upd_lhs[p]) for p in prs]
        yield

        wr = [dot(wr_lhs[p], state[p].astype(BF16)) for p in prs]
        yield
        u = [wr[p][:L] + ut[p] for p in prs]
        out2 = [dot(c2_lhs[p], cat0(bd(u[p]), vbd[p])) for p in prs]
        yield
        y = [wr[p][L:] + out2[p][:L] for p in prs]
        for p in prs:
            state[p] = state[p] * gam[p] + out2[p][L:]
        if not epilogue:
            store(y)
            return
        yield

        inv_n = 1.0 / H
        ytot = [y[p] + yo for p, yo in zip(prs, tiles(yo_ref))]
        mean = head_sum(ytot)
        a_o = [sigmoid(a0 + al) for a0, al in zip(vecs(a0o_ref), tiles(alo_ref))]
        kd_sum = [kd[p] + k[p] * (1.0 + (a_o[p] - 1.0) * ka[p]) for p in prs]
        bonus = head_sum([r[p] * rk * kd_sum[p] for p, rk in zip(prs, vecs(rk_ref))])
        yield
        dev = [ytot[p] - mean[p] * inv_n for p in prs]
        var = head_sum([x * x for x in dev])
        yield
        lng, lnb, gate = vecs(lng_ref), vecs(lnb_ref), tiles(g_ref)
        store([(dev[p] * lax.rsqrt(var[p] * inv_n + RW_GN_EPS) * lng[p] + lnb[p] + bonus[p] * v[p]) * gate[p]
               for p in prs])

    waiting = [chunk_stages(ci) for ci in (reversed(range(nsub)) if reverse else range(nsub))]
    running = []
    step = 0
    while waiting or running:
        if waiting and step % WKV_STAGGER == 0:
            running.append(waiting.pop(0))
        for gen in list(running):
            if next(gen, "done") == "done":
                running.remove(gen)
        step += 1
    for p in prs:
        s_ref[p] = state[p]


def _wkv(r, k, v, wl, al, w0, a0, kkp, ka, lay, *, reverse, epilogue=None):
    M, D = r.shape
    L = WKV_CHUNK
    td = min(D, 8 * LANES)
    npairs = td // LANES
    nsub = _pick(math.gcd(lay.C, lay.S) // L, (4, 2, 1))
    tb = nsub * L
    ctx_blk = lay.C // tb
    lat_blk = lay.S // tb
    ctx_total = lay.n_ctx // tb

    def tok_block(b, c):
        if reverse:
            ctx_c = ctx_blk - 1 - c
            lat_c = lat_blk - 1 - (c - ctx_blk)
        else:
            ctx_c = c
            lat_c = c - ctx_blk
        return jnp.where(c < ctx_blk, b * ctx_blk + ctx_c, ctx_total + b * lat_blk + lat_c)

    mat = pl.BlockSpec((tb, td), lambda b, d, c: (tok_block(b, c), d))
    vec = pl.BlockSpec((1, td), lambda b, d, c: (0, d))
    row = lambda x: x.reshape(1, D)
    args = [r, k, v, wl, al, row(w0), row(a0), row(kkp), row(ka)]
    specs = [mat] * 5 + [vec] * 4
    if epilogue is not None:
        al_o, a0_o, y_o, gate, rk, lnx_g, lnx_b = epilogue
        args += [al_o, row(a0_o), y_o, gate, row(rk), row(lnx_g), row(lnx_b)]
        specs += [mat, vec, mat, mat, vec, vec, vec]
    return pl.pallas_call(
        functools.partial(_wkv_body, reverse=reverse, npairs=npairs, nsub=nsub, epilogue=epilogue is not None),
        grid=(lay.B, D // td, ctx_blk + lat_blk),
        in_specs=specs,
        out_specs=mat,
        out_shape=jax.ShapeDtypeStruct((M, D), F32 if epilogue is None else BF16),
        scratch_shapes=[pltpu.VMEM((npairs, LANES, LANES), F32)],
        compiler_params=_params(("parallel", "parallel", "arbitrary")),
        name="wkv_bwd" if reverse else "wkv_fwd",
    )(*args)


ATTN_KEY_CHUNK = 1024


def _attn_body(q_ref, *refs, n_kv):
    k_refs, vt_refs = refs[:n_kv], refs[n_kv:2 * n_kv]
    o_ref = refs[2 * n_kv]
    q = q_ref[...]
    dv = vt_refs[0].shape[0]
    pieces = []
    for k_ref, vt_ref in zip(k_refs, vt_refs):
        n = k_ref.shape[0]
        for lo in range(0, n, ATTN_KEY_CHUNK):
            pieces.append((k_ref, vt_ref, lo, min(lo + ATTN_KEY_CHUNK, n)))

    def scores(piece):
        k_ref, _, lo, hi = piece
        return lax.dot_general(k_ref[lo:hi, :], q, _NT, preferred_element_type=F32)

    s_next = scores(pieces[0])
    m = l = acc = None
    for c, (_, vt_ref, lo, hi) in enumerate(pieces):
        s = s_next
        if c + 1 < len(pieces):
            s_next = scores(pieces[c + 1])
        m_c = jnp.max(s, axis=0, keepdims=True)
        m_new = m_c if m is None else jnp.maximum(m, m_c)
        p = jnp.exp2(s - m_new).astype(BF16)
        lhs = jnp.concatenate([vt_ref[:, lo:hi], jnp.ones((BF16_ROWS, hi - lo), BF16)], axis=0)
        pv = jnp.dot(lhs, p, preferred_element_type=F32)
        l_c, pv = pv[dv:dv + 1], pv[:dv]
        if m is None:
            l, acc = l_c, pv
        else:
            alpha = jnp.exp2(m - m_new)
            l, acc = alpha * l + l_c, alpha * acc + pv
        m = m_new
    o_ref[...] = (acc / l).T.astype(o_ref.dtype)


def _attn(q, k, vt, heads, lay, *, q_row0, n_q_rows, with_lat_keys):
    dk = k.shape[1] // heads
    dv = vt.shape[0] // heads
    tq = _pick(n_q_rows, (2048, 1024, 512, 256, 128, 64, 32, 16, 8))
    per_b = n_q_rows // tq
    q0 = q_row0 // tq
    kv = [(lay.C, 0)]
    k_lat, vt_lat = k, vt
    if with_lat_keys:
        kv.append((lay.S, lay.n_ctx // lay.S))
        if lay.n_ctx % lay.S:
            k_lat, vt_lat, kv[1] = k[lay.n_ctx:], vt[:, lay.n_ctx:], (lay.S, 0)
    in_specs = [pl.BlockSpec((tq, dk), lambda b, h, i: (q0 + b * per_b + i, h))]
    in_specs += [pl.BlockSpec((n, dk), functools.partial(lambda b, h, i, base: (base + b, h), base=base))
                 for n, base in kv]
    in_specs += [pl.BlockSpec((dv, n), functools.partial(lambda b, h, i, base: (h, base + b), base=base))
                 for n, base in kv]
    operands = [q, k] + ([k_lat] if with_lat_keys else []) + [vt] + ([vt_lat] if with_lat_keys else [])
    return pl.pallas_call(
        functools.partial(_attn_body, n_kv=len(kv)),
        grid=(lay.B, heads, per_b),
        in_specs=in_specs,
        out_specs=pl.BlockSpec((tq, dv), lambda b, h, i: (b * per_b + i, h)),
        out_shape=jax.ShapeDtypeStruct((lay.B * n_q_rows, heads * dv), BF16),
        compiler_params=_params(("parallel", "parallel", "arbitrary")),
        name="attn",
    )(*operands)


def _rms_rows(x, gain):
    return x * lax.rsqrt(jnp.mean(x * x, axis=-1, keepdims=True) + NORM_EPS) * gain


def _rope_lanes(x, cos, sin):
    lane = lax.broadcasted_iota(jnp.int32, x.shape, 1)
    first = (lane % MLA_ROPE) < MLA_ROPE // 2
    partner = jnp.where(first, pltpu.roll(x, LANES - MLA_ROPE // 2, axis=1), pltpu.roll(x, MLA_ROPE // 2, axis=1))
    return x * cos + partner * sin


def _mla_down_body(h_ref, g_ref, mod_ref, wkv_ref, wq_ref, gkv_ref, gr_ref, gq_ref, cos_ref, sin_ref,
                   ckv_ref, kr_ref, cq_ref, *, kv_lora):
    u = _norm_mod_rows(h_ref[...], g_ref[...], mod_ref[0], 0, 1).astype(BF16)
    acc = jnp.dot(u, wkv_ref[...], preferred_element_type=F32)
    ckv_ref[...] = _rms_rows(acc[:, :kv_lora], gkv_ref[...]).astype(ckv_ref.dtype)
    kr = acc[:, kv_lora:]
    ms = jnp.sum(kr * kr, axis=-1, keepdims=True) * (1.0 / MLA_ROPE)
    kr = kr * lax.rsqrt(ms + NORM_EPS) * gr_ref[...]
    kr_ref[...] = _rope_lanes(kr, cos_ref[...], sin_ref[...]).astype(kr_ref.dtype)
    acc_q = jnp.dot(u, wq_ref[...], preferred_element_type=F32)
    cq_ref[...] = _rms_rows(acc_q, gq_ref[...]).astype(cq_ref.dtype)


def _rms_groups(x, gain, group):
    ones = jnp.where(lax.broadcasted_iota(jnp.int32, (LANES, LANES), 0) // group
                     == lax.broadcasted_iota(jnp.int32, (LANES, LANES), 1) // group, 1.0, 0.0).astype(BF16)
    ss = jnp.dot((x * x).astype(BF16), ones, preferred_element_type=F32)
    return x * lax.rsqrt(ss * (1.0 / group) + NORM_EPS) * gain


def _mla_ukv_body(c_ref, w_ref, g_ref, kr_ref, kcat_ref, vt_ref, *, heads_per_tile):
    acc = jnp.dot(c_ref[...], w_ref[...], preferred_element_type=F32)
    kr = kr_ref[...]
    dk = MLA_NOPE + LANES
    for h in range(heads_per_tile):
        base = h * (MLA_NOPE + MLA_V)
        kcat_ref[:, h * dk:h * dk + MLA_NOPE] = _rms_groups(acc[:, base:base + MLA_NOPE], g_ref[...],
                                                            MLA_NOPE).astype(kcat_ref.dtype)
        kcat_ref[:, h * dk + MLA_NOPE:(h + 1) * dk] = kr
        vt_ref[h * MLA_V:(h + 1) * MLA_V, :] = acc[:, base + MLA_NOPE:base + MLA_NOPE + MLA_V].T.astype(vt_ref.dtype)


def _mla_uq_body(c_ref, wn_ref, wr_ref, gn_ref, gr_ref, cos_ref, sin_ref, q_ref, *, heads_per_tile, qscale):
    tm = c_ref.shape[0]
    rb = _pick(tm, (256,))
    dk = MLA_NOPE + LANES
    low = lax.broadcasted_iota(jnp.int32, (rb, LANES), 1) < MLA_ROPE

    def project(r0):
        x = c_ref[r0:r0 + rb, :]
        return (jnp.dot(x, wn_ref[...], preferred_element_type=F32),
                jnp.dot(x, wr_ref[...], preferred_element_type=F32))

    def finish(r0, nope, rope):
        rows = slice(r0, r0 + rb)
        for b in range(heads_per_tile // 2):
            xr = _rms_groups(rope[:, b * LANES:(b + 1) * LANES], gr_ref[...], MLA_ROPE)
            xr = _rope_lanes(xr, cos_ref[rows, :], sin_ref[rows, :]) * qscale
            for t in range(2):
                h = 2 * b + t
                qn = _rms_groups(nope[:, h * MLA_NOPE:(h + 1) * MLA_NOPE], gn_ref[...], MLA_NOPE) * qscale
                q_ref[rows, h * dk:h * dk + MLA_NOPE] = qn.astype(q_ref.dtype)
                rr = xr if t == 0 else pltpu.roll(xr, MLA_ROPE, axis=1)
                q_ref[rows, h * dk + MLA_NOPE:(h + 1) * dk] = jnp.where(low, rr, 0.0).astype(q_ref.dtype)

    starts = list(range(0, tm, rb))
    ahead = project(starts[0])
    for n, r0 in enumerate(starts):
        cur = ahead
        if n + 1 < len(starts):
            ahead = project(starts[n + 1])
        finish(r0, *cur)


def _pad_cols(w, n):
    return jnp.pad(w, ((0, 0), (0, n - w.shape[1])))


def _pad_rows(w, n):
    return jnp.pad(w, ((0, n - w.shape[0]), (0, 0)))


def _up128(n):
    return -(-n // LANES) * LANES


def _rope_tables(lay):
    n = lay.S
    rows = n // GRID_W
    row = jnp.broadcast_to(jnp.arange(rows)[:, None], (rows, GRID_W)).reshape(-1)
    col = jnp.broadcast_to(jnp.arange(GRID_W)[None, :], (rows, GRID_W)).reshape(-1)
    n_freq = MLA_ROPE // 4
    inv = ROPE_THETA ** (-jnp.arange(n_freq, dtype=F32) / n_freq)
    ang = jnp.concatenate([row[:, None].astype(F32) * inv, col[:, None].astype(F32) * inv], axis=-1)
    cos, sin = jnp.cos(ang), jnp.sin(ang)
    reps = LANES // MLA_ROPE
    cos_l = jnp.tile(jnp.concatenate([cos, cos], axis=1), (lay.B, reps))
    sin_l = jnp.tile(jnp.concatenate([-sin, sin], axis=1), (lay.B, reps))
    cos_t = jnp.concatenate([jnp.ones((lay.n_ctx, LANES), F32), cos_l], axis=0)
    sin_t = jnp.concatenate([jnp.zeros((lay.n_ctx, LANES), F32), sin_l], axis=0)
    return cos_t, sin_t


def _lora_up_body(*refs, n):
    for x_ref, w_ref, o_ref in zip(refs[:n], refs[n:2 * n], refs[2 * n:]):
        o_ref[...] = jnp.dot(x_ref[...], w_ref[...], preferred_element_type=F32).astype(o_ref.dtype)


def _lora_up(items, lay):
    n = len(items)
    M = items[0][0].shape[0]
    N = items[0][2].shape[1]
    tm = lay.row_tile()
    tn = _pick(N, (1024, 512, 256, 128))
    x_specs = [pl.BlockSpec((tm, w.shape[0]), functools.partial(lambda j, i, blk: (i, blk), blk=blk))
               for _, blk, w in items]
    w_specs = [pl.BlockSpec((w.shape[0], tn), lambda j, i: (0, j)) for _, _, w in items]
    return pl.pallas_call(
        functools.partial(_lora_up_body, n=n),
        grid=(N // tn, M // tm),
        in_specs=x_specs + w_specs,
        out_specs=[pl.BlockSpec((tm, tn), lambda j, i: (i, j))] * n,
        out_shape=[jax.ShapeDtypeStruct((M, N), BF16)] * n,
        compiler_params=_params(("parallel", "parallel")),
        name="lora_up",
    )(*[x for x, _, _ in items], *[w for _, _, w in items])


def _rwkv_mixer(h, norm_g, mod, lay, v_first, j, big_w, pw, vres, rk, lnx_g, lnx_b):
    wr, wk, wv = big_w
    mu, w0, w1, w2, a0, a1, a2, g1, g2, kkp, ka = pw
    nw, na = _up128(w1.shape[2]), _up128(a1.shape[2])
    lora = [(MIX_W, jnp.concatenate([_pad_cols(w1[e], nw) for e in range(2)], axis=1).astype(BF16), "tanh"),
            (MIX_A, jnp.concatenate([_pad_cols(a1[e], na) for e in range(2)], axis=1).astype(BF16), None),
            (MIX_G, g1.astype(BF16), "sigmoid")]
    if vres is not None:
        v0, v1, v2 = vres
        nv = _up128(v1.shape[1])
        lora.append((MIX_V, _pad_cols(v1, nv).astype(BF16), None))
    xr, xk, xv, w_mid, a_mid, g_mid, *v_mid = _rw_pre(h, norm_g, mod, mu, lora, lay)
    mm = functools.partial(_mm, lay=lay)
    r = mm(xr, wr, layer=j, name="rw_r")
    k = mm(xk, wk, layer=j, name="rw_k")
    v = mm(xv, wv, layer=j, name="rw_v")
    if vres is not None:
        v = mm(v_mid[0], _pad_rows(v2, nv).astype(BF16), blend=(v, v_first, v0), name="rw_v2")
    g, wl0, wl1, al0, al1 = _lora_up(
        [(g_mid, 0, g2.astype(BF16))]
        + [(w_mid, e, _pad_rows(w2[e], nw).astype(BF16)) for e in range(2)]
        + [(a_mid, e, _pad_rows(a2[e], na).astype(BF16)) for e in range(2)], lay)
    wl, al = (wl0, wl1), (al0, al1)

    y_fwd = _wkv(r, k, v, wl[0], al[0], w0[0], a0[0], kkp, ka, lay, reverse=False)
    o = _wkv(r, k, v, wl[1], al[1], w0[1], a0[1], kkp, ka, lay, reverse=True,
             epilogue=(al[0], a0[0], y_fwd, g, rk.reshape(-1), lnx_g, lnx_b))
    return o, v


def _mla_mixer(h, norm_g, mod, lay, rope_tabs, wdq, qnorm, wuq, wdkv, kvnorm, wukv, qn_nope, qn_rope, kn_nope,
               kn_rope, need_ctx):
    M, D = h.shape
    heads = D // MLA_V
    kv_lora, q_lora = kvnorm.shape[0], qnorm.shape[0]
    cos_t, sin_t = rope_tabs
    tm = lay.row_tile()
    hpt = _pick(heads, (4, 2))
    dk = MLA_NOPE + LANES
    pad = LANES - MLA_ROPE
    perm = jnp.concatenate([jnp.arange(0, MLA_ROPE, 2), jnp.arange(1, MLA_ROPE, 2)])
    row = lambda t: t.reshape(1, -1)
    par1 = _params(("parallel",))
    par2 = _params(("parallel", "parallel"))

    w_dkv = jnp.concatenate([wdkv[:, :kv_lora], wdkv[:, kv_lora:][:, perm], jnp.zeros((D, pad), F32)], axis=1)
    tmd = min(tm, 512)
    c_kv, k_rope, c_q = pl.pallas_call(
        functools.partial(_mla_down_body, kv_lora=kv_lora),
        grid=(M // tmd,),
        in_specs=[pl.BlockSpec((tmd, D), lambda i: (i, 0)),
                  pl.BlockSpec((1, D), lambda i: (0, 0)),
                  pl.BlockSpec((1, N_MOD, D), lambda i: (lay.mod_row(i, tmd), 0, 0)),
                  pl.BlockSpec((D, kv_lora + LANES), lambda i: (0, 0)),
                  pl.BlockSpec((D, q_lora), lambda i: (0, 0)),
                  pl.BlockSpec((1, kv_lora), lambda i: (0, 0)),
                  pl.BlockSpec((1, LANES), lambda i: (0, 0)),
                  pl.BlockSpec((1, q_lora), lambda i: (0, 0)),
                  pl.BlockSpec((tmd, LANES), lambda i: (i, 0)),
                  pl.BlockSpec((tmd, LANES), lambda i: (i, 0))],
        out_specs=[pl.BlockSpec((tmd, kv_lora), lambda i: (i, 0)), pl.BlockSpec((tmd, LANES), lambda i: (i, 0)),
                   pl.BlockSpec((tmd, q_lora), lambda i: (i, 0))],
        out_shape=[jax.ShapeDtypeStruct((M, kv_lora), BF16), jax.ShapeDtypeStruct((M, LANES), BF16),
                   jax.ShapeDtypeStruct((M, q_lora), BF16)],
        compiler_params=par1, name="mla_down",
    )(h, row(norm_g), mod, w_dkv.astype(BF16), wdq.astype(BF16), row(kvnorm),
      row(jnp.pad(kn_rope[perm], (0, pad))), row(qnorm), cos_t, sin_t)

    k_cat, vt = pl.pallas_call(
        functools.partial(_mla_ukv_body, heads_per_tile=hpt),
        grid=(heads // hpt, M // tm),
        in_specs=[pl.BlockSpec((tm, kv_lora), lambda j, i: (i, 0)),
                  pl.BlockSpec((kv_lora, hpt * (MLA_NOPE + MLA_V)), lambda j, i: (0, j)),
                  pl.BlockSpec((1, MLA_NOPE), lambda j, i: (0, 0)),
                  pl.BlockSpec((tm, LANES), lambda j, i: (i, 0))],
        out_specs=[pl.BlockSpec((tm, hpt * dk), lambda j, i: (i, j)),
                   pl.BlockSpec((hpt * MLA_V, tm), lambda j, i: (j, i))],
        out_shape=[jax.ShapeDtypeStruct((M, heads * dk), BF16), jax.ShapeDtypeStruct((heads * MLA_V, M), BF16)],
        compiler_params=par2, name="mla_ukv",
    )(c_kv, wukv.astype(BF16), row(kn_nope), k_rope)

    q_rows = M if need_ctx else lay.B * lay.S
    r0 = (M - q_rows) // tm
    w3 = wuq.reshape(q_lora, heads, MLA_NOPE + MLA_ROPE)
    w_nope = w3[:, :, :MLA_NOPE].reshape(q_lora, heads * MLA_NOPE)
    w_rope = w3[:, :, MLA_NOPE:][:, :, perm].reshape(q_lora, heads * MLA_ROPE)
    qscale = math.log2(math.e) / math.sqrt(MLA_NOPE + MLA_ROPE)
    q_cat = pl.pallas_call(
        functools.partial(_mla_uq_body, heads_per_tile=hpt, qscale=qscale),
        grid=(heads // hpt, q_rows // tm),
        in_specs=[pl.BlockSpec((tm, q_lora), lambda j, i: (i + r0, 0)),
                  pl.BlockSpec((q_lora, hpt * MLA_NOPE), lambda j, i: (0, j)),
                  pl.BlockSpec((q_lora, hpt * MLA_ROPE), lambda j, i: (0, j)),
                  pl.BlockSpec((1, MLA_NOPE), lambda j, i: (0, 0)),
                  pl.BlockSpec((1, LANES), lambda j, i: (0, 0)),
                  pl.BlockSpec((tm, LANES), lambda j, i: (i + r0, 0)),
                  pl.BlockSpec((tm, LANES), lambda j, i: (i + r0, 0))],
        out_specs=pl.BlockSpec((tm, hpt * dk), lambda j, i: (i, j)),
        out_shape=jax.ShapeDtypeStruct((q_rows, heads * dk), BF16),
        compiler_params=par2, name="mla_uq",
    )(c_q, w_nope.astype(BF16), w_rope.astype(BF16), row(qn_nope), row(jnp.tile(qn_rope[perm], LANES // MLA_ROPE)),
      cos_t, sin_t)

    o_lat = _attn(q_cat, k_cat, vt, heads, lay, q_row0=q_rows - lay.B * lay.S, n_q_rows=lay.S, with_lat_keys=True)
    if not need_ctx:
        return o_lat
    o_ctx = _attn(q_cat, k_cat, vt, heads, lay, q_row0=0, n_q_rows=lay.C, with_lat_keys=False)
    return jnp.concatenate([o_ctx, o_lat], axis=0)


def kernel(x, c, ctx, c_ctx, mod_w, mod_b, norm_g, mlp_w1, mlp_w2, rw_mu, rw_wr, rw_wk, rw_wv, rw_wo, rw_w0, rw_w1, rw_w2, rw_a0, rw_a1, rw_a2, rw_g1, rw_g2, rw_kk, rw_ka, rw_rk, rw_lnx_g, rw_lnx_b, rw_v0, rw_v1, rw_v2, mla_wdq, mla_qnorm, mla_wuq, mla_wdkv, mla_kvnorm, mla_wukv, mla_qn_nope, mla_qn_rope, mla_kn_nope, mla_kn_rope, mla_wo):
    B, S, D = x.shape
    C = ctx.shape[1]
    depth = mod_w.shape[0]
    lay = _Layout(B, C, S)
    rope = _rope_tables(lay)

    sc_all = jnp.concatenate([jax.nn.silu(c_ctx)[None], jax.nn.silu(c)], axis=0)
    h = jnp.concatenate([ctx.reshape(B * C, D), x.reshape(B * S, D)], axis=0)
    v_first = None
    mlp_w2_bf16 = mlp_w2.astype(BF16)
    rw_wo_bf16, mla_wo_bf16 = rw_wo.astype(BF16), mla_wo.astype(BF16)

    for i in range(depth):
        last = i == depth - 1
        j = i // 2
        mod = (_mm(sc_all, mod_w, layer=i, name="adaln") + mod_b[i]).reshape(B + 1, N_MOD, D)
        if i % 2 == 0:
            pw = (rw_mu[j], rw_w0[j], rw_w1[j], rw_w2[j],
                  rw_a0[j], rw_a1[j], rw_a2[j], rw_g1[j], rw_g2[j], rw_kk[j], rw_ka[j])
            vres = None if j == 0 else (rw_v0[j - 1], rw_v1[j - 1], rw_v2[j - 1])
            o, v_cur = _rwkv_mixer(h, norm_g[i, 0], mod, lay, v_first, j, (rw_wr, rw_wk, rw_wv), pw, vres,
                                   rw_rk[j], rw_lnx_g[j], rw_lnx_b[j])
            if j == 0:
                v_first = v_cur
            wo = rw_wo_bf16
        else:
            o = _mla_mixer(h, norm_g[i, 0], mod, lay, rope, mla_wdq[j], mla_qnorm[j], mla_wuq[j], mla_wdkv[j], mla_kvnorm[j],
                           mla_wukv[j], mla_qn_nope[j], mla_qn_rope[j], mla_kn_nope[j], mla_kn_rope[j],
                           need_ctx=not last)
            wo = mla_wo_bf16
        res_row0 = 0
        if last:
            if o.shape[0] != B * S:
                o = o[lay.n_ctx:]
            res_row0 = lay.n_ctx
            lay = _Layout(B, C, S, with_ctx=False)
        h, u2 = _mix_out(o, wo, j, h, res_row0, norm_g[i, 1], mod, lay)
        hid = _mm(u2, mlp_w1, layer=i, act="relu2", out_dtype=BF16, lay=lay, name="mlp_up")
        h = _mm(hid, mlp_w2_bf16, layer=i, res=h, gate=mod[:, 5], lay=lay, name="mlp_down")
    return h.reshape(B, S, D)
```

```python
import functools
import math

import jax
import jax.numpy as jnp
from jax import lax
from jax.experimental import pallas as pl
from jax.experimental.pallas import tpu as pltpu

F32 = jnp.float32
BF16 = jnp.bfloat16

NORM_EPS = 1e-6
N_MOD = 6
GRID_W = 64
RW_HEAD = 64
RW_GN_EPS = 64e-5
RW_L2_EPS = 1e-12
MLA_NOPE = 128
MLA_ROPE = 64
MLA_V = 128
ROPE_THETA = 10000.0

LANES = 128
SUBLANES = 8
BF16_ROWS = 16
WKV_CHUNK = 64
WKV_STAGGER = 3
VMEM_LIMIT_BYTES = 56 * 1024 * 1024

_NT = (((1,), (1,)), ((), ()))


def _pick(n, prefs):
    for p in prefs:
        if n % p == 0:
            return p
    return n


def _params(sem):
    return pltpu.CompilerParams(dimension_semantics=sem, vmem_limit_bytes=VMEM_LIMIT_BYTES)


class _Layout:
    def __init__(self, B, C, S, with_ctx=True):
        self.B, self.C, self.S = B, C, S
        self.n_ctx = B * C if with_ctx else 0
        self.M = self.n_ctx + B * S

    def row_tile(self):
        return _pick(math.gcd(self.n_ctx, self.S) if self.n_ctx else self.S, (1024, 512, 256, 128, 64, 32, 16, 8))

    def seq_tile(self):
        return _pick(math.gcd(self.C, self.S) if self.n_ctx else self.S, (256, 128, 64, 32, 16, 8))

    def mod_row(self, i, tm):
        n_ctx_tiles = self.n_ctx // tm
        per_b = self.S // tm
        lat = 1 + (i - n_ctx_tiles) // per_b
        if n_ctx_tiles == 0:
            return lat
        return jnp.where(i < n_ctx_tiles, 0, lat)


def _mm_body(*refs, nk, act, has_gate, has_blend, cache_w):
    if has_gate:
        x_ref, w_ref, res_ref, gate_ref, o_ref, *scratch = refs
    elif has_blend:
        x_ref, w_ref, first_ref, bias_ref, mid_ref, w2_ref, o_ref, *scratch = refs
    else:
        x_ref, w_ref, o_ref, *scratch = refs

    def finish(acc):
        if act == "relu2":
            acc = jnp.square(jnp.maximum(acc, 0.0))
        elif act == "sigmoid":
            acc = jax.nn.sigmoid(acc)
        elif act == "tanh":
            acc = jnp.tanh(acc)
        if has_gate:
            acc = res_ref[...] + gate_ref[0] * acc
        if has_blend:
            lora = jnp.dot(mid_ref[...], w2_ref[...], preferred_element_type=F32)
            acc = acc + (first_ref[...] - acc) * jax.nn.sigmoid(bias_ref[...] + lora)
        o_ref[...] = acc.astype(o_ref.dtype)

    if cache_w:
        wc_ref = scratch[-1]

        @pl.when(pl.program_id(1) == 0)
        def _():
            wc_ref[...] = w_ref[...].astype(BF16)

        w = wc_ref[...]
    else:
        w = w_ref[...].astype(BF16)
    part = jnp.dot(x_ref[...].astype(BF16), w, preferred_element_type=F32)
    if nk == 1:
        finish(part)
    else:
        acc_ref = scratch[0]
        k = pl.program_id(2)

        @pl.when(k == 0)
        def _():
            acc_ref[...] = part

        @pl.when(k > 0)
        def _():
            acc_ref[...] += part

        @pl.when(k == nk - 1)
        def _():
            finish(acc_ref[...])


def _mm(x, w, *, layer=None, act=None, out_dtype=F32, res=None, gate=None, blend=None, lay=None,
        tm=None, x_kblock=0, name="mm"):
    M = x.shape[0]
    K, N = w.shape[-2:]
    if tm is None:
        tm = lay.row_tile() if lay is not None else _pick(M, (1024, 512, 256, 128, 64, 32, 16, 8))
    tn = _pick(N, (1024, 512, 256, 128))
    tk = K if K <= 2048 else _pick(K, (2048, 1024, 512))
    if K > 2048 and w.dtype == BF16 and tm % 512 == 0 and tn % 512 == 0:
        tm, tn, tk = 256, 1024, K
    nk = K // tk
    has_gate = gate is not None
    has_blend = blend is not None
    cache_w = w.dtype == F32 and nk == 1 and M // tm > 1
    if w.ndim == 3:
        w_spec = pl.BlockSpec((None, tk, tn), lambda j, i, k: (layer, k, j))
    else:
        w_spec = pl.BlockSpec((tk, tn), lambda j, i, k: (k, j))
    in_specs = [pl.BlockSpec((tm, tk), lambda j, i, k: (i, k + x_kblock * nk)), w_spec]
    args = [x, w]
    if has_gate:
        in_specs += [pl.BlockSpec((tm, tn), lambda j, i, k: (i, j)),
                     pl.BlockSpec((1, 1, tn), lambda j, i, k: (lay.mod_row(i, tm), 0, j))]
        args += [res, gate.reshape(gate.shape[0], 1, N)]
    if has_blend:
        first, bias, mid, w2 = blend
        rank = w2.shape[0]
        in_specs += [pl.BlockSpec((tm, tn), lambda j, i, k: (i, j)), pl.BlockSpec((1, tn), lambda j, i, k: (0, j)),
                     pl.BlockSpec((tm, rank), lambda j, i, k: (i, 0)), pl.BlockSpec((rank, tn), lambda j, i, k: (0, j))]
        args += [first, bias.reshape(1, N), mid, w2]
    scratch = [pltpu.VMEM((tm, tn), F32)] if nk > 1 else []
    if cache_w:
        scratch.append(pltpu.VMEM((tk, tn), BF16))
    return pl.pallas_call(
        functools.partial(_mm_body, nk=nk, act=act, has_gate=has_gate, has_blend=has_blend, cache_w=cache_w),
        grid=(N // tn, M // tm, nk),
        in_specs=in_specs,
        out_specs=pl.BlockSpec((tm, tn), lambda j, i, k: (i, j)),
        out_shape=jax.ShapeDtypeStruct((M, N), out_dtype),
        scratch_shapes=scratch,
        compiler_params=_params(("parallel", "arbitrary", "arbitrary")),
        name=name,
    )(*args)


def _norm_mod_rows(x, gain, mod, shift_row, scale_row):
    ms = jnp.mean(x * x, axis=-1, keepdims=True)
    y = x * lax.rsqrt(ms + NORM_EPS) * gain
    return y * (1.0 + mod[scale_row:scale_row + 1]) + mod[shift_row:shift_row + 1]


def _mix_out_body(o_ref, w_ref, *refs, n_src, n_ctx_tiles):
    res_refs = refs[:n_src]
    g_ref, mod_ref, h_ref, u_ref = refs[n_src:]
    in_ctx = pl.program_id(0) < n_ctx_tiles
    mod = mod_ref[0]
    tm = o_ref.shape[0]
    rb = _pick(tm, (256,))
    starts = list(range(0, tm, rb))

    def project(r0):
        return jnp.dot(o_ref[r0:r0 + rb, :], w_ref[...], preferred_element_type=F32)

    ahead = project(starts[0])
    for n, r0 in enumerate(starts):
        acc = ahead
        if n + 1 < len(starts):
            ahead = project(starts[n + 1])
        rows = slice(r0, r0 + rb)
        res = [r[rows, :] for r in res_refs]
        res = res[0] if n_src == 1 else jnp.where(in_ctx, res[0], res[1])
        h = res + mod[2:3] * acc
        h_ref[rows, :] = h
        u_ref[rows, :] = _norm_mod_rows(h, g_ref[...], mod, 3, 4).astype(u_ref.dtype)


def _mix_out(o, w, layer, res, res_row0, g, mod, lay):
    M, D = o.shape
    tm = min(lay.row_tile(), 512)
    r0 = res_row0 // tm
    row_spec = pl.BlockSpec((tm, D), lambda i: (i, 0))
    sources = _row_sources(res, lay, tm)
    res_specs = [pl.BlockSpec((tm, D), lambda i, t0=t0, nt=nt: (jnp.clip(i + r0 - t0, 0, nt - 1), 0))
                 for _, t0, nt in sources]
    return pl.pallas_call(
        functools.partial(_mix_out_body, n_src=len(sources), n_ctx_tiles=lay.n_ctx // tm),
        grid=(M // tm,),
        in_specs=[row_spec, pl.BlockSpec((None, D, D), lambda i: (layer, 0, 0))] + res_specs
                 + [pl.BlockSpec((1, D), lambda i: (0, 0)),
                    pl.BlockSpec((1, N_MOD, D), lambda i: (lay.mod_row(i, tm), 0, 0))],
        out_specs=[row_spec, row_spec],
        out_shape=[jax.ShapeDtypeStruct((M, D), F32), jax.ShapeDtypeStruct((M, D), BF16)],
        compiler_params=_params(("arbitrary",)),
        name="mix_out",
    )(o, w, *[arr for arr, _, _ in sources], g.reshape(1, D), mod)


MIX_R, MIX_W, MIX_K, MIX_V, MIX_A, MIX_G = range(6)


def _rw_pre_body(*refs, tm, n_ctx_tiles, ctx_tiles, lat_tiles, lora_mix, lora_act, n_src):
    n_lora = len(lora_mix)
    src = [refs[3 * s:3 * s + 3] for s in range(n_src)]
    g_ref, mod_ref, mu_ref = refs[3 * n_src:3 * n_src + 3]
    rest = refs[3 * n_src + 3:]
    lora_refs = rest[:n_lora]
    xr_ref, xk_ref, xv_ref = rest[n_lora:n_lora + 3]
    mid_refs = rest[n_lora + 3:]
    i = pl.program_id(0)
    in_ctx = i < n_ctx_tiles
    D = src[0][0].shape[1]

    def load(which, rows, cs):
        vals = [s[which][rows, cs] for s in src]
        return vals[0] if n_src == 1 else jnp.where(in_ctx, vals[0], vals[1])

    slabs = [slice(c, c + LANES) for c in range(0, D, LANES)]
    above = slice(SUBLANES - 1, SUBLANES)
    below = slice(0, 1)

    def inv_rms(which, rows):
        sq = None
        for cs in slabs:
            x = load(which, rows, cs)
            sq = x * x if sq is None else sq + x * x
        return lax.rsqrt(jnp.sum(sq, axis=-1, keepdims=True) * (1.0 / D) + NORM_EPS)

    r_tile, r_above, r_below = inv_rms(0, slice(None)), inv_rms(1, above), inv_rms(2, below)
    pos = jnp.where(in_ctx, i % ctx_tiles, (i - n_ctx_tiles) % lat_tiles)
    last = jnp.where(in_ctx, ctx_tiles - 1, lat_tiles - 1)
    rows = lax.broadcasted_iota(jnp.int32, (tm, 1), 0)
    acc = [None] * n_lora
    for cs in slabs:
        gain, shift, scale1 = g_ref[:, cs], mod_ref[0, 0:1, cs], 1.0 + mod_ref[0, 1:2, cs]
        u = load(0, slice(None), cs) * r_tile * gain * scale1 + shift
        u_above = jnp.where(pos == 0, 0.0, load(1, above, cs) * r_above * gain * scale1 + shift)
        u_below = jnp.where(pos == last, 0.0, load(2, below, cs) * r_below * gain * scale1 + shift)
        prev = jnp.where(rows == 0, u_above, pltpu.roll(u, 1, axis=0))
        nxt = jnp.where(rows == tm - 1, u_below, pltpu.roll(u, tm - 1, axis=0))
        xx = 0.5 * (prev + nxt) - u

        def mix(j):
            return (u + xx * mu_ref[j:j + 1, cs]).astype(BF16)

        xv = mix(MIX_V)
        xr_ref[:, cs] = mix(MIX_R)
        xk_ref[:, cs] = mix(MIX_K)
        xv_ref[:, cs] = xv
        for n, j in enumerate(lora_mix):
            part = jnp.dot(xv if j == MIX_V else mix(j), lora_refs[n][cs, :], preferred_element_type=F32)
            acc[n] = part if acc[n] is None else acc[n] + part
    for n, act in enumerate(lora_act):
        val = acc[n]
        if act == "tanh":
            val = jnp.tanh(val)
        elif act == "sigmoid":
            val = jax.nn.sigmoid(val)
        mid_refs[n][...] = val.astype(mid_refs[n].dtype)


def _row_sources(h, lay, tm):
    if not isinstance(h, tuple):
        return [(h, 0, h.shape[0] // tm)]
    n_ctx_tiles = lay.n_ctx // tm
    return [(h[0], 0, n_ctx_tiles), (h[1], n_ctx_tiles, h[1].shape[0] // tm)]


def _rw_pre(h, g, mod, mu, lora, lay):
    M, D = lay.M, mu.shape[1]
    tm = lay.seq_tile()
    per8 = tm // SUBLANES
    ranks = [w.shape[1] for _, w, _ in lora]
    row_spec = pl.BlockSpec((tm, D), lambda i: (i, 0))
    sources = _row_sources(h, lay, tm)
    src_specs, src_args = [], []
    for arr, t0, nt in sources:
        last8 = nt * per8 - 1

        def local(i, t0=t0, nt=nt):
            return jnp.clip(i - t0, 0, nt - 1)

        src_specs += [pl.BlockSpec((tm, D), lambda i, f=local: (f(i), 0)),
                      pl.BlockSpec((SUBLANES, D), lambda i, f=local: (jnp.maximum(f(i) * per8 - 1, 0), 0)),
                      pl.BlockSpec((SUBLANES, D), lambda i, f=local, l8=last8: (jnp.minimum((f(i) + 1) * per8, l8), 0))]
        src_args += [arr, arr, arr]
    return pl.pallas_call(
        functools.partial(_rw_pre_body, tm=tm, n_ctx_tiles=lay.n_ctx // tm, ctx_tiles=max(lay.C // tm, 1),
                          lat_tiles=lay.S // tm, lora_mix=tuple(j for j, _, _ in lora),
                          lora_act=tuple(a for _, _, a in lora), n_src=len(sources)),
        grid=(M // tm,),
        in_specs=src_specs
                 + [pl.BlockSpec((1, D), lambda i: (0, 0)),
                    pl.BlockSpec((1, N_MOD, D), lambda i: (lay.mod_row(i, tm), 0, 0)),
                    pl.BlockSpec(mu.shape, lambda i: (0, 0))]
                 + [pl.BlockSpec((D, n), lambda i: (0, 0)) for n in ranks],
        out_specs=[row_spec] * 3 + [pl.BlockSpec((tm, n), lambda i: (i, 0)) for n in ranks],
        out_shape=[jax.ShapeDtypeStruct((M, D), BF16)] * 3 + [jax.ShapeDtypeStruct((M, n), BF16) for n in ranks],
        compiler_params=_params(("parallel",)),
        name="rw_pre",
    )(*src_args, g.reshape(1, D), mod, mu, *[w for _, w, _ in lora])


def _wkv_body(*refs, reverse, npairs, nsub, epilogue):
    if epilogue:
        (r_ref, k_ref, v_ref, wl_ref, al_ref, w0_ref, a0_ref, kkp_ref, ka_ref,
         alo_ref, a0o_ref, yo_ref, g_ref, rk_ref, lng_ref, lnb_ref, o_ref, s_ref) = refs
    else:
        r_ref, k_ref, v_ref, wl_ref, al_ref, w0_ref, a0_ref, kkp_ref, ka_ref, o_ref, s_ref = refs
    L = WKV_CHUNK
    H = RW_HEAD

    @pl.when(pl.program_id(2) == 0)
    def _():
        s_ref[...] = jnp.zeros_like(s_ref)

    def order(row, col):
        return (row <= col) if reverse else (row >= col)

    row = lax.broadcasted_iota(jnp.int32, (L, L), 0)
    col = lax.broadcasted_iota(jnp.int32, (L, L), 1)
    tri = jnp.where(order(row, col), 1.0, 0.0).astype(BF16)
    tri2 = jnp.concatenate([tri, tri], axis=1)

    prow = lax.broadcasted_iota(jnp.int32, (L, 2 * L), 0)
    pcol = lax.broadcasted_iota(jnp.int32, (L, 2 * L), 1) & (L - 1)
    incl = order(prow, pcol)
    strict = incl & (prow != pcol)
    incl2 = jnp.concatenate([incl, incl], axis=1)
    eye = jnp.where(prow == pcol, 1.0, 0.0)

    def sibling(s):
        return ((prow // (2 * s)) == (pcol // (2 * s))) & ((prow // s) != (pcol // s))

    bmask = (lax.broadcasted_iota(jnp.int32, (2 * L, LANES), 0) // L
             == lax.broadcasted_iota(jnp.int32, (2 * L, LANES), 1) // H)
    head_ones = jnp.where(lax.broadcasted_iota(jnp.int32, (LANES, LANES), 0) // H
                          == lax.broadcasted_iota(jnp.int32, (LANES, LANES), 1) // H, 1.0, 0.0).astype(BF16)

    def bdf(x):
        return jnp.where(bmask, jnp.concatenate([x, x], axis=0), 0.0)

    def bd(x):
        return bdf(x).astype(BF16)

    def dot(a, b):
        return jnp.dot(a, b, preferred_element_type=F32)

    sigmoid = jax.nn.sigmoid

    def cat0(*xs):
        return jnp.concatenate(xs, axis=0)

    def cat1(*xs):
        return jnp.concatenate(xs, axis=1)

    prs = range(npairs)
    state = [s_ref[p] for p in prs]

    def chunk_stages(ci):
        rows = slice(ci * L, (ci + 1) * L)

        def tiles(ref):
            return [ref[rows, p * LANES:(p + 1) * LANES] for p in prs]

        def vecs(ref):
            return [ref[:, p * LANES:(p + 1) * LANES] for p in prs]

        def head_sum(xs):
            tot = dot(cat0(*xs).astype(BF16), head_ones)
            return [tot[p * L:(p + 1) * L] for p in prs]

        def store(vals):
            for p, val in zip(prs, vals):
                o_ref[rows, p * LANES:(p + 1) * LANES] = val.astype(o_ref.dtype)

        r, k, v = tiles(r_ref), tiles(k_ref), tiles(v_ref)
        ka = vecs(ka_ref)

        a = [sigmoid(a0 + al) for a0, al in zip(vecs(a0_ref), tiles(al_ref))]
        lw = [-math.exp(-0.5) * sigmoid(w0 + wl) for w0, wl in zip(vecs(w0_ref), tiles(wl_ref))]
        yield
        kraw = [k[p] * kkp for p, kkp in zip(prs, vecs(kkp_ref))]
        norm2 = head_sum([x * x for x in kraw])
        kd = [k[p] * (1.0 + (a[p] - 1.0) * ka[p]) for p in prs]
        hi = [x.astype(BF16) for x in lw]
        lo = [(lw[p] - hi[p].astype(F32)).astype(BF16) for p in prs]
        g = [dot(tri2, cat0(hi[p], lo[p])) for p in prs]
        ee = [jnp.exp(jnp.sum(x, axis=0, keepdims=True)) for x in lw]
        yield
        kk = [kraw[p] * lax.rsqrt(jnp.maximum(norm2[p], RW_L2_EPS * RW_L2_EPS)) for p in prs]
        en = [jnp.exp(-x) for x in g]
        abar = [-(kk[p] * jnp.exp(g[p] - lw[p])) for p in prs]
        rbar = [r[p] * jnp.exp(g[p]) for p in prs]
        bt = [kk[p] * a[p] * en[p] for p in prs]
        kt = [kd[p] * en[p] for p in prs]
        yield
        amat = [lax.dot_general(cat0(abar[p], rbar[p]).astype(BF16), cat0(bd(bt[p]), bd(kt[p])), _NT,
                                preferred_element_type=F32) for p in prs]
        vbd = [bd(x) for x in v]
        gam = [jnp.broadcast_to(x, (LANES, LANES)).T for x in ee]
        upd_lhs = [cat1(bdf(bt[p] * ee[p]).T, bdf(kt[p] * ee[p]).T).astype(BF16) for p in prs]
        yield
        a_ab = [jnp.where(strict, x[:L, :2 * L], 0.0) for x in amat]
        a_ak = [jnp.where(strict, x[:L, 2 * L:], 0.0).astype(BF16) for x in amat]
        a_rbk = [jnp.where(incl2, x[L:], 0.0).astype(BF16) for x in amat]
        av = [dot(a_ak[p], vbd[p]) for p in prs]

        t = [eye + jnp.where(sibling(1), x, 0.0) for x in a_ab]
        s = 2
        while s < L:
            sib = sibling(s)
            a_l = [bd(jnp.where(sib, a_ab[p], 0.0)) for p in prs]
            if s < SUBLANES:
                half = [dot(t[p].astype(BF16), a_l[p]).astype(BF16) for p in prs]
                yield
                t = [t[p] + dot(half[p], bd(t[p])) for p in prs]
            else:
                blocks = range(L // s)
                moving = [b for b in blocks if (b % 2 == 1) != reverse]

                def pick(x, n):
                    return x[n * s:(n + 1) * s]

                half = [dot(cat0(*[pick(t[p], b) for b in moving]).astype(BF16), a_l[p]).astype(BF16) for p in prs]
                yield
                upd = [dot(half[p], bd(t[p])) for p in prs]
                t = [cat0(*[pick(t[p], b) + pick(upd[p], moving.index(b)) if b in moving else pick(t[p], b)
                            for b in blocks]) for p in prs]
            yield
            s *= 2

        wu = [dot(t[p].astype(BF16), cat1(bd(abar[p]), bd(av[p]))) for p in prs]
        yield
        wr_lhs = [cat0(wu[p][:, :LANES], rbar[p]).astype(BF16) for p in prs]
        ut = [x[:, LANES:] for x in wu]
        c2_lhs = [cat0(a_rbk[p], upd_lhs[p]) for p in prs]
        yield

        wr = [dot(wr_lhs[p], state[p].astype(BF16)) for p in prs]
        yield
        u = [wr[p][:L] + ut[p] for p in prs]
        out2 = [dot(c2_lhs[p], cat0(bd(u[p]), vbd[p])) for p in prs]
        yield
        y = [wr[p][L:] + out2[p][:L] for p in prs]
        for p in prs:
            state[p] = state[p] * gam[p] + out2[p][L:]
        if not epilogue:
            store(y)
            return
        yield

        inv_n = 1.0 / H
        ytot = [y[p] + yo for p, yo in zip(prs, tiles(yo_ref))]
        mean = head_sum(ytot)
        a_o = [sigmoid(a0 + al) for a0, al in zip(vecs(a0o_ref), tiles(alo_ref))]
        kd_sum = [kd[p] + k[p] * (1.0 + (a_o[p] - 1.0) * ka[p]) for p in prs]
        bonus = head_sum([r[p] * rk * kd_sum[p] for p, rk in zip(prs, vecs(rk_ref))])
        yield
        dev = [ytot[p] - mean[p] * inv_n for p in prs]
        var = head_sum([x * x for x in dev])
        yield
        lng, lnb, gate = vecs(lng_ref), vecs(lnb_ref), tiles(g_ref)
        store([(dev[p] * lax.rsqrt(var[p] * inv_n + RW_GN_EPS) * lng[p] + lnb[p] + bonus[p] * v[p]) * gate[p]
               for p in prs])

    waiting = [chunk_stages(ci) for ci in (reversed(range(nsub)) if reverse else range(nsub))]
    running = []
    step = 0
    while waiting or running:
        if waiting and step % WKV_STAGGER == 0:
            running.append(waiting.pop(0))
        for gen in list(running):
            if next(gen, "done") == "done":
                running.remove(gen)
        step += 1
    for p in prs:
        s_ref[p] = state[p]


def _wkv(r, k, v, wl, al, w0, a0, kkp, ka, lay, *, reverse, epilogue=None):
    M, D = r.shape
    L = WKV_CHUNK
    td = min(D, 16 * LANES)
    npairs = td // LANES
    nsub = _pick(math.gcd(lay.C, lay.S) // L, (4, 2, 1))
    tb = nsub * L
    ctx_blk = lay.C // tb
    lat_blk = lay.S // tb
    ctx_total = lay.n_ctx // tb

    def tok_block(b, c):
        if reverse:
            ctx_c = ctx_blk - 1 - c
            lat_c = lat_blk - 1 - (c - ctx_blk)
        else:
            ctx_c = c
            lat_c = c - ctx_blk
        return jnp.where(c < ctx_blk, b * ctx_blk + ctx_c, ctx_total + b * lat_blk + lat_c)

    mat = pl.BlockSpec((tb, td), lambda b, d, c: (tok_block(b, c), d))
    vec = pl.BlockSpec((1, td), lambda b, d, c: (0, d))
    row = lambda x: x.reshape(1, D)
    args = [r, k, v, wl, al, row(w0), row(a0), row(kkp), row(ka)]
    specs = [mat] * 5 + [vec] * 4
    if epilogue is not None:
        al_o, a0_o, y_o, gate, rk, lnx_g, lnx_b = epilogue
        args += [al_o, row(a0_o), y_o, gate, row(rk), row(lnx_g), row(lnx_b)]
        specs += [mat, vec, mat, mat, vec, vec, vec]
    return pl.pallas_call(
        functools.partial(_wkv_body, reverse=reverse, npairs=npairs, nsub=nsub, epilogue=epilogue is not None),
        grid=(lay.B, D // td, ctx_blk + lat_blk),
        in_specs=specs,
        out_specs=mat,
        out_shape=jax.ShapeDtypeStruct((M, D), F32 if epilogue is None else BF16),
        scratch_shapes=[pltpu.VMEM((npairs, LANES, LANES), F32)],
        compiler_params=_params(("parallel", "parallel", "arbitrary")),
        name="wkv_bwd" if reverse else "wkv_fwd",
    )(*args)


ATTN_KEY_CHUNK = 1024


def _attn_body(q_ref, *refs, n_kv):
    k_refs, vt_refs = refs[:n_kv], refs[n_kv:2 * n_kv]
    o_ref = refs[2 * n_kv]
    q = q_ref[...]
    dv = vt_refs[0].shape[0]
    pieces = []
    for k_ref, vt_ref in zip(k_refs, vt_refs):
        n = k_ref.shape[0]
        for lo in range(0, n, ATTN_KEY_CHUNK):
            pieces.append((k_ref, vt_ref, lo, min(lo + ATTN_KEY_CHUNK, n)))

    def scores(piece):
        k_ref, _, lo, hi = piece
        return lax.dot_general(k_ref[lo:hi, :], q, _NT, preferred_element_type=F32)

    s_next = scores(pieces[0])
    m = l = acc = None
    for c, (_, vt_ref, lo, hi) in enumerate(pieces):
        s = s_next
        if c + 1 < len(pieces):
            s_next = scores(pieces[c + 1])
        m_c = jnp.max(s, axis=0, keepdims=True)
        m_new = m_c if m is None else jnp.maximum(m, m_c)
        p = jnp.exp2(s - m_new).astype(BF16)
        lhs = jnp.concatenate([vt_ref[:, lo:hi], jnp.ones((BF16_ROWS, hi - lo), BF16)], axis=0)
        pv = jnp.dot(lhs, p, preferred_element_type=F32)
        l_c, pv = pv[dv:dv + 1], pv[:dv]
        if m is None:
            l, acc = l_c, pv
        else:
            alpha = jnp.exp2(m - m_new)
            l, acc = alpha * l + l_c, alpha * acc + pv
        m = m_new
    o_ref[...] = (acc / l).T.astype(o_ref.dtype)


def _attn(q, k, vt, heads, lay, *, q_row0, n_q_rows, with_lat_keys):
    dk = k.shape[1] // heads
    dv = vt.shape[0] // heads
    tq = _pick(n_q_rows, (2048, 1024, 512, 256, 128, 64, 32, 16, 8))
    per_b = n_q_rows // tq
    q0 = q_row0 // tq
    kv = [(lay.C, 0)]
    k_lat, vt_lat = k, vt
    if with_lat_keys:
        kv.append((lay.S, lay.n_ctx // lay.S))
        if lay.n_ctx % lay.S:
            k_lat, vt_lat, kv[1] = k[lay.n_ctx:], vt[:, lay.n_ctx:], (lay.S, 0)
    in_specs = [pl.BlockSpec((tq, dk), lambda b, h, i: (q0 + b * per_b + i, h))]
    in_specs += [pl.BlockSpec((n, dk), functools.partial(lambda b, h, i, base: (base + b, h), base=base))
                 for n, base in kv]
    in_specs += [pl.BlockSpec((dv, n), functools.partial(lambda b, h, i, base: (h, base + b), base=base))
                 for n, base in kv]
    operands = [q, k] + ([k_lat] if with_lat_keys else []) + [vt] + ([vt_lat] if with_lat_keys else [])
    return pl.pallas_call(
        functools.partial(_attn_body, n_kv=len(kv)),
        grid=(lay.B, heads, per_b),
        in_specs=in_specs,
        out_specs=pl.BlockSpec((tq, dv), lambda b, h, i: (b * per_b + i, h)),
        out_shape=jax.ShapeDtypeStruct((lay.B * n_q_rows, heads * dv), BF16),
        compiler_params=_params(("parallel", "parallel", "arbitrary")),
        name="attn",
    )(*operands)


def _rms_rows(x, gain):
    return x * lax.rsqrt(jnp.mean(x * x, axis=-1, keepdims=True) + NORM_EPS) * gain


def _rope_lanes(x, cos, sin):
    lane = lax.broadcasted_iota(jnp.int32, x.shape, 1)
    first = (lane % MLA_ROPE) < MLA_ROPE // 2
    partner = jnp.where(first, pltpu.roll(x, LANES - MLA_ROPE // 2, axis=1), pltpu.roll(x, MLA_ROPE // 2, axis=1))
    return x * cos + partner * sin


def _mla_down_body(h_ref, g_ref, mod_ref, wkv_ref, wq_ref, gkv_ref, gr_ref, gq_ref, cos_ref, sin_ref,
                   ckv_ref, kr_ref, cq_ref, *, kv_lora):
    u = _norm_mod_rows(h_ref[...], g_ref[...], mod_ref[0], 0, 1).astype(BF16)
    acc = jnp.dot(u, wkv_ref[...], preferred_element_type=F32)
    ckv_ref[...] = _rms_rows(acc[:, :kv_lora], gkv_ref[...]).astype(ckv_ref.dtype)
    kr = acc[:, kv_lora:]
    ms = jnp.sum(kr * kr, axis=-1, keepdims=True) * (1.0 / MLA_ROPE)
    kr = kr * lax.rsqrt(ms + NORM_EPS) * gr_ref[...]
    kr_ref[...] = _rope_lanes(kr, cos_ref[...], sin_ref[...]).astype(kr_ref.dtype)
    acc_q = jnp.dot(u, wq_ref[...], preferred_element_type=F32)
    cq_ref[...] = _rms_rows(acc_q, gq_ref[...]).astype(cq_ref.dtype)


def _rms_groups(x, gain, group):
    ones = jnp.where(lax.broadcasted_iota(jnp.int32, (LANES, LANES), 0) // group
                     == lax.broadcasted_iota(jnp.int32, (LANES, LANES), 1) // group, 1.0, 0.0).astype(BF16)
    ss = jnp.dot((x * x).astype(BF16), ones, preferred_element_type=F32)
    return x * lax.rsqrt(ss * (1.0 / group) + NORM_EPS) * gain


def _mla_ukv_body(c_ref, w_ref, g_ref, kr_ref, kcat_ref, vt_ref, *, heads_per_tile):
    acc = jnp.dot(c_ref[...], w_ref[...], preferred_element_type=F32)
    kr = kr_ref[...]
    dk = MLA_NOPE + LANES
    for h in range(heads_per_tile):
        base = h * (MLA_NOPE + MLA_V)
        kcat_ref[:, h * dk:h * dk + MLA_NOPE] = _rms_groups(acc[:, base:base + MLA_NOPE], g_ref[...],
                                                            MLA_NOPE).astype(kcat_ref.dtype)
        kcat_ref[:, h * dk + MLA_NOPE:(h + 1) * dk] = kr
        vt_ref[h * MLA_V:(h + 1) * MLA_V, :] = acc[:, base + MLA_NOPE:base + MLA_NOPE + MLA_V].T.astype(vt_ref.dtype)


def _mla_uq_body(c_ref, wn_ref, wr_ref, gn_ref, gr_ref, cos_ref, sin_ref, q_ref, *, heads_per_tile, qscale):
    tm = c_ref.shape[0]
    rb = _pick(tm, (256,))
    dk = MLA_NOPE + LANES
    low = lax.broadcasted_iota(jnp.int32, (rb, LANES), 1) < MLA_ROPE

    def project(r0):
        x = c_ref[r0:r0 + rb, :]
        return (jnp.dot(x, wn_ref[...], preferred_element_type=F32),
                jnp.dot(x, wr_ref[...], preferred_element_type=F32))

    def finish(r0, nope, rope):
        rows = slice(r0, r0 + rb)
        for b in range(heads_per_tile // 2):
            xr = _rms_groups(rope[:, b * LANES:(b + 1) * LANES], gr_ref[...], MLA_ROPE)
            xr = _rope_lanes(xr, cos_ref[rows, :], sin_ref[rows, :]) * qscale
            for t in range(2):
                h = 2 * b + t
                qn = _rms_groups(nope[:, h * MLA_NOPE:(h + 1) * MLA_NOPE], gn_ref[...], MLA_NOPE) * qscale
                q_ref[rows, h * dk:h * dk + MLA_NOPE] = qn.astype(q_ref.dtype)
                rr = xr if t == 0 else pltpu.roll(xr, MLA_ROPE, axis=1)
                q_ref[rows, h * dk + MLA_NOPE:(h + 1) * dk] = jnp.where(low, rr, 0.0).astype(q_ref.dtype)

    starts = list(range(0, tm, rb))
    ahead = project(starts[0])
    for n, r0 in enumerate(starts):
        cur = ahead
        if n + 1 < len(starts):
            ahead = project(starts[n + 1])
        finish(r0, *cur)


def _pad_cols(w, n):
    return jnp.pad(w, ((0, 0), (0, n - w.shape[1])))


def _pad_rows(w, n):
    return jnp.pad(w, ((0, n - w.shape[0]), (0, 0)))


def _up128(n):
    return -(-n // LANES) * LANES


def _rope_tables(lay):
    n = lay.S
    rows = n // GRID_W
    row = jnp.broadcast_to(jnp.arange(rows)[:, None], (rows, GRID_W)).reshape(-1)
    col = jnp.broadcast_to(jnp.arange(GRID_W)[None, :], (rows, GRID_W)).reshape(-1)
    n_freq = MLA_ROPE // 4
    inv = ROPE_THETA ** (-jnp.arange(n_freq, dtype=F32) / n_freq)
    ang = jnp.concatenate([row[:, None].astype(F32) * inv, col[:, None].astype(F32) * inv], axis=-1)
    cos, sin = jnp.cos(ang), jnp.sin(ang)
    reps = LANES // MLA_ROPE
    cos_l = jnp.tile(jnp.concatenate([cos, cos], axis=1), (lay.B, reps))
    sin_l = jnp.tile(jnp.concatenate([-sin, sin], axis=1), (lay.B, reps))
    cos_t = jnp.concatenate([jnp.ones((lay.n_ctx, LANES), F32), cos_l], axis=0)
    sin_t = jnp.concatenate([jnp.zeros((lay.n_ctx, LANES), F32), sin_l], axis=0)
    return cos_t, sin_t


def _lora_up_body(*refs, n):
    for x_ref, w_ref, o_ref in zip(refs[:n], refs[n:2 * n], refs[2 * n:]):
        o_ref[...] = jnp.dot(x_ref[...], w_ref[...], preferred_element_type=F32).astype(o_ref.dtype)


def _lora_up(items, lay):
    n = len(items)
    M = items[0][0].shape[0]
    N = items[0][2].shape[1]
    tm = lay.row_tile()
    tn = _pick(N, (1024, 512, 256, 128))
    x_specs = [pl.BlockSpec((tm, w.shape[0]), functools.partial(lambda j, i, blk: (i, blk), blk=blk))
               for _, blk, w in items]
    w_specs = [pl.BlockSpec((w.shape[0], tn), lambda j, i: (0, j)) for _, _, w in items]
    return pl.pallas_call(
        functools.partial(_lora_up_body, n=n),
        grid=(N // tn, M // tm),
        in_specs=x_specs + w_specs,
        out_specs=[pl.BlockSpec((tm, tn), lambda j, i: (i, j))] * n,
        out_shape=[jax.ShapeDtypeStruct((M, N), BF16)] * n,
        compiler_params=_params(("parallel", "parallel")),
        name="lora_up",
    )(*[x for x, _, _ in items], *[w for _, _, w in items])


def _rwkv_mixer(h, norm_g, mod, lay, v_first, j, big_w, pw, vres, rk, lnx_g, lnx_b):
    wr, wk, wv = big_w
    mu, w0, w1, w2, a0, a1, a2, g1, g2, kkp, ka = pw
    nw, na = _up128(w1.shape[2]), _up128(a1.shape[2])
    lora = [(MIX_W, jnp.concatenate([_pad_cols(w1[e], nw) for e in range(2)], axis=1).astype(BF16), "tanh"),
            (MIX_A, jnp.concatenate([_pad_cols(a1[e], na) for e in range(2)], axis=1).astype(BF16), None),
            (MIX_G, g1.astype(BF16), "sigmoid")]
    if vres is not None:
        v0, v1, v2 = vres
        nv = _up128(v1.shape[1])
        lora.append((MIX_V, _pad_cols(v1, nv).astype(BF16), None))
    xr, xk, xv, w_mid, a_mid, g_mid, *v_mid = _rw_pre(h, norm_g, mod, mu, lora, lay)
    mm = functools.partial(_mm, lay=lay)
    r = mm(xr, wr, layer=j, name="rw_r")
    k = mm(xk, wk, layer=j, name="rw_k")
    blend = None if vres is None else (v_first, v0, v_mid[0], _pad_rows(v2, nv).astype(BF16))
    v = mm(xv, wv, layer=j, blend=blend, name="rw_v")
    g, wl0, wl1, al0, al1 = _lora_up(
        [(g_mid, 0, g2.astype(BF16))]
        + [(w_mid, e, _pad_rows(w2[e], nw).astype(BF16)) for e in range(2)]
        + [(a_mid, e, _pad_rows(a2[e], na).astype(BF16)) for e in range(2)], lay)
    wl, al = (wl0, wl1), (al0, al1)

    y_fwd = _wkv(r, k, v, wl[0], al[0], w0[0], a0[0], kkp, ka, lay, reverse=False)
    o = _wkv(r, k, v, wl[1], al[1], w0[1], a0[1], kkp, ka, lay, reverse=True,
             epilogue=(al[0], a0[0], y_fwd, g, rk.reshape(-1), lnx_g, lnx_b))
    return o, v


def _mla_mixer(h, norm_g, mod, lay, rope_tabs, wdq, qnorm, wuq, wdkv, kvnorm, wukv, qn_nope, qn_rope, kn_nope,
               kn_rope, need_ctx):
    M, D = h.shape
    heads = D // MLA_V
    kv_lora, q_lora = kvnorm.shape[0], qnorm.shape[0]
    cos_t, sin_t = rope_tabs
    tm = lay.row_tile()
    hpt = _pick(heads, (4, 2))
    dk = MLA_NOPE + LANES
    pad = LANES - MLA_ROPE
    perm = jnp.concatenate([jnp.arange(0, MLA_ROPE, 2), jnp.arange(1, MLA_ROPE, 2)])
    row = lambda t: t.reshape(1, -1)
    par1 = _params(("parallel",))
    par2 = _params(("parallel", "parallel"))

    w_dkv = jnp.concatenate([wdkv[:, :kv_lora], wdkv[:, kv_lora:][:, perm], jnp.zeros((D, pad), F32)], axis=1)
    tmd = min(tm, 512)
    c_kv, k_rope, c_q = pl.pallas_call(
        functools.partial(_mla_down_body, kv_lora=kv_lora),
        grid=(M // tmd,),
        in_specs=[pl.BlockSpec((tmd, D), lambda i: (i, 0)),
                  pl.BlockSpec((1, D), lambda i: (0, 0)),
                  pl.BlockSpec((1, N_MOD, D), lambda i: (lay.mod_row(i, tmd), 0, 0)),
                  pl.BlockSpec((D, kv_lora + LANES), lambda i: (0, 0)),
                  pl.BlockSpec((D, q_lora), lambda i: (0, 0)),
                  pl.BlockSpec((1, kv_lora), lambda i: (0, 0)),
                  pl.BlockSpec((1, LANES), lambda i: (0, 0)),
                  pl.BlockSpec((1, q_lora), lambda i: (0, 0)),
                  pl.BlockSpec((tmd, LANES), lambda i: (i, 0)),
                  pl.BlockSpec((tmd, LANES), lambda i: (i, 0))],
        out_specs=[pl.BlockSpec((tmd, kv_lora), lambda i: (i, 0)), pl.BlockSpec((tmd, LANES), lambda i: (i, 0)),
                   pl.BlockSpec((tmd, q_lora), lambda i: (i, 0))],
        out_shape=[jax.ShapeDtypeStruct((M, kv_lora), BF16), jax.ShapeDtypeStruct((M, LANES), BF16),
                   jax.ShapeDtypeStruct((M, q_lora), BF16)],
        compiler_params=par1, name="mla_down",
    )(h, row(norm_g), mod, w_dkv.astype(BF16), wdq.astype(BF16), row(kvnorm),
      row(jnp.pad(kn_rope[perm], (0, pad))), row(qnorm), cos_t, sin_t)

    k_cat, vt = pl.pallas_call(
        functools.partial(_mla_ukv_body, heads_per_tile=hpt),
        grid=(heads // hpt, M // tm),
        in_specs=[pl.BlockSpec((tm, kv_lora), lambda j, i: (i, 0)),
                  pl.BlockSpec((kv_lora, hpt * (MLA_NOPE + MLA_V)), lambda j, i: (0, j)),
                  pl.BlockSpec((1, MLA_NOPE), lambda j, i: (0, 0)),
                  pl.BlockSpec((tm, LANES), lambda j, i: (i, 0))],
        out_specs=[pl.BlockSpec((tm, hpt * dk), lambda j, i: (i, j)),
                   pl.BlockSpec((hpt * MLA_V, tm), lambda j, i: (j, i))],
        out_shape=[jax.ShapeDtypeStruct((M, heads * dk), BF16), jax.ShapeDtypeStruct((heads * MLA_V, M), BF16)],
        compiler_params=par2, name="mla_ukv",
    )(c_kv, wukv.astype(BF16), row(kn_nope), k_rope)

    q_rows = M if need_ctx else lay.B * lay.S
    r0 = (M - q_rows) // tm
    w3 = wuq.reshape(q_lora, heads, MLA_NOPE + MLA_ROPE)
    w_nope = w3[:, :, :MLA_NOPE].reshape(q_lora, heads * MLA_NOPE)
    w_rope = w3[:, :, MLA_NOPE:][:, :, perm].reshape(q_lora, heads * MLA_ROPE)
    qscale = math.log2(math.e) / math.sqrt(MLA_NOPE + MLA_ROPE)
    q_cat = pl.pallas_call(
        functools.partial(_mla_uq_body, heads_per_tile=hpt, qscale=qscale),
        grid=(heads // hpt, q_rows // tm),
        in_specs=[pl.BlockSpec((tm, q_lora), lambda j, i: (i + r0, 0)),
                  pl.BlockSpec((q_lora, hpt * MLA_NOPE), lambda j, i: (0, j)),
                  pl.BlockSpec((q_lora, hpt * MLA_ROPE), lambda j, i: (0, j)),
                  pl.BlockSpec((1, MLA_NOPE), lambda j, i: (0, 0)),
                  pl.BlockSpec((1, LANES), lambda j, i: (0, 0)),
                  pl.BlockSpec((tm, LANES), lambda j, i: (i + r0, 0)),
                  pl.BlockSpec((tm, LANES), lambda j, i: (i + r0, 0))],
        out_specs=pl.BlockSpec((tm, hpt * dk), lambda j, i: (i, j)),
        out_shape=jax.ShapeDtypeStruct((q_rows, heads * dk), BF16),
        compiler_params=par2, name="mla_uq",
    )(c_q, w_nope.astype(BF16), w_rope.astype(BF16), row(qn_nope), row(jnp.tile(qn_rope[perm], LANES // MLA_ROPE)),
      cos_t, sin_t)

    o_lat = _attn(q_cat, k_cat, vt, heads, lay, q_row0=q_rows - lay.B * lay.S, n_q_rows=lay.S, with_lat_keys=True)
    if not need_ctx:
        return o_lat
    o_ctx = _attn(q_cat, k_cat, vt, heads, lay, q_row0=0, n_q_rows=lay.C, with_lat_keys=False)
    return jnp.concatenate([o_ctx, o_lat], axis=0)


def kernel(x, c, ctx, c_ctx, mod_w, mod_b, norm_g, mlp_w1, mlp_w2, rw_mu, rw_wr, rw_wk, rw_wv, rw_wo, rw_w0, rw_w1, rw_w2, rw_a0, rw_a1, rw_a2, rw_g1, rw_g2, rw_kk, rw_ka, rw_rk, rw_lnx_g, rw_lnx_b, rw_v0, rw_v1, rw_v2, mla_wdq, mla_qnorm, mla_wuq, mla_wdkv, mla_kvnorm, mla_wukv, mla_qn_nope, mla_qn_rope, mla_kn_nope, mla_kn_rope, mla_wo):
    B, S, D = x.shape
    C = ctx.shape[1]
    depth = mod_w.shape[0]
    lay = _Layout(B, C, S)
    rope = _rope_tables(lay)

    sc_all = jnp.concatenate([jax.nn.silu(c_ctx)[None], jax.nn.silu(c)], axis=0)
    h = (ctx.reshape(B * C, D), x.reshape(B * S, D))
    v_first = None
    mlp_w2_bf16 = mlp_w2.astype(BF16)
    rw_wo_bf16, mla_wo_bf16 = rw_wo.astype(BF16), mla_wo.astype(BF16)

    for i in range(depth):
        last = i == depth - 1
        j = i // 2
        mod = (_mm(sc_all, mod_w, layer=i, name="adaln") + mod_b[i]).reshape(B + 1, N_MOD, D)
        if i % 2 == 0:
            pw = (rw_mu[j], rw_w0[j], rw_w1[j], rw_w2[j],
                  rw_a0[j], rw_a1[j], rw_a2[j], rw_g1[j], rw_g2[j], rw_kk[j], rw_ka[j])
            vres = None if j == 0 else (rw_v0[j - 1], rw_v1[j - 1], rw_v2[j - 1])
            o, v_cur = _rwkv_mixer(h, norm_g[i, 0], mod, lay, v_first, j, (rw_wr, rw_wk, rw_wv), pw, vres,
                                   rw_rk[j], rw_lnx_g[j], rw_lnx_b[j])
            if j == 0:
                v_first = v_cur
            wo = rw_wo_bf16
        else:
            o = _mla_mixer(h, norm_g[i, 0], mod, lay, rope, mla_wdq[j], mla_qnorm[j], mla_wuq[j], mla_wdkv[j], mla_kvnorm[j],
                           mla_wukv[j], mla_qn_nope[j], mla_qn_rope[j], mla_kn_nope[j], mla_kn_rope[j],
                           need_ctx=not last)
            wo = mla_wo_bf16
        res_row0 = 0
        if last:
            if o.shape[0] != B * S:
                o = o[lay.n_ctx:]
            res_row0 = lay.n_ctx
            lay = _Layout(B, C, S, with_ctx=False)
        h, u2 = _mix_out(o, wo, j, h, res_row0, norm_g[i, 1], mod, lay)
        hid = _mm(u2, mlp_w1, layer=i, act="relu2", out_dtype=BF16, lay=lay, name="mlp_up")
        h = _mm(hid, mlp_w2_bf16, layer=i, res=h, gate=mod[:, 5], lay=lay, name="mlp_down")
    return h.reshape(B, S, D)
```

```python
import functools
import math

import jax
import jax.numpy as jnp
from jax import lax
from jax.experimental import pallas as pl
from jax.experimental.pallas import tpu as pltpu

F32 = jnp.float32
BF16 = jnp.bfloat16

NORM_EPS = 1e-6
N_MOD = 6
GRID_W = 64
RW_HEAD = 64
RW_GN_EPS = 64e-5
RW_L2_EPS = 1e-12
MLA_NOPE = 128
MLA_ROPE = 64
MLA_V = 128
ROPE_THETA = 10000.0

LANES = 128
SUBLANES = 8
BF16_ROWS = 16
WKV_CHUNK = 64
WKV_STAGGER = 3
VMEM_LIMIT_BYTES = 56 * 1024 * 1024

_NT = (((1,), (1,)), ((), ()))


def _pick(n, prefs):
    for p in prefs:
        if n % p == 0:
            return p
    return n


def _params(sem):
    return pltpu.CompilerParams(dimension_semantics=sem, vmem_limit_bytes=VMEM_LIMIT_BYTES)


class _Layout:
    def __init__(self, B, C, S, with_ctx=True):
        self.B, self.C, self.S = B, C, S
        self.n_ctx = B * C if with_ctx else 0
        self.M = self.n_ctx + B * S

    def row_tile(self):
        return _pick(math.gcd(self.n_ctx, self.S) if self.n_ctx else self.S, (1024, 512, 256, 128, 64, 32, 16, 8))

    def seq_tile(self):
        return _pick(math.gcd(self.C, self.S) if self.n_ctx else self.S, (256, 128, 64, 32, 16, 8))

    def mod_row(self, i, tm):
        n_ctx_tiles = self.n_ctx // tm
        per_b = self.S // tm
        lat = 1 + (i - n_ctx_tiles) // per_b
        if n_ctx_tiles == 0:
            return lat
        return jnp.where(i < n_ctx_tiles, 0, lat)


MM_MAX_F32_K = 2048
MM_DEEP_ROW_TILE = 256
ROW_SUBBLOCK = 256


def _mm_body(*refs, relu2, has_gate, has_blend, cache_w):
    if has_gate:
        x_ref, w_ref, res_ref, gate_ref, o_ref, *scratch = refs
    elif has_blend:
        x_ref, w_ref, first_ref, bias_ref, mid_ref, w2_ref, o_ref, *scratch = refs
    else:
        x_ref, w_ref, o_ref, *scratch = refs

    if cache_w:
        wc_ref = scratch[0]

        @pl.when(pl.program_id(1) == 0)
        def _():
            wc_ref[...] = w_ref[...].astype(BF16)

        w = wc_ref[...]
    else:
        w = w_ref[...].astype(BF16)
    acc = jnp.dot(x_ref[...].astype(BF16), w, preferred_element_type=F32)
    if relu2:
        acc = jnp.square(jnp.maximum(acc, 0.0))
    if has_gate:
        acc = res_ref[...] + gate_ref[0] * acc
    if has_blend:
        lora = jnp.dot(mid_ref[...], w2_ref[...], preferred_element_type=F32)
        acc = acc + (first_ref[...] - acc) * jax.nn.sigmoid(bias_ref[...] + lora)
    o_ref[...] = acc.astype(o_ref.dtype)


def _mm(x, w, *, layer=None, relu2=False, out_dtype=F32, res=None, gate=None, blend=None, lay=None, name="mm"):
    M, K = x.shape
    N = w.shape[-1]
    tm = lay.row_tile() if lay is not None else _pick(M, (1024, 512, 256, 128, 64, 32, 16, 8))
    tn = _pick(N, (1024, 512, 256, 128))
    if K > MM_MAX_F32_K:
        assert w.dtype == BF16, "deep contractions take pre-rounded weights"
        tm = _pick(tm, (MM_DEEP_ROW_TILE,))
    has_gate = gate is not None
    has_blend = blend is not None
    cache_w = w.dtype == F32 and M // tm > 1
    if w.ndim == 3:
        w_spec = pl.BlockSpec((None, K, tn), lambda j, i: (layer, 0, j))
    else:
        w_spec = pl.BlockSpec((K, tn), lambda j, i: (0, j))
    tile = pl.BlockSpec((tm, tn), lambda j, i: (i, j))
    in_specs = [pl.BlockSpec((tm, K), lambda j, i: (i, 0)), w_spec]
    args = [x, w]
    if has_gate:
        in_specs += [tile, pl.BlockSpec((1, 1, tn), lambda j, i: (lay.mod_row(i, tm), 0, j))]
        args += [res, gate.reshape(gate.shape[0], 1, N)]
    if has_blend:
        first, bias, mid, w2 = blend
        rank = w2.shape[0]
        in_specs += [tile, pl.BlockSpec((1, tn), lambda j, i: (0, j)),
                     pl.BlockSpec((tm, rank), lambda j, i: (i, 0)), pl.BlockSpec((rank, tn), lambda j, i: (0, j))]
        args += [first, bias.reshape(1, N), mid, w2]
    return pl.pallas_call(
        functools.partial(_mm_body, relu2=relu2, has_gate=has_gate, has_blend=has_blend, cache_w=cache_w),
        grid=(N // tn, M // tm),
        in_specs=in_specs,
        out_specs=tile,
        out_shape=jax.ShapeDtypeStruct((M, N), out_dtype),
        scratch_shapes=[pltpu.VMEM((K, tn), BF16)] if cache_w else [],
        compiler_params=_params(("parallel", "arbitrary")),
        name=name,
    )(*args)


def _norm_mod_rows(x, gain, mod, shift_row, scale_row):
    ms = jnp.mean(x * x, axis=-1, keepdims=True)
    y = x * lax.rsqrt(ms + NORM_EPS) * gain
    return y * (1.0 + mod[scale_row:scale_row + 1]) + mod[shift_row:shift_row + 1]


def _mix_out_body(o_ref, w_ref, *refs, n_src, n_ctx_tiles):
    res_refs = refs[:n_src]
    g_ref, mod_ref, h_ref, u_ref = refs[n_src:]
    in_ctx = pl.program_id(0) < n_ctx_tiles
    mod = mod_ref[0]
    tm = o_ref.shape[0]
    rb = _pick(tm, (ROW_SUBBLOCK,))
    starts = list(range(0, tm, rb))

    def project(r0):
        return jnp.dot(o_ref[r0:r0 + rb, :], w_ref[...], preferred_element_type=F32)

    ahead = project(starts[0])
    for n, r0 in enumerate(starts):
        acc = ahead
        if n + 1 < len(starts):
            ahead = project(starts[n + 1])
        rows = slice(r0, r0 + rb)
        res = [r[rows, :] for r in res_refs]
        res = res[0] if n_src == 1 else jnp.where(in_ctx, res[0], res[1])
        h = res + mod[2:3] * acc
        h_ref[rows, :] = h
        u_ref[rows, :] = _norm_mod_rows(h, g_ref[...], mod, 3, 4).astype(u_ref.dtype)


def _mix_out(o, w, layer, res, res_row0, g, mod, lay):
    M, D = o.shape
    tm = min(lay.row_tile(), 512)
    r0 = res_row0 // tm
    row_spec = pl.BlockSpec((tm, D), lambda i: (i, 0))
    sources = _row_sources(res, lay, tm)
    res_specs = [pl.BlockSpec((tm, D), lambda i, t0=t0, nt=nt: (jnp.clip(i + r0 - t0, 0, nt - 1), 0))
                 for _, t0, nt in sources]
    return pl.pallas_call(
        functools.partial(_mix_out_body, n_src=len(sources), n_ctx_tiles=lay.n_ctx // tm),
        grid=(M // tm,),
        in_specs=[row_spec, pl.BlockSpec((None, D, D), lambda i: (layer, 0, 0))] + res_specs
                 + [pl.BlockSpec((1, D), lambda i: (0, 0)),
                    pl.BlockSpec((1, N_MOD, D), lambda i: (lay.mod_row(i, tm), 0, 0))],
        out_specs=[row_spec, row_spec],
        out_shape=[jax.ShapeDtypeStruct((M, D), F32), jax.ShapeDtypeStruct((M, D), BF16)],
        compiler_params=_params(("arbitrary",)),
        name="mix_out",
    )(o, w, *[arr for arr, _, _ in sources], g.reshape(1, D), mod)


MIX_R, MIX_W, MIX_K, MIX_V, MIX_A, MIX_G = range(6)


def _rw_pre_body(*refs, tm, n_ctx_tiles, ctx_tiles, lat_tiles, lora_mix, lora_act, n_src):
    n_lora = len(lora_mix)
    src = [refs[3 * s:3 * s + 3] for s in range(n_src)]
    g_ref, mod_ref, mu_ref = refs[3 * n_src:3 * n_src + 3]
    rest = refs[3 * n_src + 3:]
    lora_refs = rest[:n_lora]
    xr_ref, xk_ref, xv_ref = rest[n_lora:n_lora + 3]
    mid_refs = rest[n_lora + 3:]
    i = pl.program_id(0)
    in_ctx = i < n_ctx_tiles
    D = src[0][0].shape[1]

    def load(which, rows, cs):
        vals = [s[which][rows, cs] for s in src]
        return vals[0] if n_src == 1 else jnp.where(in_ctx, vals[0], vals[1])

    slabs = [slice(c, c + LANES) for c in range(0, D, LANES)]
    above = slice(SUBLANES - 1, SUBLANES)
    below = slice(0, 1)

    def inv_rms(which, rows):
        sq = None
        for cs in slabs:
            x = load(which, rows, cs)
            sq = x * x if sq is None else sq + x * x
        return lax.rsqrt(jnp.sum(sq, axis=-1, keepdims=True) * (1.0 / D) + NORM_EPS)

    r_tile, r_above, r_below = inv_rms(0, slice(None)), inv_rms(1, above), inv_rms(2, below)
    pos = jnp.where(in_ctx, i % ctx_tiles, (i - n_ctx_tiles) % lat_tiles)
    last = jnp.where(in_ctx, ctx_tiles - 1, lat_tiles - 1)
    rows = lax.broadcasted_iota(jnp.int32, (tm, 1), 0)
    acc = [None] * n_lora
    for cs in slabs:
        gain, shift, scale1 = g_ref[:, cs], mod_ref[0, 0:1, cs], 1.0 + mod_ref[0, 1:2, cs]
        u = load(0, slice(None), cs) * r_tile * gain * scale1 + shift
        u_above = jnp.where(pos == 0, 0.0, load(1, above, cs) * r_above * gain * scale1 + shift)
        u_below = jnp.where(pos == last, 0.0, load(2, below, cs) * r_below * gain * scale1 + shift)
        prev = jnp.where(rows == 0, u_above, pltpu.roll(u, 1, axis=0))
        nxt = jnp.where(rows == tm - 1, u_below, pltpu.roll(u, tm - 1, axis=0))
        xx = 0.5 * (prev + nxt) - u

        def mix(j):
            return (u + xx * mu_ref[j:j + 1, cs]).astype(BF16)

        xv = mix(MIX_V)
        xr_ref[:, cs] = mix(MIX_R)
        xk_ref[:, cs] = mix(MIX_K)
        xv_ref[:, cs] = xv
        for n, j in enumerate(lora_mix):
            part = jnp.dot(xv if j == MIX_V else mix(j), lora_refs[n][cs, :], preferred_element_type=F32)
            acc[n] = part if acc[n] is None else acc[n] + part
    for n, act in enumerate(lora_act):
        val = acc[n]
        if act == "tanh":
            val = jnp.tanh(val)
        elif act == "sigmoid":
            val = jax.nn.sigmoid(val)
        mid_refs[n][...] = val.astype(mid_refs[n].dtype)


def _row_sources(h, lay, tm):
    if not isinstance(h, tuple):
        return [(h, 0, h.shape[0] // tm)]
    n_ctx_tiles = lay.n_ctx // tm
    return [(h[0], 0, n_ctx_tiles), (h[1], n_ctx_tiles, h[1].shape[0] // tm)]


def _rw_pre(h, g, mod, mu, lora, lay):
    M, D = lay.M, mu.shape[1]
    tm = lay.seq_tile()
    per8 = tm // SUBLANES
    ranks = [w.shape[1] for _, w, _ in lora]
    row_spec = pl.BlockSpec((tm, D), lambda i: (i, 0))
    sources = _row_sources(h, lay, tm)
    src_specs, src_args = [], []
    for arr, t0, nt in sources:
        last8 = nt * per8 - 1

        def local(i, t0=t0, nt=nt):
            return jnp.clip(i - t0, 0, nt - 1)

        src_specs += [pl.BlockSpec((tm, D), lambda i, f=local: (f(i), 0)),
                      pl.BlockSpec((SUBLANES, D), lambda i, f=local: (jnp.maximum(f(i) * per8 - 1, 0), 0)),
                      pl.BlockSpec((SUBLANES, D), lambda i, f=local, l8=last8: (jnp.minimum((f(i) + 1) * per8, l8), 0))]
        src_args += [arr, arr, arr]
    return pl.pallas_call(
        functools.partial(_rw_pre_body, tm=tm, n_ctx_tiles=lay.n_ctx // tm, ctx_tiles=max(lay.C // tm, 1),
                          lat_tiles=lay.S // tm, lora_mix=tuple(j for j, _, _ in lora),
                          lora_act=tuple(a for _, _, a in lora), n_src=len(sources)),
        grid=(M // tm,),
        in_specs=src_specs
                 + [pl.BlockSpec((1, D), lambda i: (0, 0)),
                    pl.BlockSpec((1, N_MOD, D), lambda i: (lay.mod_row(i, tm), 0, 0)),
                    pl.BlockSpec(mu.shape, lambda i: (0, 0))]
                 + [pl.BlockSpec((D, n), lambda i: (0, 0)) for n in ranks],
        out_specs=[row_spec] * 3 + [pl.BlockSpec((tm, n), lambda i: (i, 0)) for n in ranks],
        out_shape=[jax.ShapeDtypeStruct((M, D), BF16)] * 3 + [jax.ShapeDtypeStruct((M, n), BF16) for n in ranks],
        compiler_params=_params(("parallel",)),
        name="rw_pre",
    )(*src_args, g.reshape(1, D), mod, mu, *[w for _, w, _ in lora])


def _wkv_body(*refs, reverse, npairs, nsub, epilogue):
    if epilogue:
        (r_ref, k_ref, v_ref, wl_ref, al_ref, w0_ref, a0_ref, kkp_ref, ka_ref,
         alo_ref, a0o_ref, yo_ref, g_ref, rk_ref, lng_ref, lnb_ref, o_ref, s_ref) = refs
    else:
        r_ref, k_ref, v_ref, wl_ref, al_ref, w0_ref, a0_ref, kkp_ref, ka_ref, o_ref, s_ref = refs
    L = WKV_CHUNK
    H = RW_HEAD

    @pl.when(pl.program_id(2) == 0)
    def _():
        s_ref[...] = jnp.zeros_like(s_ref)

    def order(row, col):
        return (row <= col) if reverse else (row >= col)

    row = lax.broadcasted_iota(jnp.int32, (L, L), 0)
    col = lax.broadcasted_iota(jnp.int32, (L, L), 1)
    tri = jnp.where(order(row, col), 1.0, 0.0).astype(BF16)
    tri2 = jnp.concatenate([tri, tri], axis=1)

    prow = lax.broadcasted_iota(jnp.int32, (L, 2 * L), 0)
    pcol = lax.broadcasted_iota(jnp.int32, (L, 2 * L), 1) & (L - 1)
    incl = order(prow, pcol)
    strict = incl & (prow != pcol)
    incl2 = jnp.concatenate([incl, incl], axis=1)
    eye = jnp.where(prow == pcol, 1.0, 0.0)

    def sibling(s):
        return ((prow // (2 * s)) == (pcol // (2 * s))) & ((prow // s) != (pcol // s))

    bmask = (lax.broadcasted_iota(jnp.int32, (2 * L, LANES), 0) // L
             == lax.broadcasted_iota(jnp.int32, (2 * L, LANES), 1) // H)
    head_ones = jnp.where(lax.broadcasted_iota(jnp.int32, (LANES, LANES), 0) // H
                          == lax.broadcasted_iota(jnp.int32, (LANES, LANES), 1) // H, 1.0, 0.0).astype(BF16)

    def bdf(x):
        return jnp.where(bmask, jnp.concatenate([x, x], axis=0), 0.0)

    def bd(x):
        return bdf(x).astype(BF16)

    def dot(a, b):
        return jnp.dot(a, b, preferred_element_type=F32)

    sigmoid = jax.nn.sigmoid

    def cat0(*xs):
        return jnp.concatenate(xs, axis=0)

    def cat1(*xs):
        return jnp.concatenate(xs, axis=1)

    prs = range(npairs)
    state = [s_ref[p] for p in prs]

    def chunk_stages(ci):
        rows = slice(ci * L, (ci + 1) * L)

        def tiles(ref):
            return [ref[rows, p * LANES:(p + 1) * LANES] for p in prs]

        def vecs(ref):
            return [ref[:, p * LANES:(p + 1) * LANES] for p in prs]

        def head_sum(xs):
            tot = dot(cat0(*xs).astype(BF16), head_ones)
            return [tot[p * L:(p + 1) * L] for p in prs]

        def store(vals):
            for p, val in zip(prs, vals):
                o_ref[rows, p * LANES:(p + 1) * LANES] = val.astype(o_ref.dtype)

        r, k, v = tiles(r_ref), tiles(k_ref), tiles(v_ref)
        ka = vecs(ka_ref)

        a = [sigmoid(a0 + al) for a0, al in zip(vecs(a0_ref), tiles(al_ref))]
        lw = [-math.exp(-0.5) * sigmoid(w0 + wl) for w0, wl in zip(vecs(w0_ref), tiles(wl_ref))]
        yield
        kraw = [k[p] * kkp for p, kkp in zip(prs, vecs(kkp_ref))]
        norm2 = head_sum([x * x for x in kraw])
        kd = [k[p] * (1.0 + (a[p] - 1.0) * ka[p]) for p in prs]
        hi = [x.astype(BF16) for x in lw]
        lo = [(lw[p] - hi[p].astype(F32)).astype(BF16) for p in prs]
        g = [dot(tri2, cat0(hi[p], lo[p])) for p in prs]
        ee = [jnp.exp(jnp.sum(x, axis=0, keepdims=True)) for x in lw]
        yield
        kk = [kraw[p] * lax.rsqrt(jnp.maximum(norm2[p], RW_L2_EPS * RW_L2_EPS)) for p in prs]
        en = [jnp.exp(-x) for x in g]
        abar = [-(kk[p] * jnp.exp(g[p] - lw[p])) for p in prs]
        rbar = [r[p] * jnp.exp(g[p]) for p in prs]
        bt = [kk[p] * a[p] * en[p] for p in prs]
        kt = [kd[p] * en[p] for p in prs]
        yield
        amat = [lax.dot_general(cat0(abar[p], rbar[p]).astype(BF16), cat0(bd(bt[p]), bd(kt[p])), _NT,
                                preferred_element_type=F32) for p in prs]
        vbd = [bd(x) for x in v]
        gam = [jnp.broadcast_to(x, (LANES, LANES)).T for x in ee]
        upd_lhs = [cat1(bdf(bt[p] * ee[p]).T, bdf(kt[p] * ee[p]).T).astype(BF16) for p in prs]
        yield
        a_ab = [jnp.where(strict, x[:L, :2 * L], 0.0) for x in amat]
        a_ak = [jnp.where(strict, x[:L, 2 * L:], 0.0).astype(BF16) for x in amat]
        a_rbk = [jnp.where(incl2, x[L:], 0.0).astype(BF16) for x in amat]
        av = [dot(a_ak[p], vbd[p]) for p in prs]

        t = [eye + jnp.where(sibling(1), x, 0.0) for x in a_ab]
        s = 2
        while s < L:
            sib = sibling(s)
            a_l = [bd(jnp.where(sib, a_ab[p], 0.0)) for p in prs]
            if s < SUBLANES:
                half = [dot(t[p].astype(BF16), a_l[p]).astype(BF16) for p in prs]
                yield
                t = [t[p] + dot(half[p], bd(t[p])) for p in prs]
            else:
                blocks = range(L // s)
                moving = [b for b in blocks if (b % 2 == 1) != reverse]

                def pick(x, n):
                    return x[n * s:(n + 1) * s]

                half = [dot(cat0(*[pick(t[p], b) for b in moving]).astype(BF16), a_l[p]).astype(BF16) for p in prs]
                yield
                upd = [dot(half[p], bd(t[p])) for p in prs]
                t = [cat0(*[pick(t[p], b) + pick(upd[p], moving.index(b)) if b in moving else pick(t[p], b)
                            for b in blocks]) for p in prs]
            yield
            s *= 2

        wu = [dot(t[p].astype(BF16), cat1(bd(abar[p]), bd(av[p]))) for p in prs]
        yield
        wr_lhs = [cat0(wu[p][:, :LANES], rbar[p]).astype(BF16) for p in prs]
        ut = [x[:, LANES:] for x in wu]
        c2_lhs = [cat0(a_rbk[p], upd_lhs[p]) for p in prs]
        yield

        wr = [dot(wr_lhs[p], state[p].astype(BF16)) for p in prs]
        yield
        u = [wr[p][:L] + ut[p] for p in prs]
        out2 = [dot(c2_lhs[p], cat0(bd(u[p]), vbd[p])) for p in prs]
        yield
        y = [wr[p][L:] + out2[p][:L] for p in prs]
        for p in prs:
            state[p] = state[p] * gam[p] + out2[p][L:]
        if not epilogue:
            store(y)
            return
        yield

        inv_n = 1.0 / H
        ytot = [y[p] + yo for p, yo in zip(prs, tiles(yo_ref))]
        mean = head_sum(ytot)
        a_o = [sigmoid(a0 + al) for a0, al in zip(vecs(a0o_ref), tiles(alo_ref))]
        kd_sum = [kd[p] + k[p] * (1.0 + (a_o[p] - 1.0) * ka[p]) for p in prs]
        bonus = head_sum([r[p] * rk * kd_sum[p] for p, rk in zip(prs, vecs(rk_ref))])
        yield
        dev = [ytot[p] - mean[p] * inv_n for p in prs]
        var = head_sum([x * x for x in dev])
        yield
        lng, lnb, gate = vecs(lng_ref), vecs(lnb_ref), tiles(g_ref)
        store([(dev[p] * lax.rsqrt(var[p] * inv_n + RW_GN_EPS) * lng[p] + lnb[p] + bonus[p] * v[p]) * gate[p]
               for p in prs])

    waiting = [chunk_stages(ci) for ci in (reversed(range(nsub)) if reverse else range(nsub))]
    running = []
    step = 0
    while waiting or running:
        if waiting and step % WKV_STAGGER == 0:
            running.append(waiting.pop(0))
        for gen in list(running):
            if next(gen, "done") == "done":
                running.remove(gen)
        step += 1
    for p in prs:
        s_ref[p] = state[p]


def _wkv(r, k, v, wl, al, w0, a0, kkp, ka, lay, *, reverse, epilogue=None):
    M, D = r.shape
    L = WKV_CHUNK
    td = min(D, 16 * LANES)
    npairs = td // LANES
    nsub = _pick(math.gcd(lay.C, lay.S) // L, (4, 2, 1))
    tb = nsub * L
    ctx_blk = lay.C // tb
    lat_blk = lay.S // tb
    ctx_total = lay.n_ctx // tb

    def tok_block(b, c):
        if reverse:
            ctx_c = ctx_blk - 1 - c
            lat_c = lat_blk - 1 - (c - ctx_blk)
        else:
            ctx_c = c
            lat_c = c - ctx_blk
        return jnp.where(c < ctx_blk, b * ctx_blk + ctx_c, ctx_total + b * lat_blk + lat_c)

    mat = pl.BlockSpec((tb, td), lambda b, d, c: (tok_block(b, c), d))
    vec = pl.BlockSpec((1, td), lambda b, d, c: (0, d))
    row = lambda x: x.reshape(1, D)
    args = [r, k, v, wl, al, row(w0), row(a0), row(kkp), row(ka)]
    specs = [mat] * 5 + [vec] * 4
    if epilogue is not None:
        al_o, a0_o, y_o, gate, rk, lnx_g, lnx_b = epilogue
        args += [al_o, row(a0_o), y_o, gate, row(rk), row(lnx_g), row(lnx_b)]
        specs += [mat, vec, mat, mat, vec, vec, vec]
    return pl.pallas_call(
        functools.partial(_wkv_body, reverse=reverse, npairs=npairs, nsub=nsub, epilogue=epilogue is not None),
        grid=(lay.B, D // td, ctx_blk + lat_blk),
        in_specs=specs,
        out_specs=mat,
        out_shape=jax.ShapeDtypeStruct((M, D), F32 if epilogue is None else BF16),
        scratch_shapes=[pltpu.VMEM((npairs, LANES, LANES), F32)],
        compiler_params=_params(("parallel", "parallel", "arbitrary")),
        name="wkv_bwd" if reverse else "wkv_fwd",
    )(*args)


ATTN_KEY_CHUNK = 1024


def _attn_body(q_ref, *refs, n_kv):
    k_refs, vt_refs = refs[:n_kv], refs[n_kv:2 * n_kv]
    o_ref = refs[2 * n_kv]
    q = q_ref[...]
    dv = vt_refs[0].shape[0]
    pieces = []
    for k_ref, vt_ref in zip(k_refs, vt_refs):
        n = k_ref.shape[0]
        for lo in range(0, n, ATTN_KEY_CHUNK):
            pieces.append((k_ref, vt_ref, lo, min(lo + ATTN_KEY_CHUNK, n)))

    def scores(piece):
        k_ref, _, lo, hi = piece
        return lax.dot_general(k_ref[lo:hi, :], q, _NT, preferred_element_type=F32)

    s_next = scores(pieces[0])
    m = l = acc = None
    for c, (_, vt_ref, lo, hi) in enumerate(pieces):
        s = s_next
        if c + 1 < len(pieces):
            s_next = scores(pieces[c + 1])
        m_c = jnp.max(s, axis=0, keepdims=True)
        m_new = m_c if m is None else jnp.maximum(m, m_c)
        p = jnp.exp2(s - m_new).astype(BF16)
        lhs = jnp.concatenate([vt_ref[:, lo:hi], jnp.ones((BF16_ROWS, hi - lo), BF16)], axis=0)
        pv = jnp.dot(lhs, p, preferred_element_type=F32)
        l_c, pv = pv[dv:dv + 1], pv[:dv]
        if m is None:
            l, acc = l_c, pv
        else:
            alpha = jnp.exp2(m - m_new)
            l, acc = alpha * l + l_c, alpha * acc + pv
        m = m_new
    o_ref[...] = (acc / l).T.astype(o_ref.dtype)


def _attn(q, k, vt, heads, lay, *, q_row0, n_q_rows, with_lat_keys):
    dk = k.shape[1] // heads
    dv = vt.shape[0] // heads
    tq = _pick(n_q_rows, (2048, 1024, 512, 256, 128, 64, 32, 16, 8))
    per_b = n_q_rows // tq
    q0 = q_row0 // tq
    kv = [(lay.C, 0)]
    k_lat, vt_lat = k, vt
    if with_lat_keys:
        kv.append((lay.S, lay.n_ctx // lay.S))
        if lay.n_ctx % lay.S:
            k_lat, vt_lat, kv[1] = k[lay.n_ctx:], vt[:, lay.n_ctx:], (lay.S, 0)
    in_specs = [pl.BlockSpec((tq, dk), lambda b, h, i: (q0 + b * per_b + i, h))]
    in_specs += [pl.BlockSpec((n, dk), functools.partial(lambda b, h, i, base: (base + b, h), base=base))
                 for n, base in kv]
    in_specs += [pl.BlockSpec((dv, n), functools.partial(lambda b, h, i, base: (h, base + b), base=base))
                 for n, base in kv]
    operands = [q, k] + ([k_lat] if with_lat_keys else []) + [vt] + ([vt_lat] if with_lat_keys else [])
    return pl.pallas_call(
        functools.partial(_attn_body, n_kv=len(kv)),
        grid=(lay.B, heads, per_b),
        in_specs=in_specs,
        out_specs=pl.BlockSpec((tq, dv), lambda b, h, i: (b * per_b + i, h)),
        out_shape=jax.ShapeDtypeStruct((lay.B * n_q_rows, heads * dv), BF16),
        compiler_params=_params(("parallel", "parallel", "arbitrary")),
        name="attn",
    )(*operands)


def _rms_rows(x, gain):
    return x * lax.rsqrt(jnp.mean(x * x, axis=-1, keepdims=True) + NORM_EPS) * gain


def _rope_lanes(x, cos, sin):
    lane = lax.broadcasted_iota(jnp.int32, x.shape, 1)
    first = (lane % MLA_ROPE) < MLA_ROPE // 2
    partner = jnp.where(first, pltpu.roll(x, LANES - MLA_ROPE // 2, axis=1), pltpu.roll(x, MLA_ROPE // 2, axis=1))
    return x * cos + partner * sin


def _mla_down_body(h_ref, g_ref, mod_ref, wkv_ref, wq_ref, gkv_ref, gr_ref, gq_ref, cos_ref, sin_ref,
                   ckv_ref, kr_ref, cq_ref, *, kv_lora):
    u = _norm_mod_rows(h_ref[...], g_ref[...], mod_ref[0], 0, 1).astype(BF16)
    acc = jnp.dot(u, wkv_ref[...], preferred_element_type=F32)
    ckv_ref[...] = _rms_rows(acc[:, :kv_lora], gkv_ref[...]).astype(ckv_ref.dtype)
    kr = acc[:, kv_lora:]
    ms = jnp.sum(kr * kr, axis=-1, keepdims=True) * (1.0 / MLA_ROPE)
    kr = kr * lax.rsqrt(ms + NORM_EPS) * gr_ref[...]
    kr_ref[...] = _rope_lanes(kr, cos_ref[...], sin_ref[...]).astype(kr_ref.dtype)
    acc_q = jnp.dot(u, wq_ref[...], preferred_element_type=F32)
    cq_ref[...] = _rms_rows(acc_q, gq_ref[...]).astype(cq_ref.dtype)


def _rms_groups(x, gain, group):
    ones = jnp.where(lax.broadcasted_iota(jnp.int32, (LANES, LANES), 0) // group
                     == lax.broadcasted_iota(jnp.int32, (LANES, LANES), 1) // group, 1.0, 0.0).astype(BF16)
    ss = jnp.dot((x * x).astype(BF16), ones, preferred_element_type=F32)
    return x * lax.rsqrt(ss * (1.0 / group) + NORM_EPS) * gain


def _mla_ukv_body(c_ref, w_ref, g_ref, kr_ref, kcat_ref, vt_ref, *, heads_per_tile):
    acc = jnp.dot(c_ref[...], w_ref[...], preferred_element_type=F32)
    kr = kr_ref[...]
    dk = MLA_NOPE + LANES
    for h in range(heads_per_tile):
        base = h * (MLA_NOPE + MLA_V)
        kcat_ref[:, h * dk:h * dk + MLA_NOPE] = _rms_groups(acc[:, base:base + MLA_NOPE], g_ref[...],
                                                            MLA_NOPE).astype(kcat_ref.dtype)
        kcat_ref[:, h * dk + MLA_NOPE:(h + 1) * dk] = kr
        vt_ref[h * MLA_V:(h + 1) * MLA_V, :] = acc[:, base + MLA_NOPE:base + MLA_NOPE + MLA_V].T.astype(vt_ref.dtype)


def _mla_uq_body(c_ref, wn_ref, wr_ref, gn_ref, gr_ref, cos_ref, sin_ref, q_ref, *, heads_per_tile, qscale):
    tm = c_ref.shape[0]
    rb = _pick(tm, (ROW_SUBBLOCK,))
    dk = MLA_NOPE + LANES
    low = lax.broadcasted_iota(jnp.int32, (rb, LANES), 1) < MLA_ROPE

    def project(r0):
        x = c_ref[r0:r0 + rb, :]
        return (jnp.dot(x, wn_ref[...], preferred_element_type=F32),
                jnp.dot(x, wr_ref[...], preferred_element_type=F32))

    def finish(r0, nope, rope):
        rows = slice(r0, r0 + rb)
        for b in range(heads_per_tile // 2):
            xr = _rms_groups(rope[:, b * LANES:(b + 1) * LANES], gr_ref[...], MLA_ROPE)
            xr = _rope_lanes(xr, cos_ref[rows, :], sin_ref[rows, :]) * qscale
            for t in range(2):
                h = 2 * b + t
                qn = _rms_groups(nope[:, h * MLA_NOPE:(h + 1) * MLA_NOPE], gn_ref[...], MLA_NOPE) * qscale
                q_ref[rows, h * dk:h * dk + MLA_NOPE] = qn.astype(q_ref.dtype)
                rr = xr if t == 0 else pltpu.roll(xr, MLA_ROPE, axis=1)
                q_ref[rows, h * dk + MLA_NOPE:(h + 1) * dk] = jnp.where(low, rr, 0.0).astype(q_ref.dtype)

    starts = list(range(0, tm, rb))
    ahead = project(starts[0])
    for n, r0 in enumerate(starts):
        cur = ahead
        if n + 1 < len(starts):
            ahead = project(starts[n + 1])
        finish(r0, *cur)


def _pad_cols(w, n):
    return jnp.pad(w, ((0, 0), (0, n - w.shape[1])))


def _pad_rows(w, n):
    return jnp.pad(w, ((0, n - w.shape[0]), (0, 0)))


def _up128(n):
    return -(-n // LANES) * LANES


def _rope_tables(lay):
    n = lay.S
    rows = n // GRID_W
    row = jnp.broadcast_to(jnp.arange(rows)[:, None], (rows, GRID_W)).reshape(-1)
    col = jnp.broadcast_to(jnp.arange(GRID_W)[None, :], (rows, GRID_W)).reshape(-1)
    n_freq = MLA_ROPE // 4
    inv = ROPE_THETA ** (-jnp.arange(n_freq, dtype=F32) / n_freq)
    ang = jnp.concatenate([row[:, None].astype(F32) * inv, col[:, None].astype(F32) * inv], axis=-1)
    cos, sin = jnp.cos(ang), jnp.sin(ang)
    reps = LANES // MLA_ROPE
    cos_l = jnp.tile(jnp.concatenate([cos, cos], axis=1), (lay.B, reps))
    sin_l = jnp.tile(jnp.concatenate([-sin, sin], axis=1), (lay.B, reps))
    cos_t = jnp.concatenate([jnp.ones((lay.n_ctx, LANES), F32), cos_l], axis=0)
    sin_t = jnp.concatenate([jnp.zeros((lay.n_ctx, LANES), F32), sin_l], axis=0)
    return cos_t, sin_t


def _lora_up_body(*refs, n):
    for x_ref, w_ref, o_ref in zip(refs[:n], refs[n:2 * n], refs[2 * n:]):
        o_ref[...] = jnp.dot(x_ref[...], w_ref[...], preferred_element_type=F32).astype(o_ref.dtype)


def _lora_up(items, lay):
    n = len(items)
    M = items[0][0].shape[0]
    N = items[0][2].shape[1]
    tm = lay.row_tile()
    tn = _pick(N, (1024, 512, 256, 128))
    x_specs = [pl.BlockSpec((tm, w.shape[0]), functools.partial(lambda j, i, blk: (i, blk), blk=blk))
               for _, blk, w in items]
    w_specs = [pl.BlockSpec((w.shape[0], tn), lambda j, i: (0, j)) for _, _, w in items]
    return pl.pallas_call(
        functools.partial(_lora_up_body, n=n),
        grid=(N // tn, M // tm),
        in_specs=x_specs + w_specs,
        out_specs=[pl.BlockSpec((tm, tn), lambda j, i: (i, j))] * n,
        out_shape=[jax.ShapeDtypeStruct((M, N), BF16)] * n,
        compiler_params=_params(("parallel", "parallel")),
        name="lora_up",
    )(*[x for x, _, _ in items], *[w for _, _, w in items])


def _rwkv_mixer(h, norm_g, mod, lay, v_first, j, big_w, pw, vres, rk, lnx_g, lnx_b):
    wr, wk, wv = big_w
    mu, w0, w1, w2, a0, a1, a2, g1, g2, kkp, ka = pw
    nw, na = _up128(w1.shape[2]), _up128(a1.shape[2])
    lora = [(MIX_W, jnp.concatenate([_pad_cols(w1[e], nw) for e in range(2)], axis=1).astype(BF16), "tanh"),
            (MIX_A, jnp.concatenate([_pad_cols(a1[e], na) for e in range(2)], axis=1).astype(BF16), None),
            (MIX_G, g1.astype(BF16), "sigmoid")]
    if vres is not None:
        v0, v1, v2 = vres
        nv = _up128(v1.shape[1])
        lora.append((MIX_V, _pad_cols(v1, nv).astype(BF16), None))
    xr, xk, xv, w_mid, a_mid, g_mid, *v_mid = _rw_pre(h, norm_g, mod, mu, lora, lay)
    mm = functools.partial(_mm, lay=lay)
    r = mm(xr, wr, layer=j, name="rw_r")
    k = mm(xk, wk, layer=j, name="rw_k")
    blend = None if vres is None else (v_first, v0, v_mid[0], _pad_rows(v2, nv).astype(BF16))
    v = mm(xv, wv, layer=j, blend=blend, name="rw_v")
    g, wl0, wl1, al0, al1 = _lora_up(
        [(g_mid, 0, g2.astype(BF16))]
        + [(w_mid, e, _pad_rows(w2[e], nw).astype(BF16)) for e in range(2)]
        + [(a_mid, e, _pad_rows(a2[e], na).astype(BF16)) for e in range(2)], lay)
    wl, al = (wl0, wl1), (al0, al1)

    y_fwd = _wkv(r, k, v, wl[0], al[0], w0[0], a0[0], kkp, ka, lay, reverse=False)
    o = _wkv(r, k, v, wl[1], al[1], w0[1], a0[1], kkp, ka, lay, reverse=True,
             epilogue=(al[0], a0[0], y_fwd, g, rk.reshape(-1), lnx_g, lnx_b))
    return o, v


def _mla_mixer(h, norm_g, mod, lay, rope_tabs, wdq, qnorm, wuq, wdkv, kvnorm, wukv, qn_nope, qn_rope, kn_nope,
               kn_rope, need_ctx):
    M, D = h.shape
    heads = D // MLA_V
    kv_lora, q_lora = kvnorm.shape[0], qnorm.shape[0]
    cos_t, sin_t = rope_tabs
    tm = lay.row_tile()
    hpt = _pick(heads, (4, 2))
    dk = MLA_NOPE + LANES
    pad = LANES - MLA_ROPE
    perm = jnp.concatenate([jnp.arange(0, MLA_ROPE, 2), jnp.arange(1, MLA_ROPE, 2)])
    row = lambda t: t.reshape(1, -1)
    par1 = _params(("parallel",))
    par2 = _params(("parallel", "parallel"))

    w_dkv = jnp.concatenate([wdkv[:, :kv_lora], wdkv[:, kv_lora:][:, perm], jnp.zeros((D, pad), F32)], axis=1)
    tmd = min(tm, 512)
    c_kv, k_rope, c_q = pl.pallas_call(
        functools.partial(_mla_down_body, kv_lora=kv_lora),
        grid=(M // tmd,),
        in_specs=[pl.BlockSpec((tmd, D), lambda i: (i, 0)),
                  pl.BlockSpec((1, D), lambda i: (0, 0)),
                  pl.BlockSpec((1, N_MOD, D), lambda i: (lay.mod_row(i, tmd), 0, 0)),
                  pl.BlockSpec((D, kv_lora + LANES), lambda i: (0, 0)),
                  pl.BlockSpec((D, q_lora), lambda i: (0, 0)),
                  pl.BlockSpec((1, kv_lora), lambda i: (0, 0)),
                  pl.BlockSpec((1, LANES), lambda i: (0, 0)),
                  pl.BlockSpec((1, q_lora), lambda i: (0, 0)),
                  pl.BlockSpec((tmd, LANES), lambda i: (i, 0)),
                  pl.BlockSpec((tmd, LANES), lambda i: (i, 0))],
        out_specs=[pl.BlockSpec((tmd, kv_lora), lambda i: (i, 0)), pl.BlockSpec((tmd, LANES), lambda i: (i, 0)),
                   pl.BlockSpec((tmd, q_lora), lambda i: (i, 0))],
        out_shape=[jax.ShapeDtypeStruct((M, kv_lora), BF16), jax.ShapeDtypeStruct((M, LANES), BF16),
                   jax.ShapeDtypeStruct((M, q_lora), BF16)],
        compiler_params=par1, name="mla_down",
    )(h, row(norm_g), mod, w_dkv.astype(BF16), wdq.astype(BF16), row(kvnorm),
      row(jnp.pad(kn_rope[perm], (0, pad))), row(qnorm), cos_t, sin_t)

    k_cat, vt = pl.pallas_call(
        functools.partial(_mla_ukv_body, heads_per_tile=hpt),
        grid=(heads // hpt, M // tm),
        in_specs=[pl.BlockSpec((tm, kv_lora), lambda j, i: (i, 0)),
                  pl.BlockSpec((kv_lora, hpt * (MLA_NOPE + MLA_V)), lambda j, i: (0, j)),
                  pl.BlockSpec((1, MLA_NOPE), lambda j, i: (0, 0)),
                  pl.BlockSpec((tm, LANES), lambda j, i: (i, 0))],
        out_specs=[pl.BlockSpec((tm, hpt * dk), lambda j, i: (i, j)),
                   pl.BlockSpec((hpt * MLA_V, tm), lambda j, i: (j, i))],
        out_shape=[jax.ShapeDtypeStruct((M, heads * dk), BF16), jax.ShapeDtypeStruct((heads * MLA_V, M), BF16)],
        compiler_params=par2, name="mla_ukv",
    )(c_kv, wukv.astype(BF16), row(kn_nope), k_rope)

    q_rows = M if need_ctx else lay.B * lay.S
    r0 = (M - q_rows) // tm
    w3 = wuq.reshape(q_lora, heads, MLA_NOPE + MLA_ROPE)
    w_nope = w3[:, :, :MLA_NOPE].reshape(q_lora, heads * MLA_NOPE)
    w_rope = w3[:, :, MLA_NOPE:][:, :, perm].reshape(q_lora, heads * MLA_ROPE)
    qscale = math.log2(math.e) / math.sqrt(MLA_NOPE + MLA_ROPE)
    q_cat = pl.pallas_call(
        functools.partial(_mla_uq_body, heads_per_tile=hpt, qscale=qscale),
        grid=(heads // hpt, q_rows // tm),
        in_specs=[pl.BlockSpec((tm, q_lora), lambda j, i: (i + r0, 0)),
                  pl.BlockSpec((q_lora, hpt * MLA_NOPE), lambda j, i: (0, j)),
                  pl.BlockSpec((q_lora, hpt * MLA_ROPE), lambda j, i: (0, j)),
                  pl.BlockSpec((1, MLA_NOPE), lambda j, i: (0, 0)),
                  pl.BlockSpec((1, LANES), lambda j, i: (0, 0)),
                  pl.BlockSpec((tm, LANES), lambda j, i: (i + r0, 0)),
                  pl.BlockSpec((tm, LANES), lambda j, i: (i + r0, 0))],
        out_specs=pl.BlockSpec((tm, hpt * dk), lambda j, i: (i, j)),
        out_shape=jax.ShapeDtypeStruct((q_rows, heads * dk), BF16),
        compiler_params=par2, name="mla_uq",
    )(c_q, w_nope.astype(BF16), w_rope.astype(BF16), row(qn_nope), row(jnp.tile(qn_rope[perm], LANES // MLA_ROPE)),
      cos_t, sin_t)

    o_lat = _attn(q_cat, k_cat, vt, heads, lay, q_row0=q_rows - lay.B * lay.S, n_q_rows=lay.S, with_lat_keys=True)
    if not need_ctx:
        return o_lat
    o_ctx = _attn(q_cat, k_cat, vt, heads, lay, q_row0=0, n_q_rows=lay.C, with_lat_keys=False)
    return jnp.concatenate([o_ctx, o_lat], axis=0)


def kernel(x, c, ctx, c_ctx, mod_w, mod_b, norm_g, mlp_w1, mlp_w2, rw_mu, rw_wr, rw_wk, rw_wv, rw_wo, rw_w0, rw_w1, rw_w2, rw_a0, rw_a1, rw_a2, rw_g1, rw_g2, rw_kk, rw_ka, rw_rk, rw_lnx_g, rw_lnx_b, rw_v0, rw_v1, rw_v2, mla_wdq, mla_qnorm, mla_wuq, mla_wdkv, mla_kvnorm, mla_wukv, mla_qn_nope, mla_qn_rope, mla_kn_nope, mla_kn_rope, mla_wo):
    B, S, D = x.shape
    C = ctx.shape[1]
    depth = mod_w.shape[0]
    lay = _Layout(B, C, S)
    rope = _rope_tables(lay)

    sc_all = jnp.concatenate([jax.nn.silu(c_ctx)[None], jax.nn.silu(c)], axis=0)
    h = (ctx.reshape(B * C, D), x.reshape(B * S, D))
    v_first = None
    mlp_w2_bf16 = mlp_w2.astype(BF16)
    rw_wo_bf16, mla_wo_bf16 = rw_wo.astype(BF16), mla_wo.astype(BF16)

    for i in range(depth):
        last = i == depth - 1
        j = i // 2
        mod = (_mm(sc_all, mod_w, layer=i, name="adaln") + mod_b[i]).reshape(B + 1, N_MOD, D)
        if i % 2 == 0:
            pw = (rw_mu[j], rw_w0[j], rw_w1[j], rw_w2[j],
                  rw_a0[j], rw_a1[j], rw_a2[j], rw_g1[j], rw_g2[j], rw_kk[j], rw_ka[j])
            vres = None if j == 0 else (rw_v0[j - 1], rw_v1[j - 1], rw_v2[j - 1])
            o, v_cur = _rwkv_mixer(h, norm_g[i, 0], mod, lay, v_first, j, (rw_wr, rw_wk, rw_wv), pw, vres,
                                   rw_rk[j], rw_lnx_g[j], rw_lnx_b[j])
            if j == 0:
                v_first = v_cur
            wo = rw_wo_bf16
        else:
            o = _mla_mixer(h, norm_g[i, 0], mod, lay, rope, mla_wdq[j], mla_qnorm[j], mla_wuq[j], mla_wdkv[j], mla_kvnorm[j],
                           mla_wukv[j], mla_qn_nope[j], mla_qn_rope[j], mla_kn_nope[j], mla_kn_rope[j],
                           need_ctx=not last)
            wo = mla_wo_bf16
        res_row0 = 0
        if last:
            if o.shape[0] != B * S:
                o = o[lay.n_ctx:]
            res_row0 = lay.n_ctx
            lay = _Layout(B, C, S, with_ctx=False)
        h, u2 = _mix_out(o, wo, j, h, res_row0, norm_g[i, 1], mod, lay)
        hid = _mm(u2, mlp_w1, layer=i, relu2=True, out_dtype=BF16, lay=lay, name="mlp_up")
        h = _mm(hid, mlp_w2_bf16, layer=i, res=h, gate=mod[:, 5], lay=lay, name="mlp_down")
    return h.reshape(B, S, D)
```
